```python
import jax, jax.numpy as jnp
from jax import lax
import numpy as np

D_MODEL = 1024
BATCH = 4
SEQ = 4096
DEPTH = 2
DEC_BATCH = 128
DEC_SEQ = 4
PAST_LEN = 2048
PAGE_SIZE = 128

N_EVEN = (DEPTH + 1) // 2
N_ODD = DEPTH // 2
A_WIDTH = D_MODEL // 4
A_KERNEL = 3
DIL_GROUPS = ((128, 1), (512, 4), (2048, 16))
N_GROUPS_B = len(DIL_GROUPS)
HEADS_PER_GROUP = 4
HEAD_DIM = 64
B_QKV = N_GROUPS_B * HEADS_PER_GROUP * HEAD_DIM
B_OUT = HEADS_PER_GROUP * HEAD_DIM
IN_COLS = 3 * A_WIDTH + 3 * B_QKV
MIX_OUT = A_WIDTH + B_OUT
C_WIDTH = D_MODEL
C_KERNEL = 31
N_EXPERT_GROUPS = 4
EXPERTS_PER_GROUP = 8
N_EXPERTS = N_EXPERT_GROUPS * EXPERTS_PER_GROUP
TOP_K_IN_GROUP = 2
D_EXPERT = D_MODEL // 4
RMS_EPS = 1e-6
LN_EPS = 1e-5
NEG_INF = -1e30

kernel_name = "hybrid_shortconv_dilated_conformer_hmoe_step"


def rms_norm(x, g):
    xf = x.astype(jnp.float32)
    y = xf * lax.rsqrt(jnp.mean(xf * xf, axis=-1, keepdims=True) + RMS_EPS)
    return (y * g.astype(jnp.float32)).astype(x.dtype)


def layer_norm(x, g, b):
    xf = x.astype(jnp.float32)
    mu = jnp.mean(xf, axis=-1, keepdims=True)
    var = jnp.mean(jnp.square(xf - mu), axis=-1, keepdims=True)
    y = (xf - mu) * lax.rsqrt(var + LN_EPS) * g.astype(jnp.float32) + b.astype(jnp.float32)
    return y.astype(x.dtype)


def causal_dwconv(x_ext, w):
    c = x_ext.shape[-1]
    return lax.conv_general_dilated(
        x_ext, w[:, None, :].astype(x_ext.dtype), window_strides=(1,), padding='VALID',
        dimension_numbers=('NWC', 'WIO', 'NWC'), feature_group_count=c)


def dilated_attn_prompt(q, k, v, window, dil):
    n, s_len, h, dh = q.shape
    nk = window // dil
    sub_len = s_len // dil
    nb = -(-sub_len // nk)
    lp = nb * nk

    def to_sub(t):
        t = t.reshape(n, sub_len, dil, h, dh).transpose(0, 2, 1, 3, 4)
        t = jnp.pad(t, ((0, 0), (0, 0), (0, lp - sub_len), (0, 0), (0, 0)))
        return t.reshape(n, dil, nb, nk, h, dh)

    def with_prev(t):
        prev = jnp.pad(t, ((0, 0), (0, 0), (1, 0), (0, 0), (0, 0), (0, 0)))[:, :, :-1]
        return jnp.concatenate([prev, t], axis=3)

    qb = to_sub(q)
    kk = with_prev(to_sub(k))
    vv = with_prev(to_sub(v))
    sc = jnp.einsum('brnqhd,brnkhd->brnhqk', qb, kk).astype(jnp.float32) * (HEAD_DIM ** -0.5)
    qi = jnp.arange(nk)[:, None]
    kc = jnp.arange(2 * nk)[None, :]
    band = (kc >= qi) & (kc <= qi + nk)
    blk = jnp.arange(nb)[:, None, None]
    mask = band[None] & ((blk > 0) | (kc[None] >= nk))
    sc = jnp.where(mask[:, None], sc, NEG_INF)
    lse = jax.nn.logsumexp(sc, axis=-1)
    p = jnp.exp(sc - lse[..., None])
    o = jnp.einsum('brnhqk,brnkhd->brnqhd', p.astype(v.dtype), vv)
    o = o.reshape(n, dil, lp, h, dh)[:, :, :sub_len].transpose(0, 2, 1, 3, 4).reshape(n, s_len, h, dh)
    lse = lse.transpose(0, 1, 2, 4, 3).reshape(n, dil, lp, h)[:, :, :sub_len]
    lse = lse.transpose(0, 2, 1, 3).reshape(n, s_len, h)
    return o, lse


def dilated_attn_sample(q, kv_new, kv_buf, window, dil):
    lbuf = kv_buf.shape[1]
    t_len = q.shape[1]
    nk = window // dil
    ext = jnp.concatenate([kv_buf, kv_new.astype(kv_buf.dtype)], axis=1)
    qi = jnp.arange(t_len)[:, None]
    kj = jnp.arange(nk + 1)[None, :]
    idx = lbuf + qi - dil * kj
    valid = (PAST_LEN + qi - dil * kj) >= 0
    g = ext[:, jnp.clip(idx, 0, None)]
    sc = jnp.einsum('nthd,ntkhd->nthk', q, g[:, :, :, 0].astype(q.dtype)).astype(jnp.float32) * (HEAD_DIM ** -0.5)
    sc = jnp.where(valid[None, :, None, :], sc, NEG_INF)
    lse = jax.nn.logsumexp(sc, axis=-1)
    p = jnp.exp(sc - lse[..., None])
    o = jnp.einsum('nthk,ntkhd->nthd', p.astype(q.dtype), g[:, :, :, 1].astype(q.dtype))
    return o, lse


def even_mixer(h, w_in, conv_a_w, w_out, conv_buf, kv_bufs):
    n, t_len, _ = h.shape
    proj = h @ w_in
    cuts = [A_WIDTH, 2 * A_WIDTH, 3 * A_WIDTH, 3 * A_WIDTH + B_QKV, 3 * A_WIDTH + 2 * B_QKV]
    gate_b, gate_c, a_in, q, k, v = jnp.split(proj, cuts, axis=-1)
    u = gate_c * a_in
    ctx = jnp.zeros((n, A_KERNEL - 1, A_WIDTH), u.dtype) if conv_buf is None else conv_buf.astype(u.dtype)
    u_ext = jnp.concatenate([ctx, u], axis=1)
    y_a = gate_b * causal_dwconv(u_ext, conv_a_w)
    new_conv = u_ext[:, -(A_KERNEL - 1):]
    q = q.reshape(n, t_len, N_GROUPS_B, HEADS_PER_GROUP, HEAD_DIM)
    kv = jnp.stack([k, v], axis=2).reshape(n, t_len, 2, N_GROUPS_B, HEADS_PER_GROUP, HEAD_DIM)
    outs, lses, new_kv = [], [], []
    for gi, (window, dil) in enumerate(DIL_GROUPS):
        kv_g = kv[:, :, :, gi]
        if kv_bufs is None:
            o, lse = dilated_attn_prompt(q[:, :, gi], kv_g[:, :, 0], kv_g[:, :, 1], window, dil)
            new_kv.append(kv_g[:, t_len - min(window, t_len):])
        else:
            o, lse = dilated_attn_sample(q[:, :, gi], kv_g, kv_bufs[gi], window, dil)
            new_kv.append(kv_g)
        outs.append(o)
        lses.append(lse)
    alpha = jax.nn.softmax(jnp.stack(lses, axis=0), axis=0)
    y_b = jnp.sum(jnp.stack(outs, axis=0).astype(jnp.float32) * alpha[..., None], axis=0)
    y_b = y_b.astype(h.dtype).reshape(n, t_len, B_OUT)
    y = jnp.concatenate([y_a, y_b], axis=-1) @ w_out
    return y, new_conv, new_kv


def odd_mixer(h, w_pw1, b_pw1, dw_w, dw_b, ln_g, ln_b, w_pw2, b_pw2, conv_buf):
    n = h.shape[0]
    z = h @ w_pw1 + b_pw1
    za, zg = jnp.split(z, 2, axis=-1)
    u = za * jax.nn.sigmoid(zg)
    ctx = jnp.zeros((n, C_KERNEL - 1, C_WIDTH), u.dtype) if conv_buf is None else conv_buf.astype(u.dtype)
    u_ext = jnp.concatenate([ctx, u], axis=1)
    c = causal_dwconv(u_ext, dw_w) + dw_b
    c = jax.nn.silu(layer_norm(c, ln_g, ln_b))
    return c @ w_pw2 + b_pw2, u_ext[:, -(C_KERNEL - 1):]


def hier_moe(h, w_rg, b_rg, w_re, b_re, w_gate, w_up, w_down):
    n, t_len, d = h.shape
    x = h.reshape(-1, d)
    gp = jax.nn.softmax((x @ w_rg + b_rg).astype(jnp.float32), axis=-1)
    g_val, g_idx = lax.top_k(gp, 1)
    el = (x @ w_re + b_re).astype(jnp.float32).reshape(-1, N_EXPERT_GROUPS, EXPERTS_PER_GROUP)
    el_sel = jnp.take_along_axis(el, g_idx[:, :, None], axis=1)[:, 0]
    e_val, e_idx = lax.top_k(jax.nn.softmax(el_sel, axis=-1), TOP_K_IN_GROUP)
    e_w = e_val / jnp.sum(e_val, axis=-1, keepdims=True) * g_val
    expert_id = g_idx * EXPERTS_PER_GROUP + e_idx
    combine = jnp.sum(jax.nn.one_hot(expert_id, N_EXPERTS, dtype=jnp.float32) * e_w[..., None], axis=1)
    hg = jnp.einsum('md,edf->mef', x, w_gate)
    hu = jnp.einsum('md,edf->mef', x, w_up)
    act = jax.nn.silu(hg) * hu * combine[..., None].astype(x.dtype)
    y = jnp.einsum('mef,efd->md', act, w_down)
    return y.reshape(n, t_len, d)


def run_trunk(x, states, p):
    new_a, new_c = [], []
    new_kv = [[] for _ in DIL_GROUPS]
    for layer in range(DEPTH):
        j = layer // 2
        h = rms_norm(x, p['g_mix'][layer])
        if layer % 2 == 0:
            conv_buf = None if states is None else states[0][j]
            kv_bufs = None if states is None else [c[j] for c in states[1]]
            y, a_state, kv_state = even_mixer(h, p['w_in'][j], p['conv_a_w'][j], p['w_out'][j], conv_buf, kv_bufs)
            new_a.append(a_state)
            for gi in range(N_GROUPS_B):
                new_kv[gi].append(kv_state[gi])
        else:
            conv_buf = None if states is None else states[2][j]
            y, c_state = odd_mixer(h, p['w_pw1'][j], p['b_pw1'][j], p['dw_w'][j], p['dw_b'][j],
                                   p['ln_g'][j], p['ln_b'][j], p['w_pw2'][j], p['b_pw2'][j], conv_buf)
            new_c.append(c_state)
        x = x + y
        x = x + hier_moe(rms_norm(x, p['g_ffn'][layer]), p['w_router_group'][layer], p['b_router_group'][layer],
                         p['w_router_expert'][layer], p['b_router_expert'][layer],
                         p['w_gate'][layer], p['w_up'][layer], p['w_down'][layer])
    y = rms_norm(x, p['g_final'])
    return (y, jnp.stack(new_a), jnp.stack(new_kv[0]), jnp.stack(new_kv[1]), jnp.stack(new_kv[2]), jnp.stack(new_c))


def setup_inputs(seed: int = 0) -> dict:
    key = jax.random.key(seed)
    ks = jax.random.split(key, 32)
    f32 = jnp.float32

    def nrm(k, shape, scale):
        return jax.random.normal(k, shape, f32) * scale

    kv_shape = lambda w: (N_EVEN, DEC_BATCH, min(w, PAST_LEN), 2, HEADS_PER_GROUP, HEAD_DIM)
    return {
        'x_prompt': nrm(ks[0], (BATCH, SEQ, D_MODEL), 1.0),
        'x_sample': nrm(ks[1], (DEC_BATCH, DEC_SEQ, D_MODEL), 1.0),
        'state_shortconv': nrm(ks[2], (N_EVEN, DEC_BATCH, A_KERNEL - 1, A_WIDTH), 1.0),
        'cache_kv_w128': nrm(ks[3], kv_shape(DIL_GROUPS[0][0]), 1.0),
        'cache_kv_w512': nrm(ks[4], kv_shape(DIL_GROUPS[1][0]), 1.0),
        'cache_kv_w2048': nrm(ks[5], kv_shape(DIL_GROUPS[2][0]), 1.0),
        'state_conformer': nrm(ks[6], (N_ODD, DEC_BATCH, C_KERNEL - 1, C_WIDTH), 0.5),
        'g_mix': 1.0 + nrm(ks[7], (DEPTH, D_MODEL), 0.02),
        'g_ffn': 1.0 + nrm(ks[8], (DEPTH, D_MODEL), 0.02),
        'g_final': 1.0 + nrm(ks[9], (D_MODEL,), 0.02),
        'w_in': nrm(ks[10], (N_EVEN, D_MODEL, IN_COLS), D_MODEL ** -0.5),
        'conv_a_w': nrm(ks[11], (N_EVEN, A_KERNEL, A_WIDTH), A_KERNEL ** -0.5),
        'w_out': nrm(ks[12], (N_EVEN, MIX_OUT, D_MODEL), MIX_OUT ** -0.5),
        'w_pw1': nrm(ks[13], (N_ODD, D_MODEL, 2 * C_WIDTH), D_MODEL ** -0.5),
        'b_pw1': nrm(ks[14], (N_ODD, 2 * C_WIDTH), 0.02),
        'dw_w': nrm(ks[15], (N_ODD, C_KERNEL, C_WIDTH), C_KERNEL ** -0.5),
        'dw_b': nrm(ks[16], (N_ODD, C_WIDTH), 0.02),
        'ln_g': 1.0 + nrm(ks[17], (N_ODD, C_WIDTH), 0.02),
        'ln_b': nrm(ks[18], (N_ODD, C_WIDTH), 0.02),
        'w_pw2': nrm(ks[19], (N_ODD, C_WIDTH, D_MODEL), C_WIDTH ** -0.5),
        'b_pw2': nrm(ks[20], (N_ODD, D_MODEL), 0.02),
        'w_router_group': nrm(ks[21], (DEPTH, D_MODEL, N_EXPERT_GROUPS), D_MODEL ** -0.5),
        'b_router_group': nrm(ks[22], (DEPTH, N_EXPERT_GROUPS), 0.01),
        'w_router_expert': nrm(ks[23], (DEPTH, D_MODEL, N_EXPERTS), D_MODEL ** -0.5),
        'b_router_expert': nrm(ks[24], (DEPTH, N_EXPERTS), 0.01),
        'w_gate': nrm(ks[25], (DEPTH, N_EXPERTS, D_MODEL, D_EXPERT), D_MODEL ** -0.5),
        'w_up': nrm(ks[26], (DEPTH, N_EXPERTS, D_MODEL, D_EXPERT), D_MODEL ** -0.5),
        'w_down': nrm(ks[27], (DEPTH, N_EXPERTS, D_EXPERT, D_MODEL), D_EXPERT ** -0.5),
    }


def reference(x_prompt, x_sample, state_shortconv, cache_kv_w128, cache_kv_w512, cache_kv_w2048, state_conformer,
              g_mix, g_ffn, g_final, w_in, conv_a_w, w_out, w_pw1, b_pw1, dw_w, dw_b, ln_g, ln_b, w_pw2, b_pw2,
              w_router_group, b_router_group, w_router_expert, b_router_expert, w_gate, w_up, w_down):
    p = {'g_mix': g_mix, 'g_ffn': g_ffn, 'g_final': g_final, 'w_in': w_in, 'conv_a_w': conv_a_w, 'w_out': w_out,
         'w_pw1': w_pw1, 'b_pw1': b_pw1, 'dw_w': dw_w, 'dw_b': dw_b, 'ln_g': ln_g, 'ln_b': ln_b,
         'w_pw2': w_pw2, 'b_pw2': b_pw2, 'w_router_group': w_router_group, 'b_router_group': b_router_group,
         'w_router_expert': w_router_expert, 'b_router_expert': b_router_expert,
         'w_gate': w_gate, 'w_up': w_up, 'w_down': w_down}
    y_prompt, p_sc, p_kv0, p_kv1, p_kv2, p_cf = run_trunk(x_prompt, None, p)
    sample_states = (state_shortconv, [cache_kv_w128, cache_kv_w512, cache_kv_w2048], state_conformer)
    y_sample, s_sc, s_kv0, s_kv1, s_kv2, s_cf = run_trunk(x_sample, sample_states, p)
    return (y_prompt, y_sample, p_sc, p_kv0, p_kv1, p_kv2, p_cf, s_sc, s_kv0, s_kv1, s_kv2, s_cf)
```

```python
import functools

import jax
import jax.numpy as jnp
from jax import lax
from jax.experimental import pallas as pl
from jax.experimental.pallas import tpu as pltpu

F32 = jnp.float32
BF16 = jnp.bfloat16

DIL_GROUPS = ((128, 1), (512, 4), (2048, 16))
N_GROUPS = len(DIL_GROUPS)
HEADS = 4
HEAD_DIM = 64
GROUP_W = HEADS * HEAD_DIM
NK = DIL_GROUPS[0][0] // DIL_GROUPS[0][1]
A_KERNEL = 3
C_KERNEL = 31
N_EXPERT_GROUPS = 4
EXPERTS_PER_GROUP = 8
N_EXPERTS = N_EXPERT_GROUPS * EXPERTS_PER_GROUP
RMS_EPS = 1e-6
LN_EPS = 1e-5
NEG_INF = -1e30

SUBLANES = 8
LANES = 128
VMEM_LIMIT = 48 * 1024 * 1024

ROW_TILE = 256
EXPERT_TILE = 256
CONV_HALO = 32
SEQ_BLOCK = 8
ATTN_SEQ_BLOCK = 2


def _params(*sem):
    return pltpu.CompilerParams(dimension_semantics=sem, vmem_limit_bytes=VMEM_LIMIT)


def _rms(x, g):
    return x * lax.rsqrt(jnp.mean(x * x, axis=-1, keepdims=True) + RMS_EPS) * g


def _dot(a, b):
    return jnp.dot(a, b, preferred_element_type=F32)


def _rows_to_lanes(ref, n_rows, base=None):
    parts = []
    for s in range(SUBLANES):
        idx = pl.ds(s, n_rows, stride=SUBLANES)
        parts.append(ref[idx, :] if base is None else ref[base, idx, :])
    return jnp.concatenate(parts, axis=-1)


def _lanes_to_rows(ref, val, base=None):
    n_rows = val.shape[0]
    for s in range(SUBLANES):
        idx = pl.ds(s, n_rows, stride=SUBLANES)
        piece = val[:, s * LANES:(s + 1) * LANES]
        if base is None:
            ref[idx, :] = piece
        else:
            ref[base, idx, :] = piece


def _inproj_kernel(n_prompt_tiles, xp_ref, xs_ref, g_ref, w_ref, q_ref, kv_ref, gbu_ref):
    i = pl.program_id(0)
    x = jnp.where(i < n_prompt_tiles, xp_ref[...], xs_ref[...])
    h = _rms(x, g_ref[...]).astype(BF16)
    aw = 3 * GROUP_W
    a = _dot(h, w_ref[:, 0:aw])
    gbu_ref[:, 0:GROUP_W] = a[:, 0:GROUP_W]
    gbu_ref[:, GROUP_W:2 * GROUP_W] = a[:, GROUP_W:2 * GROUP_W] * a[:, 2 * GROUP_W:aw]
    qw = N_GROUPS * GROUP_W
    _store_chunks(q_ref, _dot(h, w_ref[:, aw:aw + qw]) * (HEAD_DIM ** -0.5))
    _store_chunks(kv_ref, _dot(h, w_ref[:, aw + qw:]))


def _store_chunks(ref, val):
    for c in range(val.shape[1] // LANES):
        ref[c] = val[:, c * LANES:(c + 1) * LANES]


def _load_chunks(ref, rows=None):
    n = ref.shape[0]
    return jnp.concatenate([ref[c] if rows is None else ref[c, rows, :] for c in range(n)], axis=-1)


def _inproj(xp, xs, g, w):
    mp, d = xp.shape
    ms = xs.shape[0]
    tm = ROW_TILE
    npt, nst = mp // tm, ms // tm
    m = mp + ms
    ncols = w.shape[1]
    qw = N_GROUPS * GROUP_W
    return pl.pallas_call(
        functools.partial(_inproj_kernel, npt),
        grid=(npt + nst,),
        in_specs=[
            pl.BlockSpec((tm, d), lambda i: (jnp.minimum(i, npt - 1), 0)),
            pl.BlockSpec((tm, d), lambda i: (jnp.maximum(i - npt, 0), 0)),
            pl.BlockSpec((1, d), lambda i: (0, 0)),
            pl.BlockSpec((d, ncols), lambda i: (0, 0)),
        ],
        out_specs=[
            pl.BlockSpec((qw // LANES, tm, LANES), lambda i: (0, i, 0)),
            pl.BlockSpec((2 * qw // LANES, tm, LANES), lambda i: (0, i, 0)),
            pl.BlockSpec((tm, 2 * GROUP_W), lambda i: (i, 0)),
        ],
        out_shape=[
            jax.ShapeDtypeStruct((qw // LANES, m, LANES), F32),
            jax.ShapeDtypeStruct((2 * qw // LANES, m, LANES), F32),
            jax.ShapeDtypeStruct((m, 2 * GROUP_W), F32),
        ],
        compiler_params=_params("arbitrary"),
        name="inproj",
    )(xp, xs, g, w)


def _attn_prompt_kernel(q_ref, kv_ref, o_ref, m_scr, l_scr, acc_scr):
    g = pl.program_id(1)
    seq = q_ref.shape[1]

    @pl.when(g == 0)
    def _():
        m_scr[...] = jnp.full(m_scr.shape, NEG_INF, F32)
        l_scr[...] = jnp.zeros(l_scr.shape, F32)
        acc_scr[...] = jnp.zeros(acc_scr.shape, F32)

    qi = lax.broadcasted_iota(jnp.int32, (NK, 2 * NK), 0)
    kc = lax.broadcasted_iota(jnp.int32, (NK, 2 * NK), 1)
    band = (kc >= qi) & (kc <= qi + NK)
    lane = lax.broadcasted_iota(jnp.int32, (NK, LANES), 1)

    def block(blk, d, last):
        span = NK * d
        base = pl.multiple_of(blk * span, span)
        pbase = pl.multiple_of(jnp.maximum(blk - 1, 0) * span, span)
        mask = band & ((blk > 0) | (kc >= NK))
        for r in range(d):
            rows = pl.ds(base + r, NK, stride=d) if d > 1 else pl.ds(base, NK)
            prows = pl.ds(pbase + r, NK, stride=d) if d > 1 else pl.ds(pbase, NK)
            qb = _load_chunks(q_ref, rows).astype(BF16)
            kvc = _load_chunks(kv_ref, rows)
            kvp = _load_chunks(kv_ref, prows)
            kcat = jnp.concatenate([kvp[:, :GROUP_W], kvc[:, :GROUP_W]], axis=0).astype(BF16)
            vcat = jnp.concatenate([kvp[:, GROUP_W:], kvc[:, GROUP_W:]], axis=0).astype(BF16)
            m_old = m_scr[rows, :]
            l_old = l_scr[rows, :]
            acc_old = _load_chunks(acc_scr, rows)
            m_new, l_new, outs = m_old, l_old, []
            for h in range(HEADS):
                cols = slice(h * HEAD_DIM, (h + 1) * HEAD_DIM)
                s = lax.dot_general(qb[:, cols], kcat[:, cols], (((1,), (1,)), ((), ())),
                                    preferred_element_type=F32)
                s = jnp.where(mask, s, NEG_INF)
                mo = m_old[:, h:h + 1]
                mn = jnp.maximum(mo, jnp.max(s, axis=-1, keepdims=True))
                p = jnp.exp(s - mn)
                alpha = jnp.exp(mo - mn)
                ln = alpha * l_old[:, h:h + 1] + jnp.sum(p, axis=-1, keepdims=True)
                acc = alpha * acc_old[:, cols] + _dot(p.astype(BF16), vcat[:, cols])
                if last:
                    outs.append(acc / ln)
                else:
                    outs.append(acc)
                    m_new = jnp.where(lane == h, mn, m_new)
                    l_new = jnp.where(lane == h, ln, l_new)
            res = jnp.concatenate(outs, axis=-1)
            dst = o_ref if last else acc_scr
            for c in range(GROUP_W // LANES):
                dst[c, rows, :] = res[:, c * LANES:(c + 1) * LANES]
            if not last:
                m_scr[rows, :] = m_new
                l_scr[rows, :] = l_new

    for gi, (window, d) in enumerate(DIL_GROUPS):
        @pl.when(g == gi)
        def _(d=d, last=(gi == N_GROUPS - 1)):
            def body(blk, carry):
                block(blk, d, last)
                return carry
            lax.fori_loop(0, seq // (NK * d), body, 0)


def _attn_prompt(q, kv, batch, seq):
    assert seq % (NK * max(d for _, d in DIL_GROUPS)) == 0
    return pl.pallas_call(
        _attn_prompt_kernel,
        grid=(batch, N_GROUPS),
        in_specs=[
            pl.BlockSpec((GROUP_W // LANES, seq, LANES), lambda b, g: (g, b, 0)),
            pl.BlockSpec((2 * GROUP_W // LANES, seq, LANES), lambda b, g: (g, b, 0)),
        ],
        out_specs=pl.BlockSpec((GROUP_W // LANES, seq, LANES), lambda b, g: (0, b, 0)),
        out_shape=jax.ShapeDtypeStruct((GROUP_W // LANES, batch * seq, LANES), F32),
        scratch_shapes=[
            pltpu.VMEM((seq, LANES), F32),
            pltpu.VMEM((seq, LANES), F32),
            pltpu.VMEM((GROUP_W // LANES, seq, LANES), F32),
        ],
        compiler_params=_params("arbitrary", "arbitrary"),
        name="attn_prompt",
    )(q, kv)


def _shortconv_prompt_kernel(tiles_per_seq, gbu_ref, halo_ref, w_ref, o_ref, scr):
    i = pl.program_id(0)
    tm = gbu_ref.shape[0]
    gb = gbu_ref[:, 0:GROUP_W]
    u = gbu_ref[:, GROUP_W:]
    first = (i % tiles_per_seq) == 0
    scr[0:SUBLANES, :] = jnp.where(first, 0.0, halo_ref[:, GROUP_W:])
    scr[SUBLANES:, :] = u
    w = w_ref[...]
    conv = (w[0:1, :] * scr[pl.ds(SUBLANES - 2, tm), :] + w[1:2, :] * scr[pl.ds(SUBLANES - 1, tm), :]
            + w[2:3, :] * u)
    o_ref[...] = gb * conv


def _shortconv_prompt(gbu, w, mp, seq):
    tm = ROW_TILE
    return pl.pallas_call(
        functools.partial(_shortconv_prompt_kernel, seq // tm),
        grid=(mp // tm,),
        in_specs=[
            pl.BlockSpec((tm, 2 * GROUP_W), lambda i: (i, 0)),
            pl.BlockSpec((SUBLANES, 2 * GROUP_W), lambda i: (jnp.maximum(i * (tm // SUBLANES) - 1, 0), 0)),
            pl.BlockSpec((A_KERNEL, GROUP_W), lambda i: (0, 0)),
        ],
        out_specs=pl.BlockSpec((tm, GROUP_W), lambda i: (i, 0)),
        out_shape=jax.ShapeDtypeStruct((mp, GROUP_W), F32),
        scratch_shapes=[pltpu.VMEM((tm + SUBLANES, GROUP_W), F32)],
        compiler_params=_params("arbitrary"),
        name="shortconv_prompt",
    )(gbu, gbu, w)


def _mix_sample_kernel(q_ref, kvn_ref, gbu_ref, st_ref, c0_ref, c1_ref, c2_ref, w_ref,
                       yb_ref, ya_ref, ns_ref):
    sb, t_len = q_ref.shape[0], q_ref.shape[1]
    w = w_ref[...]
    n0 = c0_ref.shape[1]
    key_idx = lax.broadcasted_iota(jnp.int32, (n0 + t_len, HEADS, 1), 0)

    def per_seq(s, carry):
        k0 = jnp.concatenate([c0_ref[s, :, 0], kvn_ref[s, :, 0, 0]], axis=0)
        v0 = jnp.concatenate([c0_ref[s, :, 1], kvn_ref[s, :, 0, 1]], axis=0)
        for i in range(t_len):
            scores, vals = [], []
            s0 = jnp.sum(k0 * q_ref[s, i, 0][None], axis=-1, keepdims=True)
            s0 = jnp.where((key_idx >= i) & (key_idx <= n0 + i), s0, NEG_INF)
            scores.append(s0)
            vals.append(v0)
            for gi, c_ref in ((1, c1_ref), (2, c2_ref)):
                kg = jnp.concatenate([c_ref[s, :, i, 0], kvn_ref[s, i, gi, 0][None]], axis=0)
                vg = jnp.concatenate([c_ref[s, :, i, 1], kvn_ref[s, i, gi, 1][None]], axis=0)
                scores.append(jnp.sum(kg * q_ref[s, i, gi][None], axis=-1, keepdims=True))
                vals.append(vg)
            sc = jnp.concatenate(scores, axis=0)
            vv = jnp.concatenate(vals, axis=0)
            mx = jnp.max(sc, axis=0, keepdims=True)
            p = jnp.exp(sc - mx)
            den = jnp.sum(p, axis=0, keepdims=True)
            o = jnp.sum(p * vv, axis=0, keepdims=True) / den
            yb_ref[s, i] = o[0]
        gbu = gbu_ref[s]
        gb = gbu[:, 0:GROUP_W]
        u = gbu[:, GROUP_W:]
        st = st_ref[s]
        ext = [st[0:1], st[1:2]] + [u[t:t + 1] for t in range(t_len)]
        for t in range(t_len):
            conv = w[0:1] * ext[t] + w[1:2] * ext[t + 1] + w[2:3] * ext[t + 2]
            ya_ref[s, t:t + 1, :] = gb[t:t + 1] * conv
        for r in range(A_KERNEL - 1):
            ns_ref[s, r:r + 1, :] = ext[t_len + r]
        return carry

    lax.fori_loop(0, sb, per_seq, 0)


def _mix_sample(q5, kvn, gbu3, st, c0, c1, c2, w):
    n, t_len = q5.shape[0], q5.shape[1]
    sb = ATTN_SEQ_BLOCK
    assert t_len >= A_KERNEL - 1 and n % sb == 0
    res1 = c1.shape[2]
    assert res1 == t_len and c2.shape[2] >= t_len
    slab = (HEADS, HEAD_DIM)
    return pl.pallas_call(
        _mix_sample_kernel,
        grid=(n // sb,),
        in_specs=[
            pl.BlockSpec((sb, t_len, N_GROUPS) + slab, lambda i: (i, 0, 0, 0, 0)),
            pl.BlockSpec((sb, t_len, N_GROUPS, 2) + slab, lambda i: (i, 0, 0, 0, 0, 0)),
            pl.BlockSpec((sb, t_len, 2 * GROUP_W), lambda i: (i, 0, 0)),
            pl.BlockSpec((sb, A_KERNEL - 1, GROUP_W), lambda i: (i, 0, 0)),
            pl.BlockSpec((sb, c0.shape[1], 2) + slab, lambda i: (i, 0, 0, 0, 0)),
            pl.BlockSpec((sb, c1.shape[1], t_len, 2) + slab, lambda i: (i, 0, 0, 0, 0, 0)),
            pl.BlockSpec((sb, c2.shape[1], t_len, 2) + slab, lambda i: (i, 0, 0, 0, 0, 0)),
            pl.BlockSpec((A_KERNEL, GROUP_W), lambda i: (0, 0)),
        ],
        out_specs=[
            pl.BlockSpec((sb, t_len) + slab, lambda i: (i, 0, 0, 0)),
            pl.BlockSpec((sb, t_len, GROUP_W), lambda i: (i, 0, 0)),
            pl.BlockSpec((sb, A_KERNEL - 1, GROUP_W), lambda i: (i, 0, 0)),
        ],
        out_shape=[
            jax.ShapeDtypeStruct((n, t_len) + slab, F32),
            jax.ShapeDtypeStruct((n, t_len, GROUP_W), F32),
            jax.ShapeDtypeStruct((n, A_KERNEL - 1, GROUP_W), F32),
        ],
        compiler_params=_params("arbitrary"),
        name="mix_sample",
    )(q5, kvn, gbu3, st, c0, c1, c2, w)


def _route(h2, wr_ref, br_ref, h2rows_ref, rinfo_ref):
    _lanes_to_rows(h2rows_ref, h2)
    logits = jnp.dot(h2, wr_ref[...], preferred_element_type=F32,
                     precision=lax.Precision.HIGHEST) + br_ref[...]
    li = lax.broadcasted_iota(jnp.int32, logits.shape, 1)
    gmask = li < N_EXPERT_GROUPS
    gl = jnp.where(gmask, logits, NEG_INF)
    ge = jnp.exp(gl - jnp.max(gl, axis=-1, keepdims=True))
    gp = ge / jnp.sum(ge, axis=-1, keepdims=True)
    g_val = jnp.max(gp, axis=-1, keepdims=True)
    g_idx = jnp.min(jnp.where(gmask & (gp == g_val), li, LANES), axis=-1, keepdims=True)
    sel = (li >= N_EXPERT_GROUPS) & (((li - N_EXPERT_GROUPS) // EXPERTS_PER_GROUP) == g_idx) \
        & (li < N_EXPERT_GROUPS + N_EXPERTS)
    el = jnp.where(sel, logits, NEG_INF)
    ee = jnp.exp(el - jnp.max(el, axis=-1, keepdims=True))
    ep = jnp.where(sel, ee / jnp.sum(ee, axis=-1, keepdims=True), -1.0)
    v1 = jnp.max(ep, axis=-1, keepdims=True)
    i1 = jnp.min(jnp.where(ep == v1, li, LANES), axis=-1, keepdims=True)
    ep2 = jnp.where(li == i1, -1.0, ep)
    v2 = jnp.max(ep2, axis=-1, keepdims=True)
    i2 = jnp.min(jnp.where(ep2 == v2, li, LANES), axis=-1, keepdims=True)
    scale = g_val / (v1 + v2)
    id1 = (i1 - N_EXPERT_GROUPS).astype(F32)
    id2 = (i2 - N_EXPERT_GROUPS).astype(F32)
    rinfo_ref[...] = jnp.where(li == 0, v1 * scale, jnp.where(li == 1, v2 * scale,
                               jnp.where(li == 2, id1, jnp.where(li == 3, id2, 0.0))))


def _router_specs(tm, d):
    ins = [pl.BlockSpec((1, d), lambda i: (0, 0)),
           pl.BlockSpec((d, LANES), lambda i: (0, 0)),
           pl.BlockSpec((1, LANES), lambda i: (0, 0))]
    outs = [pl.BlockSpec((tm * SUBLANES, LANES), lambda i: (i, 0)),
            pl.BlockSpec((tm, LANES), lambda i: (i, 0))]
    return ins, outs


def _router_shapes(m):
    return [jax.ShapeDtypeStruct((m * SUBLANES, LANES), F32), jax.ShapeDtypeStruct((m, LANES), F32)]


def _outproj_kernel(n_prompt_tiles, xp_ref, xs_ref, ya_ref, yb_ref, ys_ref, wo_ref, g_ref, wr_ref, br_ref,
                    x1_ref, h2rows_ref, rinfo_ref):
    i = pl.program_id(0)
    is_p = i < n_prompt_tiles
    x = jnp.where(is_p, xp_ref[...], xs_ref[...])
    ymix = jnp.where(is_p, jnp.concatenate([ya_ref[...], _load_chunks(yb_ref)], axis=-1), ys_ref[...])
    x1 = x + _dot(ymix.astype(BF16), wo_ref[...])
    x1_ref[...] = x1
    _route(_rms(x1, g_ref[...]), wr_ref, br_ref, h2rows_ref, rinfo_ref)


def _outproj(xp, xs, ya_p, yb_p, ymix_s, wo, g, wr, br):
    mp, d = xp.shape
    ms = xs.shape[0]
    tm = ROW_TILE
    npt, nst = mp // tm, ms // tm
    m = mp + ms
    r_in, r_out = _router_specs(tm, d)
    pmap = lambda i: (jnp.minimum(i, npt - 1), 0)
    smap = lambda i: (jnp.maximum(i - npt, 0), 0)
    return pl.pallas_call(
        functools.partial(_outproj_kernel, npt),
        grid=(npt + nst,),
        in_specs=[
            pl.BlockSpec((tm, d), pmap),
            pl.BlockSpec((tm, d), smap),
            pl.BlockSpec((tm, GROUP_W), pmap),
            pl.BlockSpec((GROUP_W // LANES, tm, LANES), lambda i: (0, jnp.minimum(i, npt - 1), 0)),
            pl.BlockSpec((tm, 2 * GROUP_W), smap),
            pl.BlockSpec((2 * GROUP_W, d), lambda i: (0, 0)),
        ] + r_in,
        out_specs=[pl.BlockSpec((tm, d), lambda i: (i, 0))] + r_out,
        out_shape=[jax.ShapeDtypeStruct((m, d), F32)] + _router_shapes(m),
        compiler_params=_params("arbitrary"),
        name="outproj_router",
    )(xp, xs, ya_p, yb_p, ymix_s, wo, g, wr, br)


def _experts_kernel(te_ref, cnt_ref, asg_ref, nt_ref, h_hbm, wg_ref, wu_ref, wd_ref, y_hbm, xbuf, ybuf, gsem, ssem,
                    *, n_tokens):
    i = pl.program_id(0)
    nt = nt_ref[0]
    te = EXPERT_TILE
    slot = i % 2

    def row_ds(r):
        return pl.ds(pl.multiple_of(r * SUBLANES, SUBLANES), SUBLANES)

    def gather(tile, sl):
        def body(j, c):
            tok = asg_ref[tile * te + j] >> 1
            pltpu.make_async_copy(h_hbm.at[row_ds(tok), :], xbuf.at[sl, row_ds(j), :], gsem.at[sl]).start()
            return c
        lax.fori_loop(0, te, body, 0)

    def scatter(tile, sl):
        def body(j, c):
            a = asg_ref[tile * te + j]
            row = (a & 1) * n_tokens + (a >> 1)
            pltpu.make_async_copy(ybuf.at[sl, row_ds(j), :], y_hbm.at[row_ds(row), :], ssem.at[sl]).start()
            return c
        lax.fori_loop(0, cnt_ref[tile], body, 0)

    def wait_gather(sl):
        pltpu.make_async_copy(h_hbm.at[pl.ds(0, te * SUBLANES), :], xbuf.at[sl], gsem.at[sl]).wait()

    def wait_scatter(tile, sl):
        rows = pl.ds(0, pl.multiple_of(cnt_ref[tile] * SUBLANES, SUBLANES))
        pltpu.make_async_copy(ybuf.at[sl, rows, :], y_hbm.at[rows, :], ssem.at[sl]).wait()

    @pl.when(i < nt)
    def _():
        @pl.when(i == 0)
        def _():
            gather(0, 0)

        @pl.when(i + 1 < nt)
        def _():
            gather(i + 1, 1 - slot)

        wait_gather(slot)
        x = _rows_to_lanes(xbuf, te, base=slot).astype(BF16)
        gate = _dot(x, wg_ref[...].astype(BF16))
        up = _dot(x, wu_ref[...].astype(BF16))
        act = (gate * jax.nn.sigmoid(gate) * up).astype(BF16)
        y = _dot(act, wd_ref[...].astype(BF16))
        _lanes_to_rows(ybuf, y, base=slot)
        scatter(i, slot)

        @pl.when(i >= 1)
        def _():
            wait_scatter(i - 1, 1 - slot)

        @pl.when(i == nt - 1)
        def _():
            wait_scatter(i, slot)


def _experts(h2rows, tile_expert, tile_cnt, asg, n_tiles, wg, wu, wd, n_tokens):
    te = EXPERT_TILE
    nt_max = tile_expert.shape[0]
    d, f = wg.shape[1], wg.shape[2]
    n_rows = 2 * n_tokens
    wmap = lambda i, te_ref, cnt_ref, asg_ref, nt_ref: (te_ref[i], 0, 0)
    grid_spec = pltpu.PrefetchScalarGridSpec(
        num_scalar_prefetch=4,
        grid=(nt_max,),
        in_specs=[
            pl.BlockSpec(memory_space=pl.ANY),
            pl.BlockSpec((None, d, f), wmap),
            pl.BlockSpec((None, d, f), wmap),
            pl.BlockSpec((None, f, d), wmap),
        ],
        out_specs=pl.BlockSpec(memory_space=pl.ANY),
        scratch_shapes=[
            pltpu.VMEM((2, te * SUBLANES, LANES), F32),
            pltpu.VMEM((2, te * SUBLANES, LANES), F32),
            pltpu.SemaphoreType.DMA((2,)),
            pltpu.SemaphoreType.DMA((2,)),
        ],
    )
    return pl.pallas_call(
        functools.partial(_experts_kernel, n_tokens=n_tokens),
        grid_spec=grid_spec,
        out_shape=jax.ShapeDtypeStruct((n_rows * SUBLANES, LANES), F32),
        compiler_params=_params("arbitrary"),
        name="experts",
    )(tile_expert, tile_cnt, asg, n_tiles, h2rows, wg, wu, wd)


def _plan_tiles(rinfo):
    m = rinfo.shape[0]
    te = EXPERT_TILE
    n_asg = 2 * m
    nt_max = n_asg // te + N_EXPERTS
    ids = rinfo[:, 2:4].astype(jnp.int32).reshape(n_asg)
    key_sorted, order = lax.sort((ids, lax.iota(jnp.int32, n_asg)), num_keys=1, is_stable=True)
    counts = jnp.sum((ids[:, None] == jnp.arange(N_EXPERTS)[None, :]).astype(jnp.int32), axis=0)
    tiles = (counts + te - 1) // te
    tile_end = jnp.cumsum(tiles)
    tile_start = tile_end - tiles
    row_start = jnp.cumsum(counts) - counts
    n_tiles = tile_end[-1]
    tidx = jnp.arange(nt_max, dtype=jnp.int32)
    tile_expert = jnp.sum((tidx[:, None] >= tile_end[None, :]).astype(jnp.int32), axis=1)
    last_expert = jnp.max(jnp.where(counts > 0, jnp.arange(N_EXPERTS), 0))
    tile_expert = jnp.where(tidx < n_tiles, jnp.minimum(tile_expert, N_EXPERTS - 1), last_expert).astype(jnp.int32)
    within = (tidx - tile_start[tile_expert])[:, None] * te + jnp.arange(te, dtype=jnp.int32)[None, :]
    valid = (within < counts[tile_expert][:, None]) & (tidx < n_tiles)[:, None]
    pos = jnp.clip(row_start[tile_expert][:, None] + within, 0, n_asg - 1)
    asg = jnp.where(valid, order[pos], 0).astype(jnp.int32).reshape(nt_max * te)
    tile_cnt = jnp.sum(valid.astype(jnp.int32), axis=1)
    return tile_expert, tile_cnt, asg, n_tiles.astype(jnp.int32).reshape(1)


def _combine(x_ref, ya_ref, yb_ref, rinfo_ref):
    tm = x_ref.shape[0]
    r = rinfo_ref[...]
    return x_ref[...] + r[:, 0:1] * _rows_to_lanes(ya_ref, tm) + r[:, 1:2] * _rows_to_lanes(yb_ref, tm)


def _combine_specs(tm, d, m):
    blocks_per_pick = m // tm
    return [
        pl.BlockSpec((tm, d), lambda i: (i, 0)),
        pl.BlockSpec((tm * SUBLANES, LANES), lambda i: (i, 0)),
        pl.BlockSpec((tm * SUBLANES, LANES), lambda i: (blocks_per_pick + i, 0)),
        pl.BlockSpec((tm, LANES), lambda i: (i, 0)),
    ]


def _combine_pw1_kernel(x_ref, ya_ref, yb_ref, rinfo_ref, g_ref, w_ref, b_ref, x2_ref, u_ref):
    x2 = _combine(x_ref, ya_ref, yb_ref, rinfo_ref)
    x2_ref[...] = x2
    d = x2.shape[1]
    z = _dot(_rms(x2, g_ref[...]).astype(BF16), w_ref[...]) + b_ref[...]
    u_ref[...] = z[:, :d] * jax.nn.sigmoid(z[:, d:])


def _combine_pw1(x1, y2, rinfo, g, w, b):
    m, d = x1.shape
    tm = ROW_TILE
    return pl.pallas_call(
        _combine_pw1_kernel,
        grid=(m // tm,),
        in_specs=_combine_specs(tm, d, m) + [
            pl.BlockSpec((1, d), lambda i: (0, 0)),
            pl.BlockSpec((d, 2 * d), lambda i: (0, 0)),
            pl.BlockSpec((1, 2 * d), lambda i: (0, 0)),
        ],
        out_specs=[pl.BlockSpec((tm, d), lambda i: (i, 0)), pl.BlockSpec((tm, d), lambda i: (i, 0))],
        out_shape=[jax.ShapeDtypeStruct((m, d), F32), jax.ShapeDtypeStruct((m, d), F32)],
        compiler_params=_params("arbitrary"),
        name="combine_pw1",
    )(x1, y2, y2, rinfo, g, w, b)


def _combine_final_kernel(x_ref, ya_ref, yb_ref, rinfo_ref, g_ref, y_ref):
    y_ref[...] = _rms(_combine(x_ref, ya_ref, yb_ref, rinfo_ref), g_ref[...])


def _combine_final(x3, y2, rinfo, g):
    m, d = x3.shape
    tm = ROW_TILE
    return pl.pallas_call(
        _combine_final_kernel,
        grid=(m // tm,),
        in_specs=_combine_specs(tm, d, m) + [pl.BlockSpec((1, d), lambda i: (0, 0))],
        out_specs=pl.BlockSpec((tm, d), lambda i: (i, 0)),
        out_shape=jax.ShapeDtypeStruct((m, d), F32),
        compiler_params=_params("arbitrary"),
        name="combine_final",
    )(x3, y2, y2, rinfo, g)


def _dwconv_prompt_kernel(tiles_per_seq, u_ref, halo_ref, w_ref, b_ref, c_ref, scr):
    i = pl.program_id(0)
    tm = u_ref.shape[0]
    first = (i % tiles_per_seq) == 0
    scr[0:CONV_HALO, :] = jnp.where(first, 0.0, halo_ref[...])
    scr[CONV_HALO:, :] = u_ref[...]
    acc = jnp.broadcast_to(b_ref[...], c_ref.shape)
    for k in range(C_KERNEL):
        acc = acc + w_ref[k:k + 1, :] * scr[pl.ds(CONV_HALO - (C_KERNEL - 1) + k, tm), :]
    c_ref[...] = acc


def _dwconv_prompt(u, w, b, mp, seq):
    tm = ROW_TILE
    d = u.shape[1]
    return pl.pallas_call(
        functools.partial(_dwconv_prompt_kernel, seq // tm),
        grid=(mp // tm,),
        in_specs=[
            pl.BlockSpec((tm, d), lambda i: (i, 0)),
            pl.BlockSpec((CONV_HALO, d), lambda i: (jnp.maximum(i * (tm // CONV_HALO) - 1, 0), 0)),
            pl.BlockSpec((C_KERNEL, d), lambda i: (0, 0)),
            pl.BlockSpec((1, d), lambda i: (0, 0)),
        ],
        out_specs=pl.BlockSpec((tm, d), lambda i: (i, 0)),
        out_shape=jax.ShapeDtypeStruct((mp, d), F32),
        scratch_shapes=[pltpu.VMEM((tm + CONV_HALO, d), F32)],
        compiler_params=_params("arbitrary"),
        name="dwconv_prompt",
    )(u, u, w, b)


def _dwconv_sample_kernel(st_ref, u_ref, w_ref, b_ref, c_ref, ns_ref):
    n_state, t_len = st_ref.shape[1], u_ref.shape[1]
    rows = [st_ref[:, r, :] for r in range(n_state)] + [u_ref[:, t, :] for t in range(t_len)]
    for t in range(t_len):
        acc = jnp.broadcast_to(b_ref[...], rows[0].shape)
        for k in range(C_KERNEL):
            acc = acc + w_ref[k:k + 1, :] * rows[t + k]
        c_ref[:, t, :] = acc
    for r in range(n_state):
        ns_ref[:, r, :] = rows[r + t_len]


def _dwconv_sample(st, u3, w, b):
    n, n_state, d = st.shape
    t_len = u3.shape[1]
    sb = SEQ_BLOCK
    assert n_state == C_KERNEL - 1 and n % sb == 0
    return pl.pallas_call(
        _dwconv_sample_kernel,
        grid=(n // sb,),
        in_specs=[
            pl.BlockSpec((sb, n_state, d), lambda i: (i, 0, 0)),
            pl.BlockSpec((sb, t_len, d), lambda i: (i, 0, 0)),
            pl.BlockSpec((C_KERNEL, d), lambda i: (0, 0)),
            pl.BlockSpec((1, d), lambda i: (0, 0)),
        ],
        out_specs=[
            pl.BlockSpec((sb, t_len, d), lambda i: (i, 0, 0)),
            pl.BlockSpec((sb, n_state, d), lambda i: (i, 0, 0)),
        ],
        out_shape=[jax.ShapeDtypeStruct((n, t_len, d), F32), jax.ShapeDtypeStruct((n, n_state, d), F32)],
        compiler_params=_params("arbitrary"),
        name="dwconv_sample",
    )(st, u3, w, b)


def _conf_tail_kernel(n_prompt_tiles, x_ref, cp_ref, cs_ref, lg_ref, lb_ref, w_ref, b_ref, g_ref, wr_ref, br_ref,
                      x3_ref, h2rows_ref, rinfo_ref):
    i = pl.program_id(0)
    c = jnp.where(i < n_prompt_tiles, cp_ref[...], cs_ref[...])
    mu = jnp.mean(c, axis=-1, keepdims=True)
    cc = c - mu
    var = jnp.mean(cc * cc, axis=-1, keepdims=True)
    y = cc * lax.rsqrt(var + LN_EPS) * lg_ref[...] + lb_ref[...]
    y = y * jax.nn.sigmoid(y)
    x3 = x_ref[...] + _dot(y.astype(BF16), w_ref[...]) + b_ref[...]
    x3_ref[...] = x3
    _route(_rms(x3, g_ref[...]), wr_ref, br_ref, h2rows_ref, rinfo_ref)


def _conf_tail(x2, c_p, c_s, ln_g, ln_b, w, b, g, wr, br):
    m, d = x2.shape
    tm = ROW_TILE
    npt = c_p.shape[0] // tm
    r_in, r_out = _router_specs(tm, d)
    vec = pl.BlockSpec((1, d), lambda i: (0, 0))
    return pl.pallas_call(
        functools.partial(_conf_tail_kernel, npt),
        grid=(m // tm,),
        in_specs=[
            pl.BlockSpec((tm, d), lambda i: (i, 0)),
            pl.BlockSpec((tm, d), lambda i: (jnp.minimum(i, npt - 1), 0)),
            pl.BlockSpec((tm, d), lambda i: (jnp.maximum(i - npt, 0), 0)),
            vec, vec,
            pl.BlockSpec((d, d), lambda i: (0, 0)),
            vec,
        ] + r_in,
        out_specs=[pl.BlockSpec((tm, d), lambda i: (i, 0))] + r_out,
        out_shape=[jax.ShapeDtypeStruct((m, d), F32)] + _router_shapes(m),
        compiler_params=_params("arbitrary"),
        name="conf_tail_router",
    )(x2, c_p, c_s, ln_g, ln_b, w, b, g, wr, br)


def _router_weights(w_rg, b_rg, w_re, b_re):
    d = w_rg.shape[0]
    pad = LANES - N_EXPERT_GROUPS - N_EXPERTS
    wr = jnp.concatenate([w_rg, w_re, jnp.zeros((d, pad), F32)], axis=1)
    br = jnp.concatenate([b_rg, b_re, jnp.zeros((pad,), F32)])[None, :]
    return wr, br


def _moe(h2rows, rinfo, wg, wu, wd):
    m = rinfo.shape[0]
    tile_expert, tile_cnt, asg, n_tiles = _plan_tiles(rinfo)
    return _experts(h2rows, tile_expert, tile_cnt, asg, n_tiles, wg, wu, wd, m)


def kernel(x_prompt, x_sample, state_shortconv, cache_kv_w128, cache_kv_w512, cache_kv_w2048, state_conformer,
           g_mix, g_ffn, g_final, w_in, conv_a_w, w_out, w_pw1, b_pw1, dw_w, dw_b, ln_g, ln_b, w_pw2, b_pw2,
           w_router_group, b_router_group, w_router_expert, b_router_expert, w_gate, w_up, w_down):
    batch, seq, d = x_prompt.shape
    n_dec, t_dec, _ = x_sample.shape
    mp, ms = batch * seq, n_dec * t_dec
    assert g_mix.shape[0] == 2 and mp % ROW_TILE == 0 and ms % ROW_TILE == 0 and seq % ROW_TILE == 0
    xp = x_prompt.reshape(mp, d)
    xs = x_sample.reshape(ms, d)
    slab = (HEADS, HEAD_DIM)

    aw, qw = 3 * GROUP_W, N_GROUPS * GROUP_W
    w0 = w_in[0]
    kcols = w0[:, aw + qw:aw + 2 * qw].reshape(d, N_GROUPS, GROUP_W)
    vcols = w0[:, aw + 2 * qw:].reshape(d, N_GROUPS, GROUP_W)
    w_perm = jnp.concatenate([w0[:, :aw + qw], jnp.stack([kcols, vcols], axis=2).reshape(d, 2 * qw)], axis=1)
    q, kv, gbu = _inproj(xp, xs, g_mix[0][None, :], w_perm.astype(BF16))

    yb_p = _attn_prompt(q, kv, batch, seq)
    ya_p = _shortconv_prompt(gbu, conv_a_w[0], mp, seq)

    q_s = jnp.transpose(q[:, mp:], (1, 0, 2)).reshape(n_dec, t_dec, N_GROUPS, *slab)
    kv_s = jnp.transpose(kv[:, mp:], (1, 0, 2)).reshape(n_dec, t_dec, N_GROUPS, 2, *slab)
    gbu_s = gbu[mp:].reshape(n_dec, t_dec, 2 * GROUP_W)
    c0 = cache_kv_w128[0]
    c1 = cache_kv_w512[0].reshape(n_dec, NK, DIL_GROUPS[1][1], 2, *slab)
    c2 = cache_kv_w2048[0].reshape(n_dec, NK, DIL_GROUPS[2][1], 2, *slab)
    assert c0.shape[1] == NK and c1.shape[2] == t_dec
    yb_s, ya_s, s_sc = _mix_sample(q_s, kv_s, gbu_s, state_shortconv[0], c0, c1, c2, conv_a_w[0])
    ymix_s = jnp.concatenate([ya_s.reshape(ms, GROUP_W), yb_s.reshape(ms, GROUP_W)], axis=-1)

    wr0, br0 = _router_weights(w_router_group[0], b_router_group[0], w_router_expert[0], b_router_expert[0])
    x1, h2rows, rinfo0 = _outproj(xp, xs, ya_p, yb_p, ymix_s, w_out[0].astype(BF16), g_ffn[0][None, :], wr0, br0)
    y2 = _moe(h2rows, rinfo0, w_gate[0], w_up[0], w_down[0])

    x2, u = _combine_pw1(x1, y2, rinfo0, g_mix[1][None, :], w_pw1[0].astype(BF16), b_pw1[0][None, :])
    c_p = _dwconv_prompt(u, dw_w[0], dw_b[0][None, :], mp, seq)
    c_s3, s_cf = _dwconv_sample(state_conformer[0], u[mp:].reshape(n_dec, t_dec, d), dw_w[0], dw_b[0][None, :])
    wr1, br1 = _router_weights(w_router_group[1], b_router_group[1], w_router_expert[1], b_router_expert[1])
    x3, h2rows1, rinfo1 = _conf_tail(x2, c_p, c_s3.reshape(ms, d), ln_g[0][None, :], ln_b[0][None, :],
                                     w_pw2[0].astype(BF16), b_pw2[0][None, :], g_ffn[1][None, :], wr1, br1)
    y2b = _moe(h2rows1, rinfo1, w_gate[1], w_up[1], w_down[1])
    y = _combine_final(x3, y2b, rinfo1, g_final[None, :])

    cpg = 2 * GROUP_W // LANES
    kv_p = kv[:, :mp].reshape(N_GROUPS, cpg, batch, seq, LANES)
    p_kv = [jnp.transpose(kv_p[gi, :, :, seq - min(w, seq):], (1, 2, 0, 3)).reshape(1, batch, min(w, seq), 2, *slab)
            for gi, (w, _) in enumerate(DIL_GROUPS)]
    s_kv = [kv_s[:, :, gi][None] for gi in range(N_GROUPS)]
    u_p = gbu[:mp, GROUP_W:].reshape(batch, seq, GROUP_W)
    p_sc = u_p[:, seq - (A_KERNEL - 1):][None]
    p_cf = u[:mp].reshape(batch, seq, d)[:, seq - (C_KERNEL - 1):][None]
    return (y[:mp].reshape(batch, seq, d), y[mp:].reshape(n_dec, t_dec, d), p_sc, p_kv[0], p_kv[1], p_kv[2], p_cf,
            s_sc[None], s_kv[0], s_kv[1], s_kv[2], s_cf[None])
```

```python
import functools

import jax
import jax.numpy as jnp
from jax import lax
from jax.experimental import pallas as pl
from jax.experimental.pallas import tpu as pltpu

F32 = jnp.float32
BF16 = jnp.bfloat16

DIL_GROUPS = ((128, 1), (512, 4), (2048, 16))
N_GROUPS = len(DIL_GROUPS)
HEADS = 4
HEAD_DIM = 64
GROUP_W = HEADS * HEAD_DIM
NK = DIL_GROUPS[0][0] // DIL_GROUPS[0][1]
A_KERNEL = 3
C_KERNEL = 31
N_EXPERT_GROUPS = 4
EXPERTS_PER_GROUP = 8
N_EXPERTS = N_EXPERT_GROUPS * EXPERTS_PER_GROUP
RMS_EPS = 1e-6
LN_EPS = 1e-5
NEG_INF = -1e30

SUBLANES = 8
LANES = 128
VMEM_LIMIT = 48 * 1024 * 1024

ROW_TILE = 256
EXPERT_TILE = 256
CONV_HALO = 32
SEQ_BLOCK = 8
ATTN_SEQ_BLOCK = 2


def _params(*sem):
    return pltpu.CompilerParams(dimension_semantics=sem, vmem_limit_bytes=VMEM_LIMIT)


def _rms(x, g):
    return x * lax.rsqrt(jnp.mean(x * x, axis=-1, keepdims=True) + RMS_EPS) * g


def _dot(a, b):
    return jnp.dot(a, b, preferred_element_type=F32)


def _rows_to_lanes(ref, n_rows, base=None):
    parts = []
    for s in range(SUBLANES):
        idx = pl.ds(s, n_rows, stride=SUBLANES)
        parts.append(ref[idx, :] if base is None else ref[base, idx, :])
    return jnp.concatenate(parts, axis=-1)


def _lanes_to_rows(ref, val, base=None):
    n_rows = val.shape[0]
    for s in range(SUBLANES):
        idx = pl.ds(s, n_rows, stride=SUBLANES)
        piece = val[:, s * LANES:(s + 1) * LANES]
        if base is None:
            ref[idx, :] = piece
        else:
            ref[base, idx, :] = piece


def _inproj_kernel(n_prompt_tiles, xp_ref, xs_ref, g_ref, w_ref, q_ref, kv_ref, gbu_ref):
    i = pl.program_id(0)
    x = jnp.where(i < n_prompt_tiles, xp_ref[...], xs_ref[...])
    h = _rms(x, g_ref[...]).astype(BF16)
    aw = 3 * GROUP_W
    a = _dot(h, w_ref[:, 0:aw])
    gbu_ref[:, 0:GROUP_W] = a[:, 0:GROUP_W]
    gbu_ref[:, GROUP_W:2 * GROUP_W] = a[:, GROUP_W:2 * GROUP_W] * a[:, 2 * GROUP_W:aw]
    qw = N_GROUPS * GROUP_W
    _store_chunks(q_ref, _dot(h, w_ref[:, aw:aw + qw]) * (HEAD_DIM ** -0.5))
    _store_chunks(kv_ref, _dot(h, w_ref[:, aw + qw:]))


def _store_chunks(ref, val):
    for c in range(val.shape[1] // LANES):
        ref[c] = val[:, c * LANES:(c + 1) * LANES]


def _load_chunks(ref, rows=None):
    n = ref.shape[0]
    return jnp.concatenate([ref[c] if rows is None else ref[c, rows, :] for c in range(n)], axis=-1)


def _inproj(xp, xs, g, w):
    mp, d = xp.shape
    ms = xs.shape[0]
    tm = ROW_TILE
    npt, nst = mp // tm, ms // tm
    m = mp + ms
    ncols = w.shape[1]
    qw = N_GROUPS * GROUP_W
    return pl.pallas_call(
        functools.partial(_inproj_kernel, npt),
        grid=(npt + nst,),
        in_specs=[
            pl.BlockSpec((tm, d), lambda i: (jnp.minimum(i, npt - 1), 0)),
            pl.BlockSpec((tm, d), lambda i: (jnp.maximum(i - npt, 0), 0)),
            pl.BlockSpec((1, d), lambda i: (0, 0)),
            pl.BlockSpec((d, ncols), lambda i: (0, 0)),
        ],
        out_specs=[
            pl.BlockSpec((qw // LANES, tm, LANES), lambda i: (0, i, 0)),
            pl.BlockSpec((2 * qw // LANES, tm, LANES), lambda i: (0, i, 0)),
            pl.BlockSpec((tm, 2 * GROUP_W), lambda i: (i, 0)),
        ],
        out_shape=[
            jax.ShapeDtypeStruct((qw // LANES, m, LANES), F32),
            jax.ShapeDtypeStruct((2 * qw // LANES, m, LANES), F32),
            jax.ShapeDtypeStruct((m, 2 * GROUP_W), F32),
        ],
        compiler_params=_params("arbitrary"),
        name="inproj",
    )(xp, xs, g, w)


def _attn_prompt_kernel(q_ref, kv_ref, o_ref, m_scr, l_scr, acc_scr):
    g = pl.program_id(1)
    seq = q_ref.shape[1]

    @pl.when(g == 0)
    def _():
        m_scr[...] = jnp.full(m_scr.shape, NEG_INF, F32)
        l_scr[...] = jnp.zeros(l_scr.shape, F32)
        acc_scr[...] = jnp.zeros(acc_scr.shape, F32)

    qi = lax.broadcasted_iota(jnp.int32, (NK, 2 * NK), 0)
    kc = lax.broadcasted_iota(jnp.int32, (NK, 2 * NK), 1)
    band = (kc >= qi) & (kc <= qi + NK)
    lane = lax.broadcasted_iota(jnp.int32, (NK, LANES), 1)

    def block(blk, d, last):
        span = NK * d
        base = pl.multiple_of(blk * span, span)
        pbase = pl.multiple_of(jnp.maximum(blk - 1, 0) * span, span)
        mask = band & ((blk > 0) | (kc >= NK))
        for r in range(d):
            rows = pl.ds(base + r, NK, stride=d) if d > 1 else pl.ds(base, NK)
            prows = pl.ds(pbase + r, NK, stride=d) if d > 1 else pl.ds(pbase, NK)
            qb = _load_chunks(q_ref, rows).astype(BF16)
            kvc = _load_chunks(kv_ref, rows)
            kvp = _load_chunks(kv_ref, prows)
            kcat = jnp.concatenate([kvp[:, :GROUP_W], kvc[:, :GROUP_W]], axis=0).astype(BF16)
            vcat = jnp.concatenate([kvp[:, GROUP_W:], kvc[:, GROUP_W:]], axis=0).astype(BF16)
            m_old = m_scr[rows, :]
            l_old = l_scr[rows, :]
            acc_old = _load_chunks(acc_scr, rows)
            m_new, l_new, outs = m_old, l_old, []
            for h in range(HEADS):
                cols = slice(h * HEAD_DIM, (h + 1) * HEAD_DIM)
                s = lax.dot_general(qb[:, cols], kcat[:, cols], (((1,), (1,)), ((), ())),
                                    preferred_element_type=F32)
                s = jnp.where(mask, s, NEG_INF)
                mo = m_old[:, h:h + 1]
                mn = jnp.maximum(mo, jnp.max(s, axis=-1, keepdims=True))
                p = jnp.exp(s - mn)
                alpha = jnp.exp(mo - mn)
                ln = alpha * l_old[:, h:h + 1] + jnp.sum(p, axis=-1, keepdims=True)
                acc = alpha * acc_old[:, cols] + _dot(p.astype(BF16), vcat[:, cols])
                if last:
                    outs.append(acc / ln)
                else:
                    outs.append(acc)
                    m_new = jnp.where(lane == h, mn, m_new)
                    l_new = jnp.where(lane == h, ln, l_new)
            res = jnp.concatenate(outs, axis=-1)
            dst = o_ref if last else acc_scr
            for c in range(GROUP_W // LANES):
                dst[c, rows, :] = res[:, c * LANES:(c + 1) * LANES]
            if not last:
                m_scr[rows, :] = m_new
                l_scr[rows, :] = l_new

    for gi, (window, d) in enumerate(DIL_GROUPS):
        @pl.when(g == gi)
        def _(d=d, last=(gi == N_GROUPS - 1)):
            def body(blk, carry):
                block(blk, d, last)
                return carry
            lax.fori_loop(0, seq // (NK * d), body, 0)


def _attn_prompt(q, kv, batch, seq):
    assert seq % (NK * max(d for _, d in DIL_GROUPS)) == 0
    return pl.pallas_call(
        _attn_prompt_kernel,
        grid=(batch, N_GROUPS),
        in_specs=[
            pl.BlockSpec((GROUP_W // LANES, seq, LANES), lambda b, g: (g, b, 0)),
            pl.BlockSpec((2 * GROUP_W // LANES, seq, LANES), lambda b, g: (g, b, 0)),
        ],
        out_specs=pl.BlockSpec((GROUP_W // LANES, seq, LANES), lambda b, g: (0, b, 0)),
        out_shape=jax.ShapeDtypeStruct((GROUP_W // LANES, batch * seq, LANES), F32),
        scratch_shapes=[
            pltpu.VMEM((seq, LANES), F32),
            pltpu.VMEM((seq, LANES), F32),
            pltpu.VMEM((GROUP_W // LANES, seq, LANES), F32),
        ],
        compiler_params=_params("arbitrary", "arbitrary"),
        name="attn_prompt",
    )(q, kv)


def _shortconv_prompt_kernel(tiles_per_seq, gbu_ref, halo_ref, w_ref, o_ref, scr):
    i = pl.program_id(0)
    tm = gbu_ref.shape[0]
    gb = gbu_ref[:, 0:GROUP_W]
    u = gbu_ref[:, GROUP_W:]
    first = (i % tiles_per_seq) == 0
    scr[0:SUBLANES, :] = jnp.where(first, 0.0, halo_ref[:, GROUP_W:])
    scr[SUBLANES:, :] = u
    w = w_ref[...]
    conv = (w[0:1, :] * scr[pl.ds(SUBLANES - 2, tm), :] + w[1:2, :] * scr[pl.ds(SUBLANES - 1, tm), :]
            + w[2:3, :] * u)
    o_ref[...] = gb * conv


def _shortconv_prompt(gbu, w, mp, seq):
    tm = ROW_TILE
    return pl.pallas_call(
        functools.partial(_shortconv_prompt_kernel, seq // tm),
        grid=(mp // tm,),
        in_specs=[
            pl.BlockSpec((tm, 2 * GROUP_W), lambda i: (i, 0)),
            pl.BlockSpec((SUBLANES, 2 * GROUP_W), lambda i: (jnp.maximum(i * (tm // SUBLANES) - 1, 0), 0)),
            pl.BlockSpec((A_KERNEL, GROUP_W), lambda i: (0, 0)),
        ],
        out_specs=pl.BlockSpec((tm, GROUP_W), lambda i: (i, 0)),
        out_shape=jax.ShapeDtypeStruct((mp, GROUP_W), F32),
        scratch_shapes=[pltpu.VMEM((tm + SUBLANES, GROUP_W), F32)],
        compiler_params=_params("arbitrary"),
        name="shortconv_prompt",
    )(gbu, gbu, w)


def _mix_sample_kernel(q_ref, kn_ref, vn_ref, gbu_ref, st_ref, c0_ref, c1_ref, c2_ref, w_ref,
                       yb_ref, ya_ref, ns_ref):
    sb, t_len = q_ref.shape[0], q_ref.shape[3]
    w = w_ref[...]
    caches = (c0_ref, c1_ref, c2_ref)
    nt_dims = (((1,), (1,)), ((), ()))
    qn = lax.broadcasted_iota(jnp.int32, (t_len, t_len), 0)
    tn = lax.broadcasted_iota(jnp.int32, (t_len, t_len), 1)

    for s in range(sb):
        for h in range(HEADS):
            sc_parts, v_parts = [], []
            for gi, (window, d) in enumerate(DIL_GROUPS):
                c_ref = caches[gi]
                n_pos = c_ref.shape[-1]
                qg = q_ref[s, gi, h].astype(BF16)
                qc = lax.broadcasted_iota(jnp.int32, (t_len, n_pos), 0)
                pc = lax.broadcasted_iota(jnp.int32, (t_len, n_pos), 1)
                sc = _dot(qg, c_ref[s, 0, h].astype(BF16))
                sc_parts.append(jnp.where((((pc - qc) & (d - 1)) == 0) & (pc >= qc), sc, NEG_INF))
                v_parts.append((c_ref[s, 1, h].astype(BF16), True))
                sn = lax.dot_general(qg, kn_ref[s, gi, h].astype(BF16), nt_dims, preferred_element_type=F32)
                sc_parts.append(jnp.where((tn <= qn) & (((qn - tn) & (d - 1)) == 0), sn, NEG_INF))
                v_parts.append((vn_ref[s, gi, h].astype(BF16), False))
            mx = functools.reduce(jnp.maximum, [jnp.max(p, axis=-1, keepdims=True) for p in sc_parts])
            den = jnp.zeros((t_len, 1), F32)
            acc = jnp.zeros((t_len, HEAD_DIM), F32)
            for sc, (v, transposed) in zip(sc_parts, v_parts):
                p = jnp.exp(sc - mx)
                den = den + jnp.sum(p, axis=-1, keepdims=True)
                pb = p.astype(BF16)
                acc = acc + (lax.dot_general(pb, v, nt_dims, preferred_element_type=F32) if transposed
                             else _dot(pb, v))
            yb_ref[s, h] = acc / den
        gbu = gbu_ref[s]
        gb = gbu[:, 0:GROUP_W]
        u = gbu[:, GROUP_W:]
        st = st_ref[s]
        ext = [st[0:1], st[1:2]] + [u[t:t + 1] for t in range(t_len)]
        for t in range(t_len):
            conv = w[0:1] * ext[t] + w[1:2] * ext[t + 1] + w[2:3] * ext[t + 2]
            ya_ref[s, t:t + 1, :] = gb[t:t + 1] * conv
        for r in range(A_KERNEL - 1):
            ns_ref[s, r:r + 1, :] = ext[t_len + r]


def _mix_sample(q5, kn, vn, gbu3, st, c0, c1, c2, w):
    n, t_len = q5.shape[0], q5.shape[3]
    sb = ATTN_SEQ_BLOCK
    assert t_len >= A_KERNEL - 1 and n % sb == 0
    for c, (window, d) in zip((c0, c1, c2), DIL_GROUPS):
        assert c.shape[-1] == NK * d and d & (d - 1) == 0
    qspec = pl.BlockSpec((sb, N_GROUPS, HEADS, t_len, HEAD_DIM), lambda i: (i, 0, 0, 0, 0))
    cspec = lambda c: pl.BlockSpec((sb,) + c.shape[1:], lambda i: (i, 0, 0, 0, 0))
    return pl.pallas_call(
        _mix_sample_kernel,
        grid=(n // sb,),
        in_specs=[
            qspec, qspec, qspec,
            pl.BlockSpec((sb, t_len, 2 * GROUP_W), lambda i: (i, 0, 0)),
            pl.BlockSpec((sb, A_KERNEL - 1, GROUP_W), lambda i: (i, 0, 0)),
            cspec(c0), cspec(c1), cspec(c2),
            pl.BlockSpec((A_KERNEL, GROUP_W), lambda i: (0, 0)),
        ],
        out_specs=[
            pl.BlockSpec((sb, HEADS, t_len, HEAD_DIM), lambda i: (i, 0, 0, 0)),
            pl.BlockSpec((sb, t_len, GROUP_W), lambda i: (i, 0, 0)),
            pl.BlockSpec((sb, A_KERNEL - 1, GROUP_W), lambda i: (i, 0, 0)),
        ],
        out_shape=[
            jax.ShapeDtypeStruct((n, HEADS, t_len, HEAD_DIM), F32),
            jax.ShapeDtypeStruct((n, t_len, GROUP_W), F32),
            jax.ShapeDtypeStruct((n, A_KERNEL - 1, GROUP_W), F32),
        ],
        compiler_params=_params("arbitrary"),
        name="mix_sample",
    )(q5, kn, vn, gbu3, st, c0, c1, c2, w)


def _route(h2, wr_ref, br_ref, h2rows_ref, rinfo_ref):
    _lanes_to_rows(h2rows_ref, h2)
    logits = jnp.dot(h2, wr_ref[...], preferred_element_type=F32,
                     precision=lax.Precision.HIGHEST) + br_ref[...]
    li = lax.broadcasted_iota(jnp.int32, logits.shape, 1)
    gmask = li < N_EXPERT_GROUPS
    gl = jnp.where(gmask, logits, NEG_INF)
    ge = jnp.exp(gl - jnp.max(gl, axis=-1, keepdims=True))
    gp = ge / jnp.sum(ge, axis=-1, keepdims=True)
    g_val = jnp.max(gp, axis=-1, keepdims=True)
    g_idx = jnp.min(jnp.where(gmask & (gp == g_val), li, LANES), axis=-1, keepdims=True)
    sel = (li >= N_EXPERT_GROUPS) & (((li - N_EXPERT_GROUPS) // EXPERTS_PER_GROUP) == g_idx) \
        & (li < N_EXPERT_GROUPS + N_EXPERTS)
    el = jnp.where(sel, logits, NEG_INF)
    ee = jnp.exp(el - jnp.max(el, axis=-1, keepdims=True))
    ep = jnp.where(sel, ee / jnp.sum(ee, axis=-1, keepdims=True), -1.0)
    v1 = jnp.max(ep, axis=-1, keepdims=True)
    i1 = jnp.min(jnp.where(ep == v1, li, LANES), axis=-1, keepdims=True)
    ep2 = jnp.where(li == i1, -1.0, ep)
    v2 = jnp.max(ep2, axis=-1, keepdims=True)
    i2 = jnp.min(jnp.where(ep2 == v2, li, LANES), axis=-1, keepdims=True)
    scale = g_val / (v1 + v2)
    id1 = (i1 - N_EXPERT_GROUPS).astype(F32)
    id2 = (i2 - N_EXPERT_GROUPS).astype(F32)
    rinfo_ref[...] = jnp.where(li == 0, v1 * scale, jnp.where(li == 1, v2 * scale,
                               jnp.where(li == 2, id1, jnp.where(li == 3, id2, 0.0))))


def _router_specs(tm, d):
    ins = [pl.BlockSpec((1, d), lambda i: (0, 0)),
           pl.BlockSpec((d, LANES), lambda i: (0, 0)),
           pl.BlockSpec((1, LANES), lambda i: (0, 0))]
    outs = [pl.BlockSpec((tm * SUBLANES, LANES), lambda i: (i, 0)),
            pl.BlockSpec((tm, LANES), lambda i: (i, 0))]
    return ins, outs


def _router_shapes(m):
    return [jax.ShapeDtypeStruct((m * SUBLANES, LANES), F32), jax.ShapeDtypeStruct((m, LANES), F32)]


def _outproj_kernel(n_prompt_tiles, xp_ref, xs_ref, ya_ref, yb_ref, ys_ref, wo_ref, g_ref, wr_ref, br_ref,
                    x1_ref, h2rows_ref, rinfo_ref):
    i = pl.program_id(0)
    is_p = i < n_prompt_tiles
    x = jnp.where(is_p, xp_ref[...], xs_ref[...])
    ymix = jnp.where(is_p, jnp.concatenate([ya_ref[...], _load_chunks(yb_ref)], axis=-1), ys_ref[...])
    x1 = x + _dot(ymix.astype(BF16), wo_ref[...])
    x1_ref[...] = x1
    _route(_rms(x1, g_ref[...]), wr_ref, br_ref, h2rows_ref, rinfo_ref)


def _outproj(xp, xs, ya_p, yb_p, ymix_s, wo, g, wr, br):
    mp, d = xp.shape
    ms = xs.shape[0]
    tm = ROW_TILE
    npt, nst = mp // tm, ms // tm
    m = mp + ms
    r_in, r_out = _router_specs(tm, d)
    pmap = lambda i: (jnp.minimum(i, npt - 1), 0)
    smap = lambda i: (jnp.maximum(i - npt, 0), 0)
    return pl.pallas_call(
        functools.partial(_outproj_kernel, npt),
        grid=(npt + nst,),
        in_specs=[
            pl.BlockSpec((tm, d), pmap),
            pl.BlockSpec((tm, d), smap),
            pl.BlockSpec((tm, GROUP_W), pmap),
            pl.BlockSpec((GROUP_W // LANES, tm, LANES), lambda i: (0, jnp.minimum(i, npt - 1), 0)),
            pl.BlockSpec((tm, 2 * GROUP_W), smap),
            pl.BlockSpec((2 * GROUP_W, d), lambda i: (0, 0)),
        ] + r_in,
        out_specs=[pl.BlockSpec((tm, d), lambda i: (i, 0))] + r_out,
        out_shape=[jax.ShapeDtypeStruct((m, d), F32)] + _router_shapes(m),
        compiler_params=_params("arbitrary"),
        name="outproj_router",
    )(xp, xs, ya_p, yb_p, ymix_s, wo, g, wr, br)


def _experts_kernel(te_ref, cnt_ref, asg_ref, nt_ref, h_hbm, wg_ref, wu_ref, wd_ref, y_hbm, xbuf, ybuf, gsem, ssem,
                    *, n_tokens):
    i = pl.program_id(0)
    nt = nt_ref[0]
    te = EXPERT_TILE
    slot = i % 2

    def row_ds(r):
        return pl.ds(pl.multiple_of(r * SUBLANES, SUBLANES), SUBLANES)

    def gather(tile, sl):
        def body(j, c):
            tok = asg_ref[tile * te + j] >> 1
            pltpu.make_async_copy(h_hbm.at[row_ds(tok), :], xbuf.at[sl, row_ds(j), :], gsem.at[sl]).start()
            return c
        lax.fori_loop(0, te, body, 0)

    def scatter(tile, sl):
        def body(j, c):
            a = asg_ref[tile * te + j]
            row = (a & 1) * n_tokens + (a >> 1)
            pltpu.make_async_copy(ybuf.at[sl, row_ds(j), :], y_hbm.at[row_ds(row), :], ssem.at[sl]).start()
            return c
        lax.fori_loop(0, cnt_ref[tile], body, 0)

    def wait_gather(sl):
        pltpu.make_async_copy(h_hbm.at[pl.ds(0, te * SUBLANES), :], xbuf.at[sl], gsem.at[sl]).wait()

    def wait_scatter(tile, sl):
        rows = pl.ds(0, pl.multiple_of(cnt_ref[tile] * SUBLANES, SUBLANES))
        pltpu.make_async_copy(ybuf.at[sl, rows, :], y_hbm.at[rows, :], ssem.at[sl]).wait()

    @pl.when(i < nt)
    def _():
        @pl.when(i == 0)
        def _():
            gather(0, 0)

        @pl.when(i + 1 < nt)
        def _():
            gather(i + 1, 1 - slot)

        wait_gather(slot)
        x = _rows_to_lanes(xbuf, te, base=slot).astype(BF16)
        gate = _dot(x, wg_ref[...].astype(BF16))
        up = _dot(x, wu_ref[...].astype(BF16))
        act = (gate * jax.nn.sigmoid(gate) * up).astype(BF16)
        y = _dot(act, wd_ref[...].astype(BF16))
        _lanes_to_rows(ybuf, y, base=slot)
        scatter(i, slot)

        @pl.when(i >= 1)
        def _():
            wait_scatter(i - 1, 1 - slot)

        @pl.when(i == nt - 1)
        def _():
            wait_scatter(i, slot)


def _experts(h2rows, tile_expert, tile_cnt, asg, n_tiles, wg, wu, wd, layer, n_tokens):
    te = EXPERT_TILE
    nt_max = tile_expert.shape[0]
    d, f = wg.shape[2], wg.shape[3]
    n_rows = 2 * n_tokens
    wmap = lambda i, te_ref, cnt_ref, asg_ref, nt_ref: (layer, te_ref[i], 0, 0)
    grid_spec = pltpu.PrefetchScalarGridSpec(
        num_scalar_prefetch=4,
        grid=(nt_max,),
        in_specs=[
            pl.BlockSpec(memory_space=pl.ANY),
            pl.BlockSpec((None, None, d, f), wmap),
            pl.BlockSpec((None, None, d, f), wmap),
            pl.BlockSpec((None, None, f, d), wmap),
        ],
        out_specs=pl.BlockSpec(memory_space=pl.ANY),
        scratch_shapes=[
            pltpu.VMEM((2, te * SUBLANES, LANES), F32),
            pltpu.VMEM((2, te * SUBLANES, LANES), F32),
            pltpu.SemaphoreType.DMA((2,)),
            pltpu.SemaphoreType.DMA((2,)),
        ],
    )
    return pl.pallas_call(
        functools.partial(_experts_kernel, n_tokens=n_tokens),
        grid_spec=grid_spec,
        out_shape=jax.ShapeDtypeStruct((n_rows * SUBLANES, LANES), F32),
        compiler_params=_params("arbitrary"),
        name="experts",
    )(tile_expert, tile_cnt, asg, n_tiles, h2rows, wg, wu, wd)


def _plan_tiles(rinfo):
    m = rinfo.shape[0]
    te = EXPERT_TILE
    n_asg = 2 * m
    nt_max = n_asg // te + N_EXPERTS
    ids = rinfo[:, 2:4].astype(jnp.int32).reshape(n_asg)
    key_sorted, order = lax.sort((ids, lax.iota(jnp.int32, n_asg)), num_keys=1, is_stable=True)
    counts = jnp.sum((ids[:, None] == jnp.arange(N_EXPERTS)[None, :]).astype(jnp.int32), axis=0)
    tiles = (counts + te - 1) // te
    tile_end = jnp.cumsum(tiles)
    tile_start = tile_end - tiles
    row_start = jnp.cumsum(counts) - counts
    n_tiles = tile_end[-1]
    tidx = jnp.arange(nt_max, dtype=jnp.int32)
    tile_expert = jnp.sum((tidx[:, None] >= tile_end[None, :]).astype(jnp.int32), axis=1)
    last_expert = jnp.max(jnp.where(counts > 0, jnp.arange(N_EXPERTS), 0))
    tile_expert = jnp.where(tidx < n_tiles, jnp.minimum(tile_expert, N_EXPERTS - 1), last_expert).astype(jnp.int32)
    within = (tidx - tile_start[tile_expert])[:, None] * te + jnp.arange(te, dtype=jnp.int32)[None, :]
    valid = (within < counts[tile_expert][:, None]) & (tidx < n_tiles)[:, None]
    pos = jnp.clip(row_start[tile_expert][:, None] + within, 0, n_asg - 1)
    asg = jnp.where(valid, order[pos], 0).astype(jnp.int32).reshape(nt_max * te)
    tile_cnt = jnp.sum(valid.astype(jnp.int32), axis=1)
    return tile_expert, tile_cnt, asg, n_tiles.astype(jnp.int32).reshape(1)


def _combine(x_ref, ya_ref, yb_ref, rinfo_ref):
    tm = x_ref.shape[0]
    r = rinfo_ref[...]
    return x_ref[...] + r[:, 0:1] * _rows_to_lanes(ya_ref, tm) + r[:, 1:2] * _rows_to_lanes(yb_ref, tm)


def _combine_specs(tm, d, m):
    blocks_per_pick = m // tm
    return [
        pl.BlockSpec((tm, d), lambda i: (i, 0)),
        pl.BlockSpec((tm * SUBLANES, LANES), lambda i: (i, 0)),
        pl.BlockSpec((tm * SUBLANES, LANES), lambda i: (blocks_per_pick + i, 0)),
        pl.BlockSpec((tm, LANES), lambda i: (i, 0)),
    ]


def _combine_pw1_kernel(x_ref, ya_ref, yb_ref, rinfo_ref, g_ref, w_ref, b_ref, x2_ref, u_ref):
    x2 = _combine(x_ref, ya_ref, yb_ref, rinfo_ref)
    x2_ref[...] = x2
    d = x2.shape[1]
    z = _dot(_rms(x2, g_ref[...]).astype(BF16), w_ref[...]) + b_ref[...]
    u_ref[...] = z[:, :d] * jax.nn.sigmoid(z[:, d:])


def _combine_pw1(x1, y2, rinfo, g, w, b):
    m, d = x1.shape
    tm = ROW_TILE
    return pl.pallas_call(
        _combine_pw1_kernel,
        grid=(m // tm,),
        in_specs=_combine_specs(tm, d, m) + [
            pl.BlockSpec((1, d), lambda i: (0, 0)),
            pl.BlockSpec((d, 2 * d), lambda i: (0, 0)),
            pl.BlockSpec((1, 2 * d), lambda i: (0, 0)),
        ],
        out_specs=[pl.BlockSpec((tm, d), lambda i: (i, 0)), pl.BlockSpec((tm, d), lambda i: (i, 0))],
        out_shape=[jax.ShapeDtypeStruct((m, d), F32), jax.ShapeDtypeStruct((m, d), F32)],
        compiler_params=_params("arbitrary"),
        name="combine_pw1",
    )(x1, y2, y2, rinfo, g, w, b)


def _combine_final_kernel(n_prompt_tiles, x_ref, ya_ref, yb_ref, rinfo_ref, g_ref, yp_ref, ys_ref):
    i = pl.program_id(0)
    y = _rms(_combine(x_ref, ya_ref, yb_ref, rinfo_ref), g_ref[...])

    @pl.when(i < n_prompt_tiles)
    def _():
        yp_ref[...] = y

    @pl.when(i >= n_prompt_tiles)
    def _():
        ys_ref[...] = y


def _combine_final(x3, y2, rinfo, g, mp):
    m, d = x3.shape
    tm = ROW_TILE
    npt = mp // tm
    return pl.pallas_call(
        functools.partial(_combine_final_kernel, npt),
        grid=(m // tm,),
        in_specs=_combine_specs(tm, d, m) + [pl.BlockSpec((1, d), lambda i: (0, 0))],
        out_specs=[pl.BlockSpec((tm, d), lambda i: (jnp.minimum(i, npt - 1), 0)),
                   pl.BlockSpec((tm, d), lambda i: (jnp.maximum(i - npt, 0), 0))],
        out_shape=[jax.ShapeDtypeStruct((mp, d), F32), jax.ShapeDtypeStruct((m - mp, d), F32)],
        compiler_params=_params("arbitrary"),
        name="combine_final",
    )(x3, y2, y2, rinfo, g)


def _dwconv_prompt_kernel(tiles_per_seq, u_ref, halo_ref, w_ref, b_ref, c_ref, scr):
    i = pl.program_id(0)
    tm = u_ref.shape[0]
    first = (i % tiles_per_seq) == 0
    scr[0:CONV_HALO, :] = jnp.where(first, 0.0, halo_ref[...])
    scr[CONV_HALO:, :] = u_ref[...]
    acc = jnp.broadcast_to(b_ref[...], c_ref.shape)
    for k in range(C_KERNEL):
        acc = acc + w_ref[k:k + 1, :] * scr[pl.ds(CONV_HALO - (C_KERNEL - 1) + k, tm), :]
    c_ref[...] = acc


def _dwconv_prompt(u, w, b, mp, seq):
    tm = ROW_TILE
    d = u.shape[1]
    return pl.pallas_call(
        functools.partial(_dwconv_prompt_kernel, seq // tm),
        grid=(mp // tm,),
        in_specs=[
            pl.BlockSpec((tm, d), lambda i: (i, 0)),
            pl.BlockSpec((CONV_HALO, d), lambda i: (jnp.maximum(i * (tm // CONV_HALO) - 1, 0), 0)),
            pl.BlockSpec((C_KERNEL, d), lambda i: (0, 0)),
            pl.BlockSpec((1, d), lambda i: (0, 0)),
        ],
        out_specs=pl.BlockSpec((tm, d), lambda i: (i, 0)),
        out_shape=jax.ShapeDtypeStruct((mp, d), F32),
        scratch_shapes=[pltpu.VMEM((tm + CONV_HALO, d), F32)],
        compiler_params=_params("arbitrary"),
        name="dwconv_prompt",
    )(u, u, w, b)


def _dwconv_sample_kernel(st_ref, u_ref, w_ref, b_ref, c_ref, ns_ref):
    n_state, t_len = st_ref.shape[0], u_ref.shape[0]
    rows = [st_ref[r] for r in range(n_state)] + [u_ref[t] for t in range(t_len)]
    for t in range(t_len):
        acc = jnp.broadcast_to(b_ref[...], rows[0].shape)
        for k in range(C_KERNEL):
            acc = acc + w_ref[k:k + 1, :] * rows[t + k]
        c_ref[t] = acc
    for r in range(n_state):
        ns_ref[r] = rows[r + t_len]


def _dwconv_sample(st, u3, w, b):
    n_state, n, d = st.shape
    t_len = u3.shape[0]
    sb = SEQ_BLOCK
    assert n_state == C_KERNEL - 1 and n % sb == 0
    return pl.pallas_call(
        _dwconv_sample_kernel,
        grid=(n // sb,),
        in_specs=[
            pl.BlockSpec((n_state, sb, d), lambda i: (0, i, 0)),
            pl.BlockSpec((t_len, sb, d), lambda i: (0, i, 0)),
            pl.BlockSpec((C_KERNEL, d), lambda i: (0, 0)),
            pl.BlockSpec((1, d), lambda i: (0, 0)),
        ],
        out_specs=[
            pl.BlockSpec((t_len, sb, d), lambda i: (0, i, 0)),
            pl.BlockSpec((n_state, sb, d), lambda i: (0, i, 0)),
        ],
        out_shape=[jax.ShapeDtypeStruct((t_len, n, d), F32), jax.ShapeDtypeStruct((n_state, n, d), F32)],
        compiler_params=_params("arbitrary"),
        name="dwconv_sample",
    )(st, u3, w, b)


def _conf_tail_kernel(n_prompt_tiles, x_ref, cp_ref, cs_ref, lg_ref, lb_ref, w_ref, b_ref, g_ref, wr_ref, br_ref,
                      x3_ref, h2rows_ref, rinfo_ref):
    i = pl.program_id(0)
    c = jnp.where(i < n_prompt_tiles, cp_ref[...], cs_ref[...])
    mu = jnp.mean(c, axis=-1, keepdims=True)
    cc = c - mu
    var = jnp.mean(cc * cc, axis=-1, keepdims=True)
    y = cc * lax.rsqrt(var + LN_EPS) * lg_ref[...] + lb_ref[...]
    y = y * jax.nn.sigmoid(y)
    x3 = x_ref[...] + _dot(y.astype(BF16), w_ref[...]) + b_ref[...]
    x3_ref[...] = x3
    _route(_rms(x3, g_ref[...]), wr_ref, br_ref, h2rows_ref, rinfo_ref)


def _conf_tail(x2, c_p, c_s, ln_g, ln_b, w, b, g, wr, br):
    m, d = x2.shape
    tm = ROW_TILE
    npt = c_p.shape[0] // tm
    r_in, r_out = _router_specs(tm, d)
    vec = pl.BlockSpec((1, d), lambda i: (0, 0))
    return pl.pallas_call(
        functools.partial(_conf_tail_kernel, npt),
        grid=(m // tm,),
        in_specs=[
            pl.BlockSpec((tm, d), lambda i: (i, 0)),
            pl.BlockSpec((tm, d), lambda i: (jnp.minimum(i, npt - 1), 0)),
            pl.BlockSpec((tm, d), lambda i: (jnp.maximum(i - npt, 0), 0)),
            vec, vec,
            pl.BlockSpec((d, d), lambda i: (0, 0)),
            vec,
        ] + r_in,
        out_specs=[pl.BlockSpec((tm, d), lambda i: (i, 0))] + r_out,
        out_shape=[jax.ShapeDtypeStruct((m, d), F32)] + _router_shapes(m),
        compiler_params=_params("arbitrary"),
        name="conf_tail_router",
    )(x2, c_p, c_s, ln_g, ln_b, w, b, g, wr, br)


def _router_weights(w_rg, b_rg, w_re, b_re):
    d = w_rg.shape[0]
    pad = LANES - N_EXPERT_GROUPS - N_EXPERTS
    wr = jnp.concatenate([w_rg, w_re, jnp.zeros((d, pad), F32)], axis=1)
    br = jnp.concatenate([b_rg, b_re, jnp.zeros((pad,), F32)])[None, :]
    return wr, br


def _moe(h2rows, rinfo, wg, wu, wd, layer):
    m = rinfo.shape[0]
    tile_expert, tile_cnt, asg, n_tiles = _plan_tiles(rinfo)
    return _experts(h2rows, tile_expert, tile_cnt, asg, n_tiles, wg, wu, wd, layer, m)


def kernel(x_prompt, x_sample, state_shortconv, cache_kv_w128, cache_kv_w512, cache_kv_w2048, state_conformer,
           g_mix, g_ffn, g_final, w_in, conv_a_w, w_out, w_pw1, b_pw1, dw_w, dw_b, ln_g, ln_b, w_pw2, b_pw2,
           w_router_group, b_router_group, w_router_expert, b_router_expert, w_gate, w_up, w_down):
    batch, seq, d = x_prompt.shape
    n_dec, t_dec, _ = x_sample.shape
    mp, ms = batch * seq, n_dec * t_dec
    assert g_mix.shape[0] == 2 and mp % ROW_TILE == 0 and ms % ROW_TILE == 0 and seq % ROW_TILE == 0
    xp = x_prompt.reshape(mp, d)
    xs = x_sample.reshape(ms, d)
    slab = (HEADS, HEAD_DIM)

    aw, qw = 3 * GROUP_W, N_GROUPS * GROUP_W
    w0 = w_in[0]
    kcols = w0[:, aw + qw:aw + 2 * qw].reshape(d, N_GROUPS, GROUP_W)
    vcols = w0[:, aw + 2 * qw:].reshape(d, N_GROUPS, GROUP_W)
    w_perm = jnp.concatenate([w0[:, :aw + qw], jnp.stack([kcols, vcols], axis=2).reshape(d, 2 * qw)], axis=1)
    q, kv, gbu = _inproj(xp, xs, g_mix[0][None, :], w_perm.astype(BF16))

    yb_p = _attn_prompt(q, kv, batch, seq)
    ya_p = _shortconv_prompt(gbu, conv_a_w[0], mp, seq)

    hpc = LANES // HEAD_DIM
    q_s = jnp.transpose(q[:, mp:].reshape(N_GROUPS, HEADS // hpc, n_dec, t_dec, hpc, HEAD_DIM),
                        (2, 0, 1, 4, 3, 5)).reshape(n_dec, N_GROUPS, HEADS, t_dec, HEAD_DIM)
    kv_s = jnp.transpose(kv[:, mp:].reshape(N_GROUPS, 2, HEADS // hpc, n_dec, t_dec, hpc, HEAD_DIM),
                         (1, 3, 0, 2, 5, 4, 6)).reshape(2, n_dec, N_GROUPS, HEADS, t_dec, HEAD_DIM)
    gbu_s = gbu[mp:].reshape(n_dec, t_dec, 2 * GROUP_W)
    c0, c1, c2 = (jnp.transpose(c[0], (0, 2, 3, 4, 1)) for c in (cache_kv_w128, cache_kv_w512, cache_kv_w2048))
    yb_s, ya_s, s_sc = _mix_sample(q_s, kv_s[0], kv_s[1], gbu_s, state_shortconv[0], c0, c1, c2, conv_a_w[0])
    ymix_s = jnp.concatenate([ya_s.reshape(ms, GROUP_W),
                              jnp.transpose(yb_s, (0, 2, 1, 3)).reshape(ms, GROUP_W)], axis=-1)

    wr0, br0 = _router_weights(w_router_group[0], b_router_group[0], w_router_expert[0], b_router_expert[0])
    x1, h2rows, rinfo0 = _outproj(xp, xs, ya_p, yb_p, ymix_s, w_out[0].astype(BF16), g_ffn[0][None, :], wr0, br0)
    y2 = _moe(h2rows, rinfo0, w_gate, w_up, w_down, 0)

    x2, u = _combine_pw1(x1, y2, rinfo0, g_mix[1][None, :], w_pw1[0].astype(BF16), b_pw1[0][None, :])
    c_p = _dwconv_prompt(u, dw_w[0], dw_b[0][None, :], mp, seq)
    c_s3, s_cf = _dwconv_sample(jnp.transpose(state_conformer[0], (1, 0, 2)),
                                jnp.transpose(u[mp:].reshape(n_dec, t_dec, d), (1, 0, 2)), dw_w[0], dw_b[0][None, :])
    c_s = jnp.transpose(c_s3, (1, 0, 2)).reshape(ms, d)
    wr1, br1 = _router_weights(w_router_group[1], b_router_group[1], w_router_expert[1], b_router_expert[1])
    x3, h2rows1, rinfo1 = _conf_tail(x2, c_p, c_s, ln_g[0][None, :], ln_b[0][None, :],
                                     w_pw2[0].astype(BF16), b_pw2[0][None, :], g_ffn[1][None, :], wr1, br1)
    y2b = _moe(h2rows1, rinfo1, w_gate, w_up, w_down, 1)
    y_p, y_s = _combine_final(x3, y2b, rinfo1, g_final[None, :], mp)

    cpg = 2 * GROUP_W // LANES
    kv_p = kv[:, :mp].reshape(N_GROUPS, cpg, batch, seq, LANES)
    p_kv = [jnp.transpose(kv_p[gi, :, :, seq - min(w, seq):], (1, 2, 0, 3)).reshape(1, batch, min(w, seq), 2, *slab)
            for gi, (w, _) in enumerate(DIL_GROUPS)]
    s_kv = [jnp.transpose(kv_s[:, :, gi], (1, 3, 0, 2, 4))[None] for gi in range(N_GROUPS)]
    u_p = gbu[:mp, GROUP_W:].reshape(batch, seq, GROUP_W)
    p_sc = u_p[:, seq - (A_KERNEL - 1):][None]
    p_cf = u[:mp].reshape(batch, seq, d)[:, seq - (C_KERNEL - 1):][None]
    return (y_p.reshape(batch, seq, d), y_s.reshape(n_dec, t_dec, d), p_sc, p_kv[0], p_kv[1], p_kv[2], p_cf,
            s_sc[None], s_kv[0], s_kv[1], s_kv[2], jnp.transpose(s_cf, (1, 0, 2))[None])
```

```python
import functools

import jax
import jax.numpy as jnp
from jax import lax
from jax.experimental import pallas as pl
from jax.experimental.pallas import tpu as pltpu

F32 = jnp.float32
BF16 = jnp.bfloat16

DIL_GROUPS = ((128, 1), (512, 4), (2048, 16))
N_GROUPS = len(DIL_GROUPS)
HEADS = 4
HEAD_DIM = 64
GROUP_W = HEADS * HEAD_DIM
NK = DIL_GROUPS[0][0] // DIL_GROUPS[0][1]
A_KERNEL = 3
C_KERNEL = 31
N_EXPERT_GROUPS = 4
EXPERTS_PER_GROUP = 8
N_EXPERTS = N_EXPERT_GROUPS * EXPERTS_PER_GROUP
RMS_EPS = 1e-6
LN_EPS = 1e-5
NEG_INF = -1e30

SUBLANES = 8
LANES = 128
VMEM_LIMIT = 48 * 1024 * 1024

ROW_TILE = 256
EXPERT_TILE = 256
CONV_HALO = 32
SEQ_BLOCK = 8
ATTN_SEQ_BLOCK = 2


def _params(*sem):
    return pltpu.CompilerParams(dimension_semantics=sem, vmem_limit_bytes=VMEM_LIMIT)


def _rms(x, g):
    return x * lax.rsqrt(jnp.mean(x * x, axis=-1, keepdims=True) + RMS_EPS) * g


def _dot(a, b):
    return jnp.dot(a, b, preferred_element_type=F32)


def _rows_to_lanes(ref, n_rows, base=None):
    parts = []
    for s in range(SUBLANES):
        idx = pl.ds(s, n_rows, stride=SUBLANES)
        parts.append(ref[idx, :] if base is None else ref[base, idx, :])
    return jnp.concatenate(parts, axis=-1)


def _lanes_to_rows(ref, val, base=None):
    n_rows = val.shape[0]
    for s in range(SUBLANES):
        idx = pl.ds(s, n_rows, stride=SUBLANES)
        piece = val[:, s * LANES:(s + 1) * LANES]
        if base is None:
            ref[idx, :] = piece
        else:
            ref[base, idx, :] = piece


def _inproj_kernel(n_prompt_tiles, xp_ref, xs_ref, g_ref, w_ref, q_ref, kv_ref, gbu_ref):
    i = pl.program_id(0)
    x = jnp.where(i < n_prompt_tiles, xp_ref[...], xs_ref[...])
    h = _rms(x, g_ref[...]).astype(BF16)
    aw = 3 * GROUP_W
    a = _dot(h, w_ref[:, 0:aw])
    gbu_ref[:, 0:GROUP_W] = a[:, 0:GROUP_W]
    gbu_ref[:, GROUP_W:2 * GROUP_W] = a[:, GROUP_W:2 * GROUP_W] * a[:, 2 * GROUP_W:aw]
    qw = N_GROUPS * GROUP_W
    _store_chunks(q_ref, _dot(h, w_ref[:, aw:aw + qw]) * (HEAD_DIM ** -0.5))
    _store_chunks(kv_ref, _dot(h, w_ref[:, aw + qw:]))


def _store_chunks(ref, val):
    for c in range(val.shape[1] // LANES):
        ref[c] = val[:, c * LANES:(c + 1) * LANES]


def _load_chunks(ref, rows=None):
    n = ref.shape[0]
    return jnp.concatenate([ref[c] if rows is None else ref[c, rows, :] for c in range(n)], axis=-1)


def _inproj(xp, xs, g, w):
    mp, d = xp.shape
    ms = xs.shape[0]
    tm = ROW_TILE
    npt, nst = mp // tm, ms // tm
    m = mp + ms
    ncols = w.shape[1]
    qw = N_GROUPS * GROUP_W
    return pl.pallas_call(
        functools.partial(_inproj_kernel, npt),
        grid=(npt + nst,),
        in_specs=[
            pl.BlockSpec((tm, d), lambda i: (jnp.minimum(i, npt - 1), 0)),
            pl.BlockSpec((tm, d), lambda i: (jnp.maximum(i - npt, 0), 0)),
            pl.BlockSpec((1, d), lambda i: (0, 0)),
            pl.BlockSpec((d, ncols), lambda i: (0, 0)),
        ],
        out_specs=[
            pl.BlockSpec((qw // LANES, tm, LANES), lambda i: (0, i, 0)),
            pl.BlockSpec((2 * qw // LANES, tm, LANES), lambda i: (0, i, 0)),
            pl.BlockSpec((tm, 2 * GROUP_W), lambda i: (i, 0)),
        ],
        out_shape=[
            jax.ShapeDtypeStruct((qw // LANES, m, LANES), F32),
            jax.ShapeDtypeStruct((2 * qw // LANES, m, LANES), F32),
            jax.ShapeDtypeStruct((m, 2 * GROUP_W), F32),
        ],
        compiler_params=_params("arbitrary"),
        name="inproj",
    )(xp, xs, g, w)


def _attn_prompt_kernel(q_ref, kv_ref, o_ref, m_scr, l_scr, acc_scr):
    g = pl.program_id(1)
    seq = q_ref.shape[1]

    @pl.when(g == 0)
    def _():
        m_scr[...] = jnp.full(m_scr.shape, NEG_INF, F32)
        l_scr[...] = jnp.zeros(l_scr.shape, F32)
        acc_scr[...] = jnp.zeros(acc_scr.shape, F32)

    qi = lax.broadcasted_iota(jnp.int32, (NK, 2 * NK), 0)
    kc = lax.broadcasted_iota(jnp.int32, (NK, 2 * NK), 1)
    band = (kc >= qi) & (kc <= qi + NK)
    lane = lax.broadcasted_iota(jnp.int32, (NK, LANES), 1)

    def block(blk, d, last):
        span = NK * d
        base = pl.multiple_of(blk * span, span)
        pbase = pl.multiple_of(jnp.maximum(blk - 1, 0) * span, span)
        mask = band & ((blk > 0) | (kc >= NK))
        for r in range(d):
            rows = pl.ds(base + r, NK, stride=d) if d > 1 else pl.ds(base, NK)
            prows = pl.ds(pbase + r, NK, stride=d) if d > 1 else pl.ds(pbase, NK)
            qb = _load_chunks(q_ref, rows).astype(BF16)
            kvc = _load_chunks(kv_ref, rows)
            kvp = _load_chunks(kv_ref, prows)
            kcat = jnp.concatenate([kvp[:, :GROUP_W], kvc[:, :GROUP_W]], axis=0).astype(BF16)
            vcat = jnp.concatenate([kvp[:, GROUP_W:], kvc[:, GROUP_W:]], axis=0).astype(BF16)
            m_old = m_scr[rows, :]
            l_old = l_scr[rows, :]
            acc_old = _load_chunks(acc_scr, rows)
            m_new, l_new, outs = m_old, l_old, []
            for h in range(HEADS):
                cols = slice(h * HEAD_DIM, (h + 1) * HEAD_DIM)
                s = lax.dot_general(qb[:, cols], kcat[:, cols], (((1,), (1,)), ((), ())),
                                    preferred_element_type=F32)
                s = jnp.where(mask, s, NEG_INF)
                mo = m_old[:, h:h + 1]
                mn = jnp.maximum(mo, jnp.max(s, axis=-1, keepdims=True))
                p = jnp.exp(s - mn)
                alpha = jnp.exp(mo - mn)
                ln = alpha * l_old[:, h:h + 1] + jnp.sum(p, axis=-1, keepdims=True)
                acc = alpha * acc_old[:, cols] + _dot(p.astype(BF16), vcat[:, cols])
                if last:
                    outs.append(acc / ln)
                else:
                    outs.append(acc)
                    m_new = jnp.where(lane == h, mn, m_new)
                    l_new = jnp.where(lane == h, ln, l_new)
            res = jnp.concatenate(outs, axis=-1)
            dst = o_ref if last else acc_scr
            for c in range(GROUP_W // LANES):
                dst[c, rows, :] = res[:, c * LANES:(c + 1) * LANES]
            if not last:
                m_scr[rows, :] = m_new
                l_scr[rows, :] = l_new

    for gi, (window, d) in enumerate(DIL_GROUPS):
        @pl.when(g == gi)
        def _(d=d, last=(gi == N_GROUPS - 1)):
            def body(blk, carry):
                block(blk, d, last)
                return carry
            lax.fori_loop(0, seq // (NK * d), body, 0)


def _attn_prompt(q, kv, batch, seq):
    assert seq % (NK * max(d for _, d in DIL_GROUPS)) == 0
    return pl.pallas_call(
        _attn_prompt_kernel,
        grid=(batch, N_GROUPS),
        in_specs=[
            pl.BlockSpec((GROUP_W // LANES, seq, LANES), lambda b, g: (g, b, 0)),
            pl.BlockSpec((2 * GROUP_W // LANES, seq, LANES), lambda b, g: (g, b, 0)),
        ],
        out_specs=pl.BlockSpec((GROUP_W // LANES, seq, LANES), lambda b, g: (0, b, 0)),
        out_shape=jax.ShapeDtypeStruct((GROUP_W // LANES, batch * seq, LANES), F32),
        scratch_shapes=[
            pltpu.VMEM((seq, LANES), F32),
            pltpu.VMEM((seq, LANES), F32),
            pltpu.VMEM((GROUP_W // LANES, seq, LANES), F32),
        ],
        compiler_params=_params("arbitrary", "arbitrary"),
        name="attn_prompt",
    )(q, kv)


def _shortconv_prompt_kernel(tiles_per_seq, gbu_ref, halo_ref, w_ref, o_ref, scr):
    i = pl.program_id(0)
    tm = gbu_ref.shape[0]
    gb = gbu_ref[:, 0:GROUP_W]
    u = gbu_ref[:, GROUP_W:]
    first = (i % tiles_per_seq) == 0
    scr[0:SUBLANES, :] = jnp.where(first, 0.0, halo_ref[:, GROUP_W:])
    scr[SUBLANES:, :] = u
    w = w_ref[...]
    conv = (w[0:1, :] * scr[pl.ds(SUBLANES - 2, tm), :] + w[1:2, :] * scr[pl.ds(SUBLANES - 1, tm), :]
            + w[2:3, :] * u)
    o_ref[...] = gb * conv


def _shortconv_prompt(gbu, w, mp, seq):
    tm = ROW_TILE
    return pl.pallas_call(
        functools.partial(_shortconv_prompt_kernel, seq // tm),
        grid=(mp // tm,),
        in_specs=[
            pl.BlockSpec((tm, 2 * GROUP_W), lambda i: (i, 0)),
            pl.BlockSpec((SUBLANES, 2 * GROUP_W), lambda i: (jnp.maximum(i * (tm // SUBLANES) - 1, 0), 0)),
            pl.BlockSpec((A_KERNEL, GROUP_W), lambda i: (0, 0)),
        ],
        out_specs=pl.BlockSpec((tm, GROUP_W), lambda i: (i, 0)),
        out_shape=jax.ShapeDtypeStruct((mp, GROUP_W), F32),
        scratch_shapes=[pltpu.VMEM((tm + SUBLANES, GROUP_W), F32)],
        compiler_params=_params("arbitrary"),
        name="shortconv_prompt",
    )(gbu, gbu, w)


def _mix_sample_kernel(q_ref, kn_ref, vn_ref, gbu_ref, st_ref, c0_ref, c1_ref, c2_ref, w_ref,
                       yb_ref, ya_ref, ns_ref):
    sb, t_len = q_ref.shape[0], q_ref.shape[3]
    w = w_ref[...]
    caches = (c0_ref, c1_ref, c2_ref)
    nt_dims = (((1,), (1,)), ((), ()))
    qn = lax.broadcasted_iota(jnp.int32, (t_len, t_len), 0)
    tn = lax.broadcasted_iota(jnp.int32, (t_len, t_len), 1)

    for s in range(sb):
        for h in range(HEADS):
            sc_parts, v_parts = [], []
            for gi, (window, d) in enumerate(DIL_GROUPS):
                c_ref = caches[gi]
                n_pos = c_ref.shape[-1]
                qg = q_ref[s, gi, h].astype(BF16)
                qc = lax.broadcasted_iota(jnp.int32, (t_len, n_pos), 0)
                pc = lax.broadcasted_iota(jnp.int32, (t_len, n_pos), 1)
                sc = _dot(qg, c_ref[s, 0, h].astype(BF16))
                sc_parts.append(jnp.where((((pc - qc) & (d - 1)) == 0) & (pc >= qc), sc, NEG_INF))
                v_parts.append((c_ref[s, 1, h].astype(BF16), True))
                sn = lax.dot_general(qg, kn_ref[s, gi, h].astype(BF16), nt_dims, preferred_element_type=F32)
                sc_parts.append(jnp.where((tn <= qn) & (((qn - tn) & (d - 1)) == 0), sn, NEG_INF))
                v_parts.append((vn_ref[s, gi, h].astype(BF16), False))
            mx = functools.reduce(jnp.maximum, [jnp.max(p, axis=-1, keepdims=True) for p in sc_parts])
            den = jnp.zeros((t_len, 1), F32)
            acc = jnp.zeros((t_len, HEAD_DIM), F32)
            for sc, (v, transposed) in zip(sc_parts, v_parts):
                p = jnp.exp(sc - mx)
                den = den + jnp.sum(p, axis=-1, keepdims=True)
                pb = p.astype(BF16)
                acc = acc + (lax.dot_general(pb, v, nt_dims, preferred_element_type=F32) if transposed
                             else _dot(pb, v))
            yb_ref[s, h] = acc / den
        gbu = gbu_ref[s]
        gb = gbu[:, 0:GROUP_W]
        u = gbu[:, GROUP_W:]
        st = st_ref[s]
        ext = [st[0:1], st[1:2]] + [u[t:t + 1] for t in range(t_len)]
        for t in range(t_len):
            conv = w[0:1] * ext[t] + w[1:2] * ext[t + 1] + w[2:3] * ext[t + 2]
            ya_ref[s, t:t + 1, :] = gb[t:t + 1] * conv
        for r in range(A_KERNEL - 1):
            ns_ref[s, r:r + 1, :] = ext[t_len + r]


def _mix_sample(q5, kn, vn, gbu3, st, c0, c1, c2, w):
    n, t_len = q5.shape[0], q5.shape[3]
    sb = ATTN_SEQ_BLOCK
    assert t_len >= A_KERNEL - 1 and n % sb == 0
    for c, (window, d) in zip((c0, c1, c2), DIL_GROUPS):
        assert c.shape[-1] == NK * d and d & (d - 1) == 0
    qspec = pl.BlockSpec((sb, N_GROUPS, HEADS, t_len, HEAD_DIM), lambda i: (i, 0, 0, 0, 0))
    cspec = lambda c: pl.BlockSpec((sb,) + c.shape[1:], lambda i: (i, 0, 0, 0, 0))
    return pl.pallas_call(
        _mix_sample_kernel,
        grid=(n // sb,),
        in_specs=[
            qspec, qspec, qspec,
            pl.BlockSpec((sb, t_len, 2 * GROUP_W), lambda i: (i, 0, 0)),
            pl.BlockSpec((sb, A_KERNEL - 1, GROUP_W), lambda i: (i, 0, 0)),
            cspec(c0), cspec(c1), cspec(c2),
            pl.BlockSpec((A_KERNEL, GROUP_W), lambda i: (0, 0)),
        ],
        out_specs=[
            pl.BlockSpec((sb, HEADS, t_len, HEAD_DIM), lambda i: (i, 0, 0, 0)),
            pl.BlockSpec((sb, t_len, GROUP_W), lambda i: (i, 0, 0)),
            pl.BlockSpec((sb, A_KERNEL - 1, GROUP_W), lambda i: (i, 0, 0)),
        ],
        out_shape=[
            jax.ShapeDtypeStruct((n, HEADS, t_len, HEAD_DIM), F32),
            jax.ShapeDtypeStruct((n, t_len, GROUP_W), F32),
            jax.ShapeDtypeStruct((n, A_KERNEL - 1, GROUP_W), F32),
        ],
        compiler_params=_params("arbitrary"),
        name="mix_sample",
    )(q5, kn, vn, gbu3, st, c0, c1, c2, w)


def _route(h2, wr_ref, br_ref, h2rows_ref, rinfo_ref):
    _lanes_to_rows(h2rows_ref, h2)
    tm = h2.shape[0]
    h_hi = h2.astype(BF16)
    h_lo = (h2 - h_hi.astype(F32)).astype(BF16)
    nt_dims = (((1,), (1,)), ((), ()))
    w_hi, w_lo = wr_ref[0], wr_ref[1]
    lt = (lax.dot_general(w_hi, h_hi, nt_dims, preferred_element_type=F32)
          + lax.dot_general(w_hi, h_lo, nt_dims, preferred_element_type=F32)
          + lax.dot_general(w_lo, h_hi, nt_dims, preferred_element_type=F32)) + br_ref[...]
    rg = lax.broadcasted_iota(jnp.int32, (SUBLANES, tm), 0)
    gl = jnp.where(rg < N_EXPERT_GROUPS, lt[0:SUBLANES], NEG_INF)
    ge = jnp.exp(gl - jnp.max(gl, axis=0, keepdims=True))
    gp = ge / jnp.sum(ge, axis=0, keepdims=True)
    g_val = jnp.max(gp, axis=0, keepdims=True)
    g_idx = jnp.min(jnp.where(gp == g_val, rg, SUBLANES), axis=0, keepdims=True)
    re = lax.broadcasted_iota(jnp.int32, (N_EXPERTS, tm), 0)
    sel = (re // EXPERTS_PER_GROUP) == g_idx
    el = jnp.where(sel, lt[ROUTER_EXPERT_ROW0:ROUTER_EXPERT_ROW0 + N_EXPERTS], NEG_INF)
    ee = jnp.exp(el - jnp.max(el, axis=0, keepdims=True))
    ep = jnp.where(sel, ee / jnp.sum(ee, axis=0, keepdims=True), -1.0)
    v1 = jnp.max(ep, axis=0, keepdims=True)
    i1 = jnp.min(jnp.where(ep == v1, re, N_EXPERTS), axis=0, keepdims=True)
    ep2 = jnp.where(re == i1, -1.0, ep)
    v2 = jnp.max(ep2, axis=0, keepdims=True)
    i2 = jnp.min(jnp.where(ep2 == v2, re, N_EXPERTS), axis=0, keepdims=True)
    scale = g_val / (v1 + v2)
    rl = lax.broadcasted_iota(jnp.int32, (LANES, tm), 0)
    info_t = jnp.where(rl == 0, v1 * scale, jnp.where(rl == 1, v2 * scale,
                       jnp.where(rl == 2, i1.astype(F32), jnp.where(rl == 3, i2.astype(F32), 0.0))))
    rinfo_ref[...] = info_t.T


ROUTER_EXPERT_ROW0 = SUBLANES
ROUTER_ROWS = 48


def _router_specs(tm, d):
    ins = [pl.BlockSpec((1, d), lambda i: (0, 0)),
           pl.BlockSpec((2, ROUTER_ROWS, d), lambda i: (0, 0, 0)),
           pl.BlockSpec((ROUTER_ROWS, 1), lambda i: (0, 0))]
    outs = [pl.BlockSpec((tm * SUBLANES, LANES), lambda i: (i, 0)),
            pl.BlockSpec((tm, LANES), lambda i: (i, 0))]
    return ins, outs


def _router_shapes(m):
    return [jax.ShapeDtypeStruct((m * SUBLANES, LANES), F32), jax.ShapeDtypeStruct((m, LANES), F32)]


def _outproj_kernel(n_prompt_tiles, xp_ref, xs_ref, ya_ref, yb_ref, ys_ref, wo_ref, g_ref, wr_ref, br_ref,
                    x1_ref, h2rows_ref, rinfo_ref):
    i = pl.program_id(0)
    is_p = i < n_prompt_tiles
    x = jnp.where(is_p, xp_ref[...], xs_ref[...])
    ymix = jnp.where(is_p, jnp.concatenate([ya_ref[...], _load_chunks(yb_ref)], axis=-1), ys_ref[...])
    x1 = x + _dot(ymix.astype(BF16), wo_ref[...])
    x1_ref[...] = x1
    _route(_rms(x1, g_ref[...]), wr_ref, br_ref, h2rows_ref, rinfo_ref)


def _outproj(xp, xs, ya_p, yb_p, ymix_s, wo, g, wr, br):
    mp, d = xp.shape
    ms = xs.shape[0]
    tm = ROW_TILE
    npt, nst = mp // tm, ms // tm
    m = mp + ms
    r_in, r_out = _router_specs(tm, d)
    pmap = lambda i: (jnp.minimum(i, npt - 1), 0)
    smap = lambda i: (jnp.maximum(i - npt, 0), 0)
    return pl.pallas_call(
        functools.partial(_outproj_kernel, npt),
        grid=(npt + nst,),
        in_specs=[
            pl.BlockSpec((tm, d), pmap),
            pl.BlockSpec((tm, d), smap),
            pl.BlockSpec((tm, GROUP_W), pmap),
            pl.BlockSpec((GROUP_W // LANES, tm, LANES), lambda i: (0, jnp.minimum(i, npt - 1), 0)),
            pl.BlockSpec((tm, 2 * GROUP_W), smap),
            pl.BlockSpec((2 * GROUP_W, d), lambda i: (0, 0)),
        ] + r_in,
        out_specs=[pl.BlockSpec((tm, d), lambda i: (i, 0))] + r_out,
        out_shape=[jax.ShapeDtypeStruct((m, d), F32)] + _router_shapes(m),
        compiler_params=_params("arbitrary"),
        name="outproj_router",
    )(xp, xs, ya_p, yb_p, ymix_s, wo, g, wr, br)


def _experts_kernel(te_ref, src_ref, dst_ref, nt_ref, h_hbm, wg_ref, wu_ref, wd_ref, y_hbm, xbuf, ybuf, gsem, ssem,
                    *, trash_row):
    i = pl.program_id(0)
    nt = nt_ref[0]
    te = EXPERT_TILE
    slot = i % 2
    other = 1 - slot

    def tile_rows(r):
        return pl.ds(r if isinstance(r, int) else pl.multiple_of(r, SUBLANES), SUBLANES)

    def gather_copy(p, sl, j):
        return pltpu.make_async_copy(h_hbm.at[tile_rows(src_ref[p]), :],
                                     xbuf.at[sl, tile_rows(j * SUBLANES), :], gsem.at[sl])

    def scatter_copy(p, sl, j):
        return pltpu.make_async_copy(ybuf.at[sl, tile_rows(j * SUBLANES), :],
                                     y_hbm.at[tile_rows(dst_ref[p]), :], ssem.at[sl])

    def wait_gather(sl):
        pltpu.make_async_copy(h_hbm.at[pl.ds(0, te * SUBLANES), :], xbuf.at[sl], gsem.at[sl]).wait()

    def wait_scatter(sl):
        pltpu.make_async_copy(ybuf.at[sl], y_hbm.at[pl.ds(0, te * SUBLANES), :], ssem.at[sl]).wait()

    @pl.when(i == 0)
    def _():
        ybuf[...] = jnp.zeros(ybuf.shape, F32)
        init = pltpu.make_async_copy(ybuf.at[0], y_hbm.at[pl.ds(trash_row, te * SUBLANES), :], ssem.at[0])
        init.start()
        init.wait()

        def body(j, c):
            gather_copy(j, 0, j).start()
            return c
        lax.fori_loop(0, te, body, 0)

    @pl.when(i < nt)
    def _():
        wait_gather(slot)

        @pl.when(i >= 1)
        def _():
            wait_scatter(slot)

        base = i * te
        for j in range(te):
            gather_copy(base + te + j, other, j).start(priority=j % 2)
            scatter_copy(base + j, other, j).start(priority=(j + 1) % 2)
        x = _rows_to_lanes(xbuf, te, base=slot).astype(BF16)
        gate = _dot(x, wg_ref[...].astype(BF16))
        up = _dot(x, wu_ref[...].astype(BF16))
        act = (gate * jax.nn.sigmoid(gate) * up).astype(BF16)
        y = _dot(act, wd_ref[...].astype(BF16))
        _lanes_to_rows(ybuf, y, base=slot)

    @pl.when(i == nt)
    def _():
        wait_gather(slot)
        wait_scatter(slot)

        def body(j, c):
            scatter_copy(i * te + j, other, j).start()
            return c
        lax.fori_loop(0, te, body, 0)
        wait_scatter(other)


def _experts(h2rows, tile_expert, src, dst, n_tiles, wg, wu, wd, layer, n_tokens):
    te = EXPERT_TILE
    nt_max = tile_expert.shape[0]
    d, f = wg.shape[2], wg.shape[3]
    n_rows = 2 * n_tokens + te
    wmap = lambda i, te_ref, src_ref, dst_ref, nt_ref: (layer, te_ref[i], 0, 0)
    grid_spec = pltpu.PrefetchScalarGridSpec(
        num_scalar_prefetch=4,
        grid=(nt_max,),
        in_specs=[
            pl.BlockSpec(memory_space=pl.ANY),
            pl.BlockSpec((None, None, d, f), wmap),
            pl.BlockSpec((None, None, d, f), wmap),
            pl.BlockSpec((None, None, f, d), wmap),
        ],
        out_specs=pl.BlockSpec(memory_space=pl.ANY),
        scratch_shapes=[
            pltpu.VMEM((2, te * SUBLANES, LANES), F32),
            pltpu.VMEM((2, te * SUBLANES, LANES), F32),
            pltpu.SemaphoreType.DMA((2,)),
            pltpu.SemaphoreType.DMA((2,)),
        ],
    )
    return pl.pallas_call(
        functools.partial(_experts_kernel, trash_row=2 * n_tokens * SUBLANES),
        grid_spec=grid_spec,
        out_shape=jax.ShapeDtypeStruct((n_rows * SUBLANES, LANES), F32),
        compiler_params=_params("arbitrary"),
        name="experts",
    )(tile_expert, src, dst, n_tiles, h2rows, wg, wu, wd)


def _plan_tiles(rinfo):
    m = rinfo.shape[0]
    te = EXPERT_TILE
    n_asg = 2 * m
    nt_max = n_asg // te + N_EXPERTS + 1
    ids = rinfo[:, 2:4].astype(jnp.int32).reshape(n_asg)
    key_sorted, order = lax.sort((ids, lax.iota(jnp.int32, n_asg)), num_keys=1, is_stable=True)
    counts = jnp.sum((ids[:, None] == jnp.arange(N_EXPERTS)[None, :]).astype(jnp.int32), axis=0)
    tiles = (counts + te - 1) // te
    tile_end = jnp.cumsum(tiles)
    tile_start = tile_end - tiles
    row_start = jnp.cumsum(counts) - counts
    n_tiles = tile_end[-1]
    tidx = jnp.arange(nt_max, dtype=jnp.int32)
    tile_expert = jnp.sum((tidx[:, None] >= tile_end[None, :]).astype(jnp.int32), axis=1)
    last_expert = jnp.max(jnp.where(counts > 0, jnp.arange(N_EXPERTS), 0))
    tile_expert = jnp.where(tidx < n_tiles, jnp.minimum(tile_expert, N_EXPERTS - 1), last_expert).astype(jnp.int32)
    within = (tidx - tile_start[tile_expert])[:, None] * te + jnp.arange(te, dtype=jnp.int32)[None, :]
    valid = (within < counts[tile_expert][:, None]) & (tidx < n_tiles)[:, None]
    pos = jnp.clip(row_start[tile_expert][:, None] + within, 0, n_asg - 1)
    asg = order[pos]
    tok, pick = asg >> 1, asg & 1
    pad_dst = (2 * m + jnp.arange(te, dtype=jnp.int32)) * SUBLANES
    src = jnp.where(valid, tok * SUBLANES, 0).astype(jnp.int32).reshape(nt_max * te)
    dst = jnp.where(valid, (pick * m + tok) * SUBLANES, pad_dst[None, :]).astype(jnp.int32).reshape(nt_max * te)
    dst = jnp.concatenate([pad_dst, dst])
    return tile_expert, src, dst, n_tiles.astype(jnp.int32).reshape(1)


def _combine(x_ref, ya_ref, yb_ref, rinfo_ref):
    tm = x_ref.shape[0]
    r = rinfo_ref[...]
    return x_ref[...] + r[:, 0:1] * _rows_to_lanes(ya_ref, tm) + r[:, 1:2] * _rows_to_lanes(yb_ref, tm)


def _combine_specs(tm, d, m):
    blocks_per_pick = m // tm
    return [
        pl.BlockSpec((tm, d), lambda i: (i, 0)),
        pl.BlockSpec((tm * SUBLANES, LANES), lambda i: (i, 0)),
        pl.BlockSpec((tm * SUBLANES, LANES), lambda i: (blocks_per_pick + i, 0)),
        pl.BlockSpec((tm, LANES), lambda i: (i, 0)),
    ]


def _combine_pw1_kernel(x_ref, ya_ref, yb_ref, rinfo_ref, g_ref, w_ref, b_ref, x2_ref, u_ref):
    x2 = _combine(x_ref, ya_ref, yb_ref, rinfo_ref)
    x2_ref[...] = x2
    d = x2.shape[1]
    z = _dot(_rms(x2, g_ref[...]).astype(BF16), w_ref[...]) + b_ref[...]
    u_ref[...] = z[:, :d] * jax.nn.sigmoid(z[:, d:])


def _combine_pw1(x1, y2, rinfo, g, w, b):
    m, d = x1.shape
    tm = ROW_TILE
    return pl.pallas_call(
        _combine_pw1_kernel,
        grid=(m // tm,),
        in_specs=_combine_specs(tm, d, m) + [
            pl.BlockSpec((1, d), lambda i: (0, 0)),
            pl.BlockSpec((d, 2 * d), lambda i: (0, 0)),
            pl.BlockSpec((1, 2 * d), lambda i: (0, 0)),
        ],
        out_specs=[pl.BlockSpec((tm, d), lambda i: (i, 0)), pl.BlockSpec((tm, d), lambda i: (i, 0))],
        out_shape=[jax.ShapeDtypeStruct((m, d), F32), jax.ShapeDtypeStruct((m, d), F32)],
        compiler_params=_params("arbitrary"),
        name="combine_pw1",
    )(x1, y2, y2, rinfo, g, w, b)


def _combine_final_kernel(n_prompt_tiles, x_ref, ya_ref, yb_ref, rinfo_ref, g_ref, yp_ref, ys_ref):
    i = pl.program_id(0)
    y = _rms(_combine(x_ref, ya_ref, yb_ref, rinfo_ref), g_ref[...])

    @pl.when(i < n_prompt_tiles)
    def _():
        yp_ref[...] = y

    @pl.when(i >= n_prompt_tiles)
    def _():
        ys_ref[...] = y


def _combine_final(x3, y2, rinfo, g, mp):
    m, d = x3.shape
    tm = ROW_TILE
    npt = mp // tm
    return pl.pallas_call(
        functools.partial(_combine_final_kernel, npt),
        grid=(m // tm,),
        in_specs=_combine_specs(tm, d, m) + [pl.BlockSpec((1, d), lambda i: (0, 0))],
        out_specs=[pl.BlockSpec((tm, d), lambda i: (jnp.minimum(i, npt - 1), 0)),
                   pl.BlockSpec((tm, d), lambda i: (jnp.maximum(i - npt, 0), 0))],
        out_shape=[jax.ShapeDtypeStruct((mp, d), F32), jax.ShapeDtypeStruct((m - mp, d), F32)],
        compiler_params=_params("arbitrary"),
        name="combine_final",
    )(x3, y2, y2, rinfo, g)


def _dwconv_prompt_kernel(tiles_per_seq, u_ref, halo_ref, w_ref, b_ref, c_ref, scr):
    i = pl.program_id(0)
    tm = u_ref.shape[0]
    first = (i % tiles_per_seq) == 0
    scr[0:CONV_HALO, :] = jnp.where(first, 0.0, halo_ref[...])
    scr[CONV_HALO:, :] = u_ref[...]
    acc = jnp.broadcast_to(b_ref[...], c_ref.shape)
    for k in range(C_KERNEL):
        acc = acc + w_ref[k:k + 1, :] * scr[pl.ds(CONV_HALO - (C_KERNEL - 1) + k, tm), :]
    c_ref[...] = acc


def _dwconv_prompt(u, w, b, mp, seq):
    tm = ROW_TILE
    d = u.shape[1]
    return pl.pallas_call(
        functools.partial(_dwconv_prompt_kernel, seq // tm),
        grid=(mp // tm,),
        in_specs=[
            pl.BlockSpec((tm, d), lambda i: (i, 0)),
            pl.BlockSpec((CONV_HALO, d), lambda i: (jnp.maximum(i * (tm // CONV_HALO) - 1, 0), 0)),
            pl.BlockSpec((C_KERNEL, d), lambda i: (0, 0)),
            pl.BlockSpec((1, d), lambda i: (0, 0)),
        ],
        out_specs=pl.BlockSpec((tm, d), lambda i: (i, 0)),
        out_shape=jax.ShapeDtypeStruct((mp, d), F32),
        scratch_shapes=[pltpu.VMEM((tm + CONV_HALO, d), F32)],
        compiler_params=_params("arbitrary"),
        name="dwconv_prompt",
    )(u, u, w, b)


def _dwconv_sample_kernel(st_ref, u_ref, w_ref, b_ref, c_ref, ns_ref):
    n_state, t_len = st_ref.shape[0], u_ref.shape[0]
    rows = [st_ref[r] for r in range(n_state)] + [u_ref[t] for t in range(t_len)]
    for t in range(t_len):
        acc = jnp.broadcast_to(b_ref[...], rows[0].shape)
        for k in range(C_KERNEL):
            acc = acc + w_ref[k:k + 1, :] * rows[t + k]
        c_ref[t] = acc
    for r in range(n_state):
        ns_ref[r] = rows[r + t_len]


def _dwconv_sample(st, u3, w, b):
    n_state, n, d = st.shape
    t_len = u3.shape[0]
    sb = SEQ_BLOCK
    assert n_state == C_KERNEL - 1 and n % sb == 0
    return pl.pallas_call(
        _dwconv_sample_kernel,
        grid=(n // sb,),
        in_specs=[
            pl.BlockSpec((n_state, sb, d), lambda i: (0, i, 0)),
            pl.BlockSpec((t_len, sb, d), lambda i: (0, i, 0)),
            pl.BlockSpec((C_KERNEL, d), lambda i: (0, 0)),
            pl.BlockSpec((1, d), lambda i: (0, 0)),
        ],
        out_specs=[
            pl.BlockSpec((t_len, sb, d), lambda i: (0, i, 0)),
            pl.BlockSpec((n_state, sb, d), lambda i: (0, i, 0)),
        ],
        out_shape=[jax.ShapeDtypeStruct((t_len, n, d), F32), jax.ShapeDtypeStruct((n_state, n, d), F32)],
        compiler_params=_params("arbitrary"),
        name="dwconv_sample",
    )(st, u3, w, b)


def _conf_tail_kernel(n_prompt_tiles, x_ref, cp_ref, cs_ref, lg_ref, lb_ref, w_ref, b_ref, g_ref, wr_ref, br_ref,
                      x3_ref, h2rows_ref, rinfo_ref):
    i = pl.program_id(0)
    c = jnp.where(i < n_prompt_tiles, cp_ref[...], cs_ref[...])
    mu = jnp.mean(c, axis=-1, keepdims=True)
    cc = c - mu
    var = jnp.mean(cc * cc, axis=-1, keepdims=True)
    y = cc * lax.rsqrt(var + LN_EPS) * lg_ref[...] + lb_ref[...]
    y = y * jax.nn.sigmoid(y)
    x3 = x_ref[...] + _dot(y.astype(BF16), w_ref[...]) + b_ref[...]
    x3_ref[...] = x3
    _route(_rms(x3, g_ref[...]), wr_ref, br_ref, h2rows_ref, rinfo_ref)


def _conf_tail(x2, c_p, c_s, ln_g, ln_b, w, b, g, wr, br):
    m, d = x2.shape
    tm = ROW_TILE
    npt = c_p.shape[0] // tm
    r_in, r_out = _router_specs(tm, d)
    vec = pl.BlockSpec((1, d), lambda i: (0, 0))
    return pl.pallas_call(
        functools.partial(_conf_tail_kernel, npt),
        grid=(m // tm,),
        in_specs=[
            pl.BlockSpec((tm, d), lambda i: (i, 0)),
            pl.BlockSpec((tm, d), lambda i: (jnp.minimum(i, npt - 1), 0)),
            pl.BlockSpec((tm, d), lambda i: (jnp.maximum(i - npt, 0), 0)),
            vec, vec,
            pl.BlockSpec((d, d), lambda i: (0, 0)),
            vec,
        ] + r_in,
        out_specs=[pl.BlockSpec((tm, d), lambda i: (i, 0))] + r_out,
        out_shape=[jax.ShapeDtypeStruct((m, d), F32)] + _router_shapes(m),
        compiler_params=_params("arbitrary"),
        name="conf_tail_router",
    )(x2, c_p, c_s, ln_g, ln_b, w, b, g, wr, br)


def _router_weights(w_rg, b_rg, w_re, b_re):
    d = w_rg.shape[0]
    gpad = ROUTER_EXPERT_ROW0 - N_EXPERT_GROUPS
    epad = ROUTER_ROWS - ROUTER_EXPERT_ROW0 - N_EXPERTS
    wt = jnp.concatenate([w_rg.T, jnp.zeros((gpad, d), F32), w_re.T, jnp.zeros((epad, d), F32)], axis=0)
    w_hi = wt.astype(BF16)
    w_lo = (wt - w_hi.astype(F32)).astype(BF16)
    br = jnp.concatenate([b_rg, jnp.zeros((gpad,), F32), b_re, jnp.zeros((epad,), F32)])[:, None]
    return jnp.stack([w_hi, w_lo]), br


def _moe(h2rows, rinfo, wg, wu, wd, layer):
    m = rinfo.shape[0]
    tile_expert, src, dst, n_tiles = _plan_tiles(rinfo)
    return _experts(h2rows, tile_expert, src, dst, n_tiles, wg, wu, wd, layer, m)


def kernel(x_prompt, x_sample, state_shortconv, cache_kv_w128, cache_kv_w512, cache_kv_w2048, state_conformer,
           g_mix, g_ffn, g_final, w_in, conv_a_w, w_out, w_pw1, b_pw1, dw_w, dw_b, ln_g, ln_b, w_pw2, b_pw2,
           w_router_group, b_router_group, w_router_expert, b_router_expert, w_gate, w_up, w_down):
    batch, seq, d = x_prompt.shape
    n_dec, t_dec, _ = x_sample.shape
    mp, ms = batch * seq, n_dec * t_dec
    assert g_mix.shape[0] == 2 and mp % ROW_TILE == 0 and ms % ROW_TILE == 0 and seq % ROW_TILE == 0
    xp = x_prompt.reshape(mp, d)
    xs = x_sample.reshape(ms, d)
    slab = (HEADS, HEAD_DIM)

    aw, qw = 3 * GROUP_W, N_GROUPS * GROUP_W
    w0 = w_in[0]
    kcols = w0[:, aw + qw:aw + 2 * qw].reshape(d, N_GROUPS, GROUP_W)
    vcols = w0[:, aw + 2 * qw:].reshape(d, N_GROUPS, GROUP_W)
    w_perm = jnp.concatenate([w0[:, :aw + qw], jnp.stack([kcols, vcols], axis=2).reshape(d, 2 * qw)], axis=1)
    q, kv, gbu = _inproj(xp, xs, g_mix[0][None, :], w_perm.astype(BF16))

    yb_p = _attn_prompt(q, kv, batch, seq)
    ya_p = _shortconv_prompt(gbu, conv_a_w[0], mp, seq)

    hpc = LANES // HEAD_DIM
    q_s = jnp.transpose(q[:, mp:].reshape(N_GROUPS, HEADS // hpc, n_dec, t_dec, hpc, HEAD_DIM),
                        (2, 0, 1, 4, 3, 5)).reshape(n_dec, N_GROUPS, HEADS, t_dec, HEAD_DIM)
    kv_s = jnp.transpose(kv[:, mp:].reshape(N_GROUPS, 2, HEADS // hpc, n_dec, t_dec, hpc, HEAD_DIM),
                         (1, 3, 0, 2, 5, 4, 6)).reshape(2, n_dec, N_GROUPS, HEADS, t_dec, HEAD_DIM)
    gbu_s = gbu[mp:].reshape(n_dec, t_dec, 2 * GROUP_W)
    c0, c1, c2 = (jnp.transpose(c[0], (0, 2, 3, 4, 1)) for c in (cache_kv_w128, cache_kv_w512, cache_kv_w2048))
    yb_s, ya_s, s_sc = _mix_sample(q_s, kv_s[0], kv_s[1], gbu_s, state_shortconv[0], c0, c1, c2, conv_a_w[0])
    ymix_s = jnp.concatenate([ya_s.reshape(ms, GROUP_W),
                              jnp.transpose(yb_s, (0, 2, 1, 3)).reshape(ms, GROUP_W)], axis=-1)

    wr0, br0 = _router_weights(w_router_group[0], b_router_group[0], w_router_expert[0], b_router_expert[0])
    x1, h2rows, rinfo0 = _outproj(xp, xs, ya_p, yb_p, ymix_s, w_out[0].astype(BF16), g_ffn[0][None, :], wr0, br0)
    y2 = _moe(h2rows, rinfo0, w_gate, w_up, w_down, 0)

    x2, u = _combine_pw1(x1, y2, rinfo0, g_mix[1][None, :], w_pw1[0].astype(BF16), b_pw1[0][None, :])
    c_p = _dwconv_prompt(u, dw_w[0], dw_b[0][None, :], mp, seq)
    c_s3, s_cf = _dwconv_sample(jnp.transpose(state_conformer[0], (1, 0, 2)),
                                jnp.transpose(u[mp:].reshape(n_dec, t_dec, d), (1, 0, 2)), dw_w[0], dw_b[0][None, :])
    c_s = jnp.transpose(c_s3, (1, 0, 2)).reshape(ms, d)
    wr1, br1 = _router_weights(w_router_group[1], b_router_group[1], w_router_expert[1], b_router_expert[1])
    x3, h2rows1, rinfo1 = _conf_tail(x2, c_p, c_s, ln_g[0][None, :], ln_b[0][None, :],
                                     w_pw2[0].astype(BF16), b_pw2[0][None, :], g_ffn[1][None, :], wr1, br1)
    y2b = _moe(h2rows1, rinfo1, w_gate, w_up, w_down, 1)
    y_p, y_s = _combine_final(x3, y2b, rinfo1, g_final[None, :], mp)

    cpg = 2 * GROUP_W // LANES
    kv_p = kv[:, :mp].reshape(N_GROUPS, cpg, batch, seq, LANES)
    p_kv = [jnp.transpose(kv_p[gi, :, :, seq - min(w, seq):], (1, 2, 0, 3)).reshape(1, batch, min(w, seq), 2, *slab)
            for gi, (w, _) in enumerate(DIL_GROUPS)]
    s_kv = [jnp.transpose(kv_s[:, :, gi], (1, 3, 0, 2, 4))[None] for gi in range(N_GROUPS)]
    u_p = gbu[:mp, GROUP_W:].reshape(batch, seq, GROUP_W)
    p_sc = u_p[:, seq - (A_KERNEL - 1):][None]
    p_cf = u[:mp].reshape(batch, seq, d)[:, seq - (C_KERNEL - 1):][None]
    return (y_p.reshape(batch, seq, d), y_s.reshape(n_dec, t_dec, d), p_sc, p_kv[0], p_kv[1], p_kv[2], p_cf,
            s_sc[None], s_kv[0], s_kv[1], s_kv[2], jnp.transpose(s_cf, (1, 0, 2))[None])
```

```python
import functools

import jax
import jax.numpy as jnp
from jax import lax
from jax.experimental import pallas as pl
from jax.experimental.pallas import tpu as pltpu

F32 = jnp.float32
BF16 = jnp.bfloat16

DIL_GROUPS = ((128, 1), (512, 4), (2048, 16))
N_GROUPS = len(DIL_GROUPS)
HEADS = 4
HEAD_DIM = 64
GROUP_W = HEADS * HEAD_DIM
NK = DIL_GROUPS[0][0] // DIL_GROUPS[0][1]
A_KERNEL = 3
C_KERNEL = 31
N_EXPERT_GROUPS = 4
EXPERTS_PER_GROUP = 8
N_EXPERTS = N_EXPERT_GROUPS * EXPERTS_PER_GROUP
RMS_EPS = 1e-6
LN_EPS = 1e-5
NEG_INF = -1e30

SUBLANES = 8
LANES = 128
VMEM_LIMIT = 48 * 1024 * 1024
ATTN_VMEM_LIMIT = 60 * 1024 * 1024

ROW_TILE = 256
EXPERT_TILE = 256
CONV_HALO = 32
SEQ_BLOCK = 8
ATTN_SEQ_BLOCK = 2


def _params(*sem, vmem_limit=VMEM_LIMIT):
    return pltpu.CompilerParams(dimension_semantics=sem, vmem_limit_bytes=vmem_limit)


def _rms(x, g):
    return x * lax.rsqrt(jnp.mean(x * x, axis=-1, keepdims=True) + RMS_EPS) * g


def _dot(a, b):
    return jnp.dot(a, b, preferred_element_type=F32)


def _rows_to_lanes(ref, n_rows, base=None):
    parts = []
    for s in range(SUBLANES):
        idx = pl.ds(s, n_rows, stride=SUBLANES)
        parts.append(ref[idx, :] if base is None else ref[base, idx, :])
    return jnp.concatenate(parts, axis=-1)


def _lanes_to_rows(ref, val, base=None):
    n_rows = val.shape[0]
    for s in range(SUBLANES):
        idx = pl.ds(s, n_rows, stride=SUBLANES)
        piece = val[:, s * LANES:(s + 1) * LANES]
        if base is None:
            ref[idx, :] = piece
        else:
            ref[base, idx, :] = piece


def _inproj_kernel(n_prompt_tiles, xp_ref, xs_ref, g_ref, w_ref, q_ref, kv_ref, gbu_ref):
    i = pl.program_id(0)
    x = jnp.where(i < n_prompt_tiles, xp_ref[...], xs_ref[...])
    h = _rms(x, g_ref[...]).astype(BF16)
    aw = 3 * GROUP_W
    a = _dot(h, w_ref[:, 0:aw])
    gbu_ref[:, 0:GROUP_W] = a[:, 0:GROUP_W]
    gbu_ref[:, GROUP_W:2 * GROUP_W] = a[:, GROUP_W:2 * GROUP_W] * a[:, 2 * GROUP_W:aw]
    qw = N_GROUPS * GROUP_W
    _store_chunks(q_ref, _dot(h, w_ref[:, aw:aw + qw]) * (HEAD_DIM ** -0.5))
    _store_chunks(kv_ref, _dot(h, w_ref[:, aw + qw:]))


def _store_chunks(ref, val):
    for c in range(val.shape[1] // LANES):
        ref[c] = val[:, c * LANES:(c + 1) * LANES]


def _load_chunks(ref, rows=None):
    n = ref.shape[0]
    return jnp.concatenate([ref[c] if rows is None else ref[c, rows, :] for c in range(n)], axis=-1)


def _inproj(xp, xs, g, w):
    mp, d = xp.shape
    ms = xs.shape[0]
    tm = ROW_TILE
    npt, nst = mp // tm, ms // tm
    m = mp + ms
    ncols = w.shape[1]
    qw = N_GROUPS * GROUP_W
    return pl.pallas_call(
        functools.partial(_inproj_kernel, npt),
        grid=(npt + nst,),
        in_specs=[
            pl.BlockSpec((tm, d), lambda i: (jnp.minimum(i, npt - 1), 0)),
            pl.BlockSpec((tm, d), lambda i: (jnp.maximum(i - npt, 0), 0)),
            pl.BlockSpec((1, d), lambda i: (0, 0)),
            pl.BlockSpec((d, ncols), lambda i: (0, 0)),
        ],
        out_specs=[
            pl.BlockSpec((qw // LANES, tm, LANES), lambda i: (0, i, 0)),
            pl.BlockSpec((2 * qw // LANES, tm, LANES), lambda i: (0, i, 0)),
            pl.BlockSpec((tm, 2 * GROUP_W), lambda i: (i, 0)),
        ],
        out_shape=[
            jax.ShapeDtypeStruct((qw // LANES, m, LANES), F32),
            jax.ShapeDtypeStruct((2 * qw // LANES, m, LANES), F32),
            jax.ShapeDtypeStruct((m, 2 * GROUP_W), F32),
        ],
        compiler_params=_params("arbitrary"),
        name="inproj",
    )(xp, xs, g, w)


def _attn_prompt_kernel(q_ref, kv_ref, o_ref, m_scr, l_scr, acc_scr):
    g = pl.program_id(1)
    seq = q_ref.shape[1]

    @pl.when(g == 0)
    def _():
        m_scr[...] = jnp.full(m_scr.shape, NEG_INF, F32)
        l_scr[...] = jnp.zeros(l_scr.shape, F32)
        acc_scr[...] = jnp.zeros(acc_scr.shape, F32)

    qi = lax.broadcasted_iota(jnp.int32, (NK, 2 * NK), 0)
    kc = lax.broadcasted_iota(jnp.int32, (NK, 2 * NK), 1)
    band = (kc >= qi) & (kc <= qi + NK)
    lane = lax.broadcasted_iota(jnp.int32, (NK, LANES), 1)
    lane_kv = lax.broadcasted_iota(jnp.int32, (2 * NK, LANES), 1)
    ones_kv = jnp.ones((2 * NK, LANES), BF16)
    n_chunks = GROUP_W // LANES
    hpc = LANES // HEAD_DIM

    def block(blk, d, last):
        span = NK * d
        base = pl.multiple_of(blk * span, span)
        pbase = pl.multiple_of(jnp.maximum(blk - 1, 0) * span, span)
        mask = band & ((blk > 0) | (kc >= NK))
        for r in range(d):
            rows = pl.ds(base + r, NK, stride=d) if d > 1 else pl.ds(base, NK)
            prows = pl.ds(pbase + r, NK, stride=d) if d > 1 else pl.ds(pbase, NK)
            m_old = m_scr[rows, :]
            l_old = l_scr[rows, :]
            m_new, l_new = m_old, l_old
            for c in range(GROUP_W // LANES):
                qc = q_ref[c, rows, :].astype(BF16)
                kc_ = jnp.concatenate([kv_ref[c, prows, :], kv_ref[c, rows, :]], axis=0).astype(BF16)
                vc_ = jnp.concatenate([kv_ref[n_chunks + c, prows, :], kv_ref[n_chunks + c, rows, :]],
                                      axis=0).astype(BF16)
                acc = acc_scr[c, rows, :]
                scale, denom, contrib = None, None, None
                for hh in range(hpc):
                    h = c * hpc + hh
                    sel_q = (lane >= hh * HEAD_DIM) & (lane < (hh + 1) * HEAD_DIM)
                    sel_v = (lane_kv >= hh * HEAD_DIM) & (lane_kv < (hh + 1) * HEAD_DIM)
                    s = lax.dot_general(jnp.where(sel_q, qc, 0), kc_, (((1,), (1,)), ((), ())),
                                        preferred_element_type=F32)
                    s = jnp.where(mask, s, NEG_INF)
                    mo = m_old[:, h:h + 1]
                    mn = jnp.maximum(mo, jnp.max(s, axis=-1, keepdims=True))
                    p = jnp.exp(s - mn)
                    alpha = jnp.exp(mo - mn)
                    pv = _dot(p.astype(BF16), jnp.concatenate([jnp.where(sel_v, vc_, 0), ones_kv], axis=1))
                    ln = alpha * l_old[:, h:h + 1] + pv[:, LANES:]
                    contrib = pv[:, :LANES] if contrib is None else contrib + pv[:, :LANES]
                    scale = alpha if scale is None else jnp.where(sel_q, alpha, scale)
                    denom = ln if denom is None else jnp.where(sel_q, ln, denom)
                    m_new = jnp.where(lane == h, mn, m_new)
                    l_new = jnp.where(lane == h, ln, l_new)
                acc = scale * acc + contrib
                if last:
                    o_ref[c, rows, :] = acc / denom
                else:
                    acc_scr[c, rows, :] = acc
            if not last:
                m_scr[rows, :] = m_new
                l_scr[rows, :] = l_new

    for gi, (window, d) in enumerate(DIL_GROUPS):
        @pl.when(g == gi)
        def _(d=d, last=(gi == N_GROUPS - 1)):
            def body(blk, carry):
                block(blk, d, last)
                return carry
            lax.fori_loop(0, seq // (NK * d), body, 0)


def _attn_prompt(q, kv, batch, seq):
    assert seq % (NK * max(d for _, d in DIL_GROUPS)) == 0
    return pl.pallas_call(
        _attn_prompt_kernel,
        grid=(batch, N_GROUPS),
        in_specs=[
            pl.BlockSpec((GROUP_W // LANES, seq, LANES), lambda b, g: (g, b, 0)),
            pl.BlockSpec((2 * GROUP_W // LANES, seq, LANES), lambda b, g: (g, b, 0)),
        ],
        out_specs=pl.BlockSpec((GROUP_W // LANES, seq, LANES), lambda b, g: (0, b, 0)),
        out_shape=jax.ShapeDtypeStruct((GROUP_W // LANES, batch * seq, LANES), F32),
        scratch_shapes=[
            pltpu.VMEM((seq, LANES), F32),
            pltpu.VMEM((seq, LANES), F32),
            pltpu.VMEM((GROUP_W // LANES, seq, LANES), F32),
        ],
        compiler_params=_params("arbitrary", "arbitrary", vmem_limit=ATTN_VMEM_LIMIT),
        name="attn_prompt",
    )(q, kv)


def _shortconv_prompt_kernel(tiles_per_seq, gbu_ref, halo_ref, w_ref, o_ref, scr):
    i = pl.program_id(0)
    tm = gbu_ref.shape[0]
    gb = gbu_ref[:, 0:GROUP_W]
    u = gbu_ref[:, GROUP_W:]
    first = (i % tiles_per_seq) == 0
    scr[0:SUBLANES, :] = jnp.where(first, 0.0, halo_ref[:, GROUP_W:])
    scr[SUBLANES:, :] = u
    w = w_ref[...]
    conv = (w[0:1, :] * scr[pl.ds(SUBLANES - 2, tm), :] + w[1:2, :] * scr[pl.ds(SUBLANES - 1, tm), :]
            + w[2:3, :] * u)
    o_ref[...] = gb * conv


def _shortconv_prompt(gbu, w, mp, seq):
    tm = ROW_TILE
    return pl.pallas_call(
        functools.partial(_shortconv_prompt_kernel, seq // tm),
        grid=(mp // tm,),
        in_specs=[
            pl.BlockSpec((tm, 2 * GROUP_W), lambda i: (i, 0)),
            pl.BlockSpec((SUBLANES, 2 * GROUP_W), lambda i: (jnp.maximum(i * (tm // SUBLANES) - 1, 0), 0)),
            pl.BlockSpec((A_KERNEL, GROUP_W), lambda i: (0, 0)),
        ],
        out_specs=pl.BlockSpec((tm, GROUP_W), lambda i: (i, 0)),
        out_shape=jax.ShapeDtypeStruct((mp, GROUP_W), F32),
        scratch_shapes=[pltpu.VMEM((tm + SUBLANES, GROUP_W), F32)],
        compiler_params=_params("arbitrary"),
        name="shortconv_prompt",
    )(gbu, gbu, w)


def _mix_sample_kernel(q_ref, kn_ref, vn_ref, gbu_ref, st_ref, c0_ref, c1_ref, c2_ref, w_ref,
                       yb_ref, ya_ref, ns_ref):
    sb, t_len = q_ref.shape[0], q_ref.shape[3]
    w = w_ref[...]
    caches = (c0_ref, c1_ref, c2_ref)
    nt_dims = (((1,), (1,)), ((), ()))
    qn = lax.broadcasted_iota(jnp.int32, (t_len, t_len), 0)
    tn = lax.broadcasted_iota(jnp.int32, (t_len, t_len), 1)

    for s in range(sb):
        for h in range(HEADS):
            sc_parts, v_parts = [], []
            for gi, (window, d) in enumerate(DIL_GROUPS):
                c_ref = caches[gi]
                n_pos = c_ref.shape[-1]
                qg = q_ref[s, gi, h].astype(BF16)
                qc = lax.broadcasted_iota(jnp.int32, (t_len, n_pos), 0)
                pc = lax.broadcasted_iota(jnp.int32, (t_len, n_pos), 1)
                sc = _dot(qg, c_ref[s, 0, h].astype(BF16))
                sc_parts.append(jnp.where((((pc - qc) & (d - 1)) == 0) & (pc >= qc), sc, NEG_INF))
                v_parts.append((c_ref[s, 1, h].astype(BF16), True))
                sn = lax.dot_general(qg, kn_ref[s, gi, h].astype(BF16), nt_dims, preferred_element_type=F32)
                sc_parts.append(jnp.where((tn <= qn) & (((qn - tn) & (d - 1)) == 0), sn, NEG_INF))
                v_parts.append((vn_ref[s, gi, h].astype(BF16), False))
            mx = functools.reduce(jnp.maximum, [jnp.max(p, axis=-1, keepdims=True) for p in sc_parts])
            den = jnp.zeros((t_len, 1), F32)
            acc = jnp.zeros((t_len, HEAD_DIM), F32)
            for sc, (v, transposed) in zip(sc_parts, v_parts):
                p = jnp.exp(sc - mx)
                den = den + jnp.sum(p, axis=-1, keepdims=True)
                pb = p.astype(BF16)
                acc = acc + (lax.dot_general(pb, v, nt_dims, preferred_element_type=F32) if transposed
                             else _dot(pb, v))
            yb_ref[s, h] = acc / den
        gbu = gbu_ref[s]
        gb = gbu[:, 0:GROUP_W]
        u = gbu[:, GROUP_W:]
        st = st_ref[s]
        ext = [st[0:1], st[1:2]] + [u[t:t + 1] for t in range(t_len)]
        for t in range(t_len):
            conv = w[0:1] * ext[t] + w[1:2] * ext[t + 1] + w[2:3] * ext[t + 2]
            ya_ref[s, t:t + 1, :] = gb[t:t + 1] * conv
        for r in range(A_KERNEL - 1):
            ns_ref[s, r:r + 1, :] = ext[t_len + r]


def _mix_sample(q5, kn, vn, gbu3, st, c0, c1, c2, w):
    n, t_len = q5.shape[0], q5.shape[3]
    sb = ATTN_SEQ_BLOCK
    assert t_len >= A_KERNEL - 1 and n % sb == 0
    for c, (window, d) in zip((c0, c1, c2), DIL_GROUPS):
        assert c.shape[-1] == NK * d and d & (d - 1) == 0
    qspec = pl.BlockSpec((sb, N_GROUPS, HEADS, t_len, HEAD_DIM), lambda i: (i, 0, 0, 0, 0))
    cspec = lambda c: pl.BlockSpec((sb,) + c.shape[1:], lambda i: (i, 0, 0, 0, 0))
    return pl.pallas_call(
        _mix_sample_kernel,
        grid=(n // sb,),
        in_specs=[
            qspec, qspec, qspec,
            pl.BlockSpec((sb, t_len, 2 * GROUP_W), lambda i: (i, 0, 0)),
            pl.BlockSpec((sb, A_KERNEL - 1, GROUP_W), lambda i: (i, 0, 0)),
            cspec(c0), cspec(c1), cspec(c2),
            pl.BlockSpec((A_KERNEL, GROUP_W), lambda i: (0, 0)),
        ],
        out_specs=[
            pl.BlockSpec((sb, HEADS, t_len, HEAD_DIM), lambda i: (i, 0, 0, 0)),
            pl.BlockSpec((sb, t_len, GROUP_W), lambda i: (i, 0, 0)),
            pl.BlockSpec((sb, A_KERNEL - 1, GROUP_W), lambda i: (i, 0, 0)),
        ],
        out_shape=[
            jax.ShapeDtypeStruct((n, HEADS, t_len, HEAD_DIM), F32),
            jax.ShapeDtypeStruct((n, t_len, GROUP_W), F32),
            jax.ShapeDtypeStruct((n, A_KERNEL - 1, GROUP_W), F32),
        ],
        compiler_params=_params("arbitrary"),
        name="mix_sample",
    )(q5, kn, vn, gbu3, st, c0, c1, c2, w)


def _route(h2, wr_ref, br_ref, h2rows_ref, rinfo_ref):
    _lanes_to_rows(h2rows_ref, h2)
    tm = h2.shape[0]
    h_hi = h2.astype(BF16)
    h_lo = (h2 - h_hi.astype(F32)).astype(BF16)
    nt_dims = (((1,), (1,)), ((), ()))
    w_hi, w_lo = wr_ref[0], wr_ref[1]
    lt = (lax.dot_general(w_hi, h_hi, nt_dims, preferred_element_type=F32)
          + lax.dot_general(w_hi, h_lo, nt_dims, preferred_element_type=F32)
          + lax.dot_general(w_lo, h_hi, nt_dims, preferred_element_type=F32)) + br_ref[...]
    rg = lax.broadcasted_iota(jnp.int32, (SUBLANES, tm), 0)
    gl = jnp.where(rg < N_EXPERT_GROUPS, lt[0:SUBLANES], NEG_INF)
    ge = jnp.exp(gl - jnp.max(gl, axis=0, keepdims=True))
    gp = ge / jnp.sum(ge, axis=0, keepdims=True)
    g_val = jnp.max(gp, axis=0, keepdims=True)
    g_idx = jnp.min(jnp.where(gp == g_val, rg, SUBLANES), axis=0, keepdims=True)
    re = lax.broadcasted_iota(jnp.int32, (N_EXPERTS, tm), 0)
    sel = (re // EXPERTS_PER_GROUP) == g_idx
    el = jnp.where(sel, lt[ROUTER_EXPERT_ROW0:ROUTER_EXPERT_ROW0 + N_EXPERTS], NEG_INF)
    ee = jnp.exp(el - jnp.max(el, axis=0, keepdims=True))
    ep = jnp.where(sel, ee / jnp.sum(ee, axis=0, keepdims=True), -1.0)
    v1 = jnp.max(ep, axis=0, keepdims=True)
    i1 = jnp.min(jnp.where(ep == v1, re, N_EXPERTS), axis=0, keepdims=True)
    ep2 = jnp.where(re == i1, -1.0, ep)
    v2 = jnp.max(ep2, axis=0, keepdims=True)
    i2 = jnp.min(jnp.where(ep2 == v2, re, N_EXPERTS), axis=0, keepdims=True)
    scale = g_val / (v1 + v2)
    rl = lax.broadcasted_iota(jnp.int32, (LANES, tm), 0)
    info_t = jnp.where(rl == 0, v1 * scale, jnp.where(rl == 1, v2 * scale,
                       jnp.where(rl == 2, i1.astype(F32), jnp.where(rl == 3, i2.astype(F32), 0.0))))
    rinfo_ref[...] = info_t.T


ROUTER_EXPERT_ROW0 = SUBLANES
ROUTER_ROWS = 48


def _router_specs(tm, d):
    ins = [pl.BlockSpec((1, d), lambda i: (0, 0)),
           pl.BlockSpec((2, ROUTER_ROWS, d), lambda i: (0, 0, 0)),
           pl.BlockSpec((ROUTER_ROWS, 1), lambda i: (0, 0))]
    outs = [pl.BlockSpec((tm * SUBLANES, LANES), lambda i: (i, 0)),
            pl.BlockSpec((tm, LANES), lambda i: (i, 0))]
    return ins, outs


def _router_shapes(m):
    return [jax.ShapeDtypeStruct((m * SUBLANES, LANES), F32), jax.ShapeDtypeStruct((m, LANES), F32)]


def _outproj_kernel(n_prompt_tiles, xp_ref, xs_ref, ya_ref, yb_ref, ys_ref, wo_ref, g_ref, wr_ref, br_ref,
                    x1_ref, h2rows_ref, rinfo_ref):
    i = pl.program_id(0)
    is_p = i < n_prompt_tiles
    x = jnp.where(is_p, xp_ref[...], xs_ref[...])
    ymix = jnp.where(is_p, jnp.concatenate([ya_ref[...], _load_chunks(yb_ref)], axis=-1), ys_ref[...])
    x1 = x + _dot(ymix.astype(BF16), wo_ref[...])
    x1_ref[...] = x1
    _route(_rms(x1, g_ref[...]), wr_ref, br_ref, h2rows_ref, rinfo_ref)


def _outproj(xp, xs, ya_p, yb_p, ymix_s, wo, g, wr, br):
    mp, d = xp.shape
    ms = xs.shape[0]
    tm = ROW_TILE
    npt, nst = mp // tm, ms // tm
    m = mp + ms
    r_in, r_out = _router_specs(tm, d)
    pmap = lambda i: (jnp.minimum(i, npt - 1), 0)
    smap = lambda i: (jnp.maximum(i - npt, 0), 0)
    return pl.pallas_call(
        functools.partial(_outproj_kernel, npt),
        grid=(npt + nst,),
        in_specs=[
            pl.BlockSpec((tm, d), pmap),
            pl.BlockSpec((tm, d), smap),
            pl.BlockSpec((tm, GROUP_W), pmap),
            pl.BlockSpec((GROUP_W // LANES, tm, LANES), lambda i: (0, jnp.minimum(i, npt - 1), 0)),
            pl.BlockSpec((tm, 2 * GROUP_W), smap),
            pl.BlockSpec((2 * GROUP_W, d), lambda i: (0, 0)),
        ] + r_in,
        out_specs=[pl.BlockSpec((tm, d), lambda i: (i, 0))] + r_out,
        out_shape=[jax.ShapeDtypeStruct((m, d), F32)] + _router_shapes(m),
        compiler_params=_params("arbitrary"),
        name="outproj_router",
    )(xp, xs, ya_p, yb_p, ymix_s, wo, g, wr, br)


def _experts_kernel(te_ref, src_ref, dst_ref, nt_ref, h_hbm, wg_ref, wu_ref, wd_ref, y_hbm, xbuf, ybuf, gsem, ssem,
                    *, trash_row):
    i = pl.program_id(0)
    nt = nt_ref[0]
    te = EXPERT_TILE
    slot = i % 2
    other = 1 - slot

    def tile_rows(r):
        return pl.ds(r if isinstance(r, int) else pl.multiple_of(r, SUBLANES), SUBLANES)

    def gather_copy(p, sl, j):
        return pltpu.make_async_copy(h_hbm.at[tile_rows(src_ref[p]), :],
                                     xbuf.at[sl, tile_rows(j * SUBLANES), :], gsem.at[sl])

    def scatter_copy(p, sl, j):
        return pltpu.make_async_copy(ybuf.at[sl, tile_rows(j * SUBLANES), :],
                                     y_hbm.at[tile_rows(dst_ref[p]), :], ssem.at[sl])

    def wait_gather(sl):
        pltpu.make_async_copy(h_hbm.at[pl.ds(0, te * SUBLANES), :], xbuf.at[sl], gsem.at[sl]).wait()

    def wait_scatter(sl):
        pltpu.make_async_copy(ybuf.at[sl], y_hbm.at[pl.ds(0, te * SUBLANES), :], ssem.at[sl]).wait()

    @pl.when(i == 0)
    def _():
        ybuf[...] = jnp.zeros(ybuf.shape, F32)
        init = pltpu.make_async_copy(ybuf.at[0], y_hbm.at[pl.ds(trash_row, te * SUBLANES), :], ssem.at[0])
        init.start()
        init.wait()

        def body(j, c):
            gather_copy(j, 0, j).start()
            return c
        lax.fori_loop(0, te, body, 0)

    @pl.when(i < nt)
    def _():
        wait_gather(slot)

        @pl.when(i >= 1)
        def _():
            wait_scatter(slot)

        base = i * te
        for j in range(te):
            gather_copy(base + te + j, other, j).start(priority=j % 2)
            scatter_copy(base + j, other, j).start(priority=(j + 1) % 2)
        x = _rows_to_lanes(xbuf, te, base=slot).astype(BF16)
        gate = _dot(x, wg_ref[...].astype(BF16))
        up = _dot(x, wu_ref[...].astype(BF16))
        act = (gate * jax.nn.sigmoid(gate) * up).astype(BF16)
        y = _dot(act, wd_ref[...].astype(BF16))
        _lanes_to_rows(ybuf, y, base=slot)

    @pl.when(i == nt)
    def _():
        wait_gather(slot)
        wait_scatter(slot)

        def body(j, c):
            scatter_copy(i * te + j, other, j).start()
            return c
        lax.fori_loop(0, te, body, 0)
        wait_scatter(other)


def _experts(h2rows, tile_expert, src, dst, n_tiles, wg, wu, wd, layer, n_tokens):
    te = EXPERT_TILE
    nt_max = tile_expert.shape[0]
    d, f = wg.shape[2], wg.shape[3]
    n_rows = 2 * n_tokens + te
    wmap = lambda i, te_ref, src_ref, dst_ref, nt_ref: (layer, te_ref[i], 0, 0)
    grid_spec = pltpu.PrefetchScalarGridSpec(
        num_scalar_prefetch=4,
        grid=(nt_max,),
        in_specs=[
            pl.BlockSpec(memory_space=pl.ANY),
            pl.BlockSpec((None, None, d, f), wmap),
            pl.BlockSpec((None, None, d, f), wmap),
            pl.BlockSpec((None, None, f, d), wmap),
        ],
        out_specs=pl.BlockSpec(memory_space=pl.ANY),
        scratch_shapes=[
            pltpu.VMEM((2, te * SUBLANES, LANES), F32),
            pltpu.VMEM((2, te * SUBLANES, LANES), F32),
            pltpu.SemaphoreType.DMA((2,)),
            pltpu.SemaphoreType.DMA((2,)),
        ],
    )
    return pl.pallas_call(
        functools.partial(_experts_kernel, trash_row=2 * n_tokens * SUBLANES),
        grid_spec=grid_spec,
        out_shape=jax.ShapeDtypeStruct((n_rows * SUBLANES, LANES), F32),
        compiler_params=_params("arbitrary"),
        name="experts",
    )(tile_expert, src, dst, n_tiles, h2rows, wg, wu, wd)


def _plan_tiles(rinfo):
    m = rinfo.shape[0]
    te = EXPERT_TILE
    n_asg = 2 * m
    nt_max = n_asg // te + N_EXPERTS + 1
    ids = rinfo[:, 2:4].astype(jnp.int32).reshape(n_asg)
    key_sorted, order = lax.sort((ids, lax.iota(jnp.int32, n_asg)), num_keys=1, is_stable=True)
    counts = jnp.sum((ids[:, None] == jnp.arange(N_EXPERTS)[None, :]).astype(jnp.int32), axis=0)
    tiles = (counts + te - 1) // te
    tile_end = jnp.cumsum(tiles)
    tile_start = tile_end - tiles
    row_start = jnp.cumsum(counts) - counts
    n_tiles = tile_end[-1]
    tidx = jnp.arange(nt_max, dtype=jnp.int32)
    tile_expert = jnp.sum((tidx[:, None] >= tile_end[None, :]).astype(jnp.int32), axis=1)
    last_expert = jnp.max(jnp.where(counts > 0, jnp.arange(N_EXPERTS), 0))
    tile_expert = jnp.where(tidx < n_tiles, jnp.minimum(tile_expert, N_EXPERTS - 1), last_expert).astype(jnp.int32)
    within = (tidx - tile_start[tile_expert])[:, None] * te + jnp.arange(te, dtype=jnp.int32)[None, :]
    valid = (within < counts[tile_expert][:, None]) & (tidx < n_tiles)[:, None]
    pos = jnp.clip(row_start[tile_expert][:, None] + within, 0, n_asg - 1)
    asg = order[pos]
    tok, pick = asg >> 1, asg & 1
    pad_dst = (2 * m + jnp.arange(te, dtype=jnp.int32)) * SUBLANES
    src = jnp.where(valid, tok * SUBLANES, 0).astype(jnp.int32).reshape(nt_max * te)
    dst = jnp.where(valid, (pick * m + tok) * SUBLANES, pad_dst[None, :]).astype(jnp.int32).reshape(nt_max * te)
    dst = jnp.concatenate([pad_dst, dst])
    return tile_expert, src, dst, n_tiles.astype(jnp.int32).reshape(1)


def _combine(x_ref, ya_ref, yb_ref, rinfo_ref):
    tm = x_ref.shape[0]
    r = rinfo_ref[...]
    return x_ref[...] + r[:, 0:1] * _rows_to_lanes(ya_ref, tm) + r[:, 1:2] * _rows_to_lanes(yb_ref, tm)


def _combine_specs(tm, d, m):
    blocks_per_pick = m // tm
    return [
        pl.BlockSpec((tm, d), lambda i: (i, 0)),
        pl.BlockSpec((tm * SUBLANES, LANES), lambda i: (i, 0)),
        pl.BlockSpec((tm * SUBLANES, LANES), lambda i: (blocks_per_pick + i, 0)),
        pl.BlockSpec((tm, LANES), lambda i: (i, 0)),
    ]


def _combine_pw1_kernel(x_ref, ya_ref, yb_ref, rinfo_ref, g_ref, w_ref, b_ref, x2_ref, u_ref):
    x2 = _combine(x_ref, ya_ref, yb_ref, rinfo_ref)
    x2_ref[...] = x2
    d = x2.shape[1]
    z = _dot(_rms(x2, g_ref[...]).astype(BF16), w_ref[...]) + b_ref[...]
    u_ref[...] = z[:, :d] * jax.nn.sigmoid(z[:, d:])


def _combine_pw1(x1, y2, rinfo, g, w, b):
    m, d = x1.shape
    tm = ROW_TILE
    return pl.pallas_call(
        _combine_pw1_kernel,
        grid=(m // tm,),
        in_specs=_combine_specs(tm, d, m) + [
            pl.BlockSpec((1, d), lambda i: (0, 0)),
            pl.BlockSpec((d, 2 * d), lambda i: (0, 0)),
            pl.BlockSpec((1, 2 * d), lambda i: (0, 0)),
        ],
        out_specs=[pl.BlockSpec((tm, d), lambda i: (i, 0)), pl.BlockSpec((tm, d), lambda i: (i, 0))],
        out_shape=[jax.ShapeDtypeStruct((m, d), F32), jax.ShapeDtypeStruct((m, d), F32)],
        compiler_params=_params("arbitrary"),
        name="combine_pw1",
    )(x1, y2, y2, rinfo, g, w, b)


def _combine_final_kernel(n_prompt_tiles, x_ref, ya_ref, yb_ref, rinfo_ref, g_ref, yp_ref, ys_ref):
    i = pl.program_id(0)
    y = _rms(_combine(x_ref, ya_ref, yb_ref, rinfo_ref), g_ref[...])

    @pl.when(i < n_prompt_tiles)
    def _():
        yp_ref[...] = y

    @pl.when(i >= n_prompt_tiles)
    def _():
        ys_ref[...] = y


def _combine_final(x3, y2, rinfo, g, mp):
    m, d = x3.shape
    tm = ROW_TILE
    npt = mp // tm
    return pl.pallas_call(
        functools.partial(_combine_final_kernel, npt),
        grid=(m // tm,),
        in_specs=_combine_specs(tm, d, m) + [pl.BlockSpec((1, d), lambda i: (0, 0))],
        out_specs=[pl.BlockSpec((tm, d), lambda i: (jnp.minimum(i, npt - 1), 0)),
                   pl.BlockSpec((tm, d), lambda i: (jnp.maximum(i - npt, 0), 0))],
        out_shape=[jax.ShapeDtypeStruct((mp, d), F32), jax.ShapeDtypeStruct((m - mp, d), F32)],
        compiler_params=_params("arbitrary"),
        name="combine_final",
    )(x3, y2, y2, rinfo, g)


def _dwconv_prompt_kernel(tiles_per_seq, u_ref, halo_ref, w_ref, b_ref, c_ref, scr):
    i = pl.program_id(0)
    tm = u_ref.shape[0]
    first = (i % tiles_per_seq) == 0
    scr[0, 0:CONV_HALO, :] = jnp.where(first, 0.0, halo_ref[...])
    scr[0, CONV_HALO:, :] = u_ref[...]
    n_shift = tm + CONV_HALO - SUBLANES
    for s in range(1, SUBLANES):
        scr[s, 0:n_shift, :] = scr[0, pl.ds(s, n_shift), :]
    acc = jnp.broadcast_to(b_ref[...], c_ref.shape)
    for k in range(C_KERNEL):
        off = CONV_HALO - (C_KERNEL - 1) + k
        acc = acc + w_ref[k:k + 1, :] * scr[off % SUBLANES, pl.ds(off - off % SUBLANES, tm), :]
    c_ref[...] = acc


def _dwconv_prompt(u, w, b, mp, seq):
    tm = ROW_TILE
    d = u.shape[1]
    return pl.pallas_call(
        functools.partial(_dwconv_prompt_kernel, seq // tm),
        grid=(mp // tm,),
        in_specs=[
            pl.BlockSpec((tm, d), lambda i: (i, 0)),
            pl.BlockSpec((CONV_HALO, d), lambda i: (jnp.maximum(i * (tm // CONV_HALO) - 1, 0), 0)),
            pl.BlockSpec((C_KERNEL, d), lambda i: (0, 0)),
            pl.BlockSpec((1, d), lambda i: (0, 0)),
        ],
        out_specs=pl.BlockSpec((tm, d), lambda i: (i, 0)),
        out_shape=jax.ShapeDtypeStruct((mp, d), F32),
        scratch_shapes=[pltpu.VMEM((SUBLANES, tm + CONV_HALO, d), F32)],
        compiler_params=_params("arbitrary"),
        name="dwconv_prompt",
    )(u, u, w, b)


def _dwconv_sample_kernel(st_ref, u_ref, w_ref, b_ref, c_ref, ns_ref):
    n_state, t_len = st_ref.shape[0], u_ref.shape[0]
    rows = [st_ref[r] for r in range(n_state)] + [u_ref[t] for t in range(t_len)]
    for t in range(t_len):
        acc = jnp.broadcast_to(b_ref[...], rows[0].shape)
        for k in range(C_KERNEL):
            acc = acc + w_ref[k:k + 1, :] * rows[t + k]
        c_ref[t] = acc
    for r in range(n_state):
        ns_ref[r] = rows[r + t_len]


def _dwconv_sample(st, u3, w, b):
    n_state, n, d = st.shape
    t_len = u3.shape[0]
    sb = SEQ_BLOCK
    assert n_state == C_KERNEL - 1 and n % sb == 0
    return pl.pallas_call(
        _dwconv_sample_kernel,
        grid=(n // sb,),
        in_specs=[
            pl.BlockSpec((n_state, sb, d), lambda i: (0, i, 0)),
            pl.BlockSpec((t_len, sb, d), lambda i: (0, i, 0)),
            pl.BlockSpec((C_KERNEL, d), lambda i: (0, 0)),
            pl.BlockSpec((1, d), lambda i: (0, 0)),
        ],
        out_specs=[
            pl.BlockSpec((t_len, sb, d), lambda i: (0, i, 0)),
            pl.BlockSpec((n_state, sb, d), lambda i: (0, i, 0)),
        ],
        out_shape=[jax.ShapeDtypeStruct((t_len, n, d), F32), jax.ShapeDtypeStruct((n_state, n, d), F32)],
        compiler_params=_params("arbitrary"),
        name="dwconv_sample",
    )(st, u3, w, b)


def _conf_tail_kernel(n_prompt_tiles, x_ref, cp_ref, cs_ref, lg_ref, lb_ref, w_ref, b_ref, g_ref, wr_ref, br_ref,
                      x3_ref, h2rows_ref, rinfo_ref):
    i = pl.program_id(0)
    c = jnp.where(i < n_prompt_tiles, cp_ref[...], cs_ref[...])
    mu = jnp.mean(c, axis=-1, keepdims=True)
    cc = c - mu
    var = jnp.mean(cc * cc, axis=-1, keepdims=True)
    y = cc * lax.rsqrt(var + LN_EPS) * lg_ref[...] + lb_ref[...]
    y = y * jax.nn.sigmoid(y)
    x3 = x_ref[...] + _dot(y.astype(BF16), w_ref[...]) + b_ref[...]
    x3_ref[...] = x3
    _route(_rms(x3, g_ref[...]), wr_ref, br_ref, h2rows_ref, rinfo_ref)


def _conf_tail(x2, c_p, c_s, ln_g, ln_b, w, b, g, wr, br):
    m, d = x2.shape
    tm = ROW_TILE
    npt = c_p.shape[0] // tm
    r_in, r_out = _router_specs(tm, d)
    vec = pl.BlockSpec((1, d), lambda i: (0, 0))
    return pl.pallas_call(
        functools.partial(_conf_tail_kernel, npt),
        grid=(m // tm,),
        in_specs=[
            pl.BlockSpec((tm, d), lambda i: (i, 0)),
            pl.BlockSpec((tm, d), lambda i: (jnp.minimum(i, npt - 1), 0)),
            pl.BlockSpec((tm, d), lambda i: (jnp.maximum(i - npt, 0), 0)),
            vec, vec,
            pl.BlockSpec((d, d), lambda i: (0, 0)),
            vec,
        ] + r_in,
        out_specs=[pl.BlockSpec((tm, d), lambda i: (i, 0))] + r_out,
        out_shape=[jax.ShapeDtypeStruct((m, d), F32)] + _router_shapes(m),
        compiler_params=_params("arbitrary"),
        name="conf_tail_router",
    )(x2, c_p, c_s, ln_g, ln_b, w, b, g, wr, br)


def _router_weights(w_rg, b_rg, w_re, b_re):
    d = w_rg.shape[0]
    gpad = ROUTER_EXPERT_ROW0 - N_EXPERT_GROUPS
    epad = ROUTER_ROWS - ROUTER_EXPERT_ROW0 - N_EXPERTS
    wt = jnp.concatenate([w_rg.T, jnp.zeros((gpad, d), F32), w_re.T, jnp.zeros((epad, d), F32)], axis=0)
    w_hi = wt.astype(BF16)
    w_lo = (wt - w_hi.astype(F32)).astype(BF16)
    br = jnp.concatenate([b_rg, jnp.zeros((gpad,), F32), b_re, jnp.zeros((epad,), F32)])[:, None]
    return jnp.stack([w_hi, w_lo]), br


def _moe(h2rows, rinfo, wg, wu, wd, layer):
    m = rinfo.shape[0]
    tile_expert, src, dst, n_tiles = _plan_tiles(rinfo)
    return _experts(h2rows, tile_expert, src, dst, n_tiles, wg, wu, wd, layer, m)


def kernel(x_prompt, x_sample, state_shortconv, cache_kv_w128, cache_kv_w512, cache_kv_w2048, state_conformer,
           g_mix, g_ffn, g_final, w_in, conv_a_w, w_out, w_pw1, b_pw1, dw_w, dw_b, ln_g, ln_b, w_pw2, b_pw2,
           w_router_group, b_router_group, w_router_expert, b_router_expert, w_gate, w_up, w_down):
    batch, seq, d = x_prompt.shape
    n_dec, t_dec, _ = x_sample.shape
    mp, ms = batch * seq, n_dec * t_dec
    assert g_mix.shape[0] == 2 and mp % ROW_TILE == 0 and ms % ROW_TILE == 0 and seq % ROW_TILE == 0
    xp = x_prompt.reshape(mp, d)
    xs = x_sample.reshape(ms, d)
    slab = (HEADS, HEAD_DIM)

    aw, qw = 3 * GROUP_W, N_GROUPS * GROUP_W
    w0 = w_in[0]
    kcols = w0[:, aw + qw:aw + 2 * qw].reshape(d, N_GROUPS, GROUP_W)
    vcols = w0[:, aw + 2 * qw:].reshape(d, N_GROUPS, GROUP_W)
    w_perm = jnp.concatenate([w0[:, :aw + qw], jnp.stack([kcols, vcols], axis=2).reshape(d, 2 * qw)], axis=1)
    q, kv, gbu = _inproj(xp, xs, g_mix[0][None, :], w_perm.astype(BF16))

    yb_p = _attn_prompt(q, kv, batch, seq)
    ya_p = _shortconv_prompt(gbu, conv_a_w[0], mp, seq)

    hpc = LANES // HEAD_DIM
    q_s = jnp.transpose(q[:, mp:].reshape(N_GROUPS, HEADS // hpc, n_dec, t_dec, hpc, HEAD_DIM),
                        (2, 0, 1, 4, 3, 5)).reshape(n_dec, N_GROUPS, HEADS, t_dec, HEAD_DIM)
    kv_s = jnp.transpose(kv[:, mp:].reshape(N_GROUPS, 2, HEADS // hpc, n_dec, t_dec, hpc, HEAD_DIM),
                         (1, 3, 0, 2, 5, 4, 6)).reshape(2, n_dec, N_GROUPS, HEADS, t_dec, HEAD_DIM)
    gbu_s = gbu[mp:].reshape(n_dec, t_dec, 2 * GROUP_W)
    c0, c1, c2 = (jnp.transpose(c[0], (0, 2, 3, 4, 1)) for c in (cache_kv_w128, cache_kv_w512, cache_kv_w2048))
    yb_s, ya_s, s_sc = _mix_sample(q_s, kv_s[0], kv_s[1], gbu_s, state_shortconv[0], c0, c1, c2, conv_a_w[0])
    ymix_s = jnp.concatenate([ya_s.reshape(ms, GROUP_W),
                              jnp.transpose(yb_s, (0, 2, 1, 3)).reshape(ms, GROUP_W)], axis=-1)

    wr0, br0 = _router_weights(w_router_group[0], b_router_group[0], w_router_expert[0], b_router_expert[0])
    x1, h2rows, rinfo0 = _outproj(xp, xs, ya_p, yb_p, ymix_s, w_out[0].astype(BF16), g_ffn[0][None, :], wr0, br0)
    y2 = _moe(h2rows, rinfo0, w_gate, w_up, w_down, 0)

    x2, u = _combine_pw1(x1, y2, rinfo0, g_mix[1][None, :], w_pw1[0].astype(BF16), b_pw1[0][None, :])
    c_p = _dwconv_prompt(u, dw_w[0], dw_b[0][None, :], mp, seq)
    c_s3, s_cf = _dwconv_sample(jnp.transpose(state_conformer[0], (1, 0, 2)),
                                jnp.transpose(u[mp:].reshape(n_dec, t_dec, d), (1, 0, 2)), dw_w[0], dw_b[0][None, :])
    c_s = jnp.transpose(c_s3, (1, 0, 2)).reshape(ms, d)
    wr1, br1 = _router_weights(w_router_group[1], b_router_group[1], w_router_expert[1], b_router_expert[1])
    x3, h2rows1, rinfo1 = _conf_tail(x2, c_p, c_s, ln_g[0][None, :], ln_b[0][None, :],
                                     w_pw2[0].astype(BF16), b_pw2[0][None, :], g_ffn[1][None, :], wr1, br1)
    y2b = _moe(h2rows1, rinfo1, w_gate, w_up, w_down, 1)
    y_p, y_s = _combine_final(x3, y2b, rinfo1, g_final[None, :], mp)

    cpg = 2 * GROUP_W // LANES
    kv_p = kv[:, :mp].reshape(N_GROUPS, cpg, batch, seq, LANES)
    p_kv = [jnp.transpose(kv_p[gi, :, :, seq - min(w, seq):], (1, 2, 0, 3)).reshape(1, batch, min(w, seq), 2, *slab)
            for gi, (w, _) in enumerate(DIL_GROUPS)]
    s_kv = [jnp.transpose(kv_s[:, :, gi], (1, 3, 0, 2, 4))[None] for gi in range(N_GROUPS)]
    u_p = gbu[:mp, GROUP_W:].reshape(batch, seq, GROUP_W)
    p_sc = u_p[:, seq - (A_KERNEL - 1):][None]
    p_cf = u[:mp].reshape(batch, seq, d)[:, seq - (C_KERNEL - 1):][None]
    return (y_p.reshape(batch, seq, d), y_s.reshape(n_dec, t_dec, d), p_sc, p_kv[0], p_kv[1], p_kv[2], p_cf,
            s_sc[None], s_kv[0], s_kv[1], s_kv[2], jnp.transpose(s_cf, (1, 0, 2))[None])
```

```python
import functools

import jax
import jax.numpy as jnp
from jax import lax
from jax.experimental import pallas as pl
from jax.experimental.pallas import tpu as pltpu

F32 = jnp.float32
BF16 = jnp.bfloat16

DIL_GROUPS = ((128, 1), (512, 4), (2048, 16))
N_GROUPS = len(DIL_GROUPS)
HEADS = 4
HEAD_DIM = 64
GROUP_W = HEADS * HEAD_DIM
NK = DIL_GROUPS[0][0] // DIL_GROUPS[0][1]
A_KERNEL = 3
C_KERNEL = 31
N_EXPERT_GROUPS = 4
EXPERTS_PER_GROUP = 8
N_EXPERTS = N_EXPERT_GROUPS * EXPERTS_PER_GROUP
RMS_EPS = 1e-6
LN_EPS = 1e-5
NEG_INF = -1e30

SUBLANES = 8
LANES = 128
VMEM_LIMIT = 48 * 1024 * 1024
ATTN_VMEM_LIMIT = 60 * 1024 * 1024

ROW_TILE = 256
CHUNK_GROUPS = 32
CONV_HALO = 32
SEQ_BLOCK = 8
ATTN_SEQ_BLOCK = 2


def _params(*sem, vmem_limit=VMEM_LIMIT):
    return pltpu.CompilerParams(dimension_semantics=sem, vmem_limit_bytes=vmem_limit)


def _rms(x, g):
    return x * lax.rsqrt(jnp.mean(x * x, axis=-1, keepdims=True) + RMS_EPS) * g


def _dot(a, b):
    return jnp.dot(a, b, preferred_element_type=F32)


def _rows_to_lanes(ref, n_rows, base=None):
    parts = []
    for s in range(SUBLANES):
        idx = pl.ds(s, n_rows, stride=SUBLANES)
        parts.append(ref[idx, :] if base is None else ref[base, idx, :])
    return jnp.concatenate(parts, axis=-1)


def _lanes_to_rows(ref, val, base=None):
    n_rows = val.shape[0]
    for s in range(SUBLANES):
        idx = pl.ds(s, n_rows, stride=SUBLANES)
        piece = val[:, s * LANES:(s + 1) * LANES]
        if base is None:
            ref[idx, :] = piece
        else:
            ref[base, idx, :] = piece


def _inproj_kernel(n_prompt_tiles, xp_ref, xs_ref, g_ref, w_ref, q_ref, kv_ref, gbu_ref):
    i = pl.program_id(0)
    x = jnp.where(i < n_prompt_tiles, xp_ref[...], xs_ref[...])
    h = _rms(x, g_ref[...]).astype(BF16)
    aw = 3 * GROUP_W
    a = _dot(h, w_ref[:, 0:aw])
    gbu_ref[:, 0:GROUP_W] = a[:, 0:GROUP_W]
    gbu_ref[:, GROUP_W:2 * GROUP_W] = a[:, GROUP_W:2 * GROUP_W] * a[:, 2 * GROUP_W:aw]
    qw = N_GROUPS * GROUP_W
    _store_chunks(q_ref, _dot(h, w_ref[:, aw:aw + qw]) * (HEAD_DIM ** -0.5))
    _store_chunks(kv_ref, _dot(h, w_ref[:, aw + qw:]))


def _store_chunks(ref, val):
    for c in range(val.shape[1] // LANES):
        ref[c] = val[:, c * LANES:(c + 1) * LANES]


def _load_chunks(ref, rows=None):
    n = ref.shape[0]
    return jnp.concatenate([ref[c] if rows is None else ref[c, rows, :] for c in range(n)], axis=-1)


def _inproj(xp, xs, g, w):
    mp, d = xp.shape
    ms = xs.shape[0]
    tm = ROW_TILE
    npt, nst = mp // tm, ms // tm
    m = mp + ms
    ncols = w.shape[1]
    qw = N_GROUPS * GROUP_W
    return pl.pallas_call(
        functools.partial(_inproj_kernel, npt),
        grid=(npt + nst,),
        in_specs=[
            pl.BlockSpec((tm, d), lambda i: (jnp.minimum(i, npt - 1), 0)),
            pl.BlockSpec((tm, d), lambda i: (jnp.maximum(i - npt, 0), 0)),
            pl.BlockSpec((1, d), lambda i: (0, 0)),
            pl.BlockSpec((d, ncols), lambda i: (0, 0)),
        ],
        out_specs=[
            pl.BlockSpec((qw // LANES, tm, LANES), lambda i: (0, i, 0)),
            pl.BlockSpec((2 * qw // LANES, tm, LANES), lambda i: (0, i, 0)),
            pl.BlockSpec((tm, 2 * GROUP_W), lambda i: (i, 0)),
        ],
        out_shape=[
            jax.ShapeDtypeStruct((qw // LANES, m, LANES), F32),
            jax.ShapeDtypeStruct((2 * qw // LANES, m, LANES), F32),
            jax.ShapeDtypeStruct((m, 2 * GROUP_W), F32),
        ],
        compiler_params=_params("arbitrary"),
        name="inproj",
    )(xp, xs, g, w)


def _attn_prompt_kernel(q_ref, kv_ref, o_ref, m_scr, l_scr, acc_scr):
    g = pl.program_id(1)
    seq = q_ref.shape[1]

    @pl.when(g == 0)
    def _():
        m_scr[...] = jnp.full(m_scr.shape, NEG_INF, F32)
        l_scr[...] = jnp.zeros(l_scr.shape, F32)
        acc_scr[...] = jnp.zeros(acc_scr.shape, F32)

    qi = lax.broadcasted_iota(jnp.int32, (NK, 2 * NK), 0)
    kc = lax.broadcasted_iota(jnp.int32, (NK, 2 * NK), 1)
    band = (kc >= qi) & (kc <= qi + NK)
    lane = lax.broadcasted_iota(jnp.int32, (NK, LANES), 1)
    lane_kv = lax.broadcasted_iota(jnp.int32, (2 * NK, LANES), 1)
    ones_kv = jnp.ones((2 * NK, LANES), BF16)
    n_chunks = GROUP_W // LANES
    hpc = LANES // HEAD_DIM

    def block(blk, d, last):
        span = NK * d
        base = pl.multiple_of(blk * span, span)
        pbase = pl.multiple_of(jnp.maximum(blk - 1, 0) * span, span)
        mask = band & ((blk > 0) | (kc >= NK))
        for r in range(d):
            rows = pl.ds(base + r, NK, stride=d) if d > 1 else pl.ds(base, NK)
            prows = pl.ds(pbase + r, NK, stride=d) if d > 1 else pl.ds(pbase, NK)
            m_old = m_scr[rows, :]
            l_old = l_scr[rows, :]
            m_new, l_new = m_old, l_old
            for c in range(GROUP_W // LANES):
                qc = q_ref[c, rows, :].astype(BF16)
                kc_ = jnp.concatenate([kv_ref[c, prows, :], kv_ref[c, rows, :]], axis=0).astype(BF16)
                vc_ = jnp.concatenate([kv_ref[n_chunks + c, prows, :], kv_ref[n_chunks + c, rows, :]],
                                      axis=0).astype(BF16)
                acc = acc_scr[c, rows, :]
                scale, denom, contrib = None, None, None
                for hh in range(hpc):
                    h = c * hpc + hh
                    sel_q = (lane >= hh * HEAD_DIM) & (lane < (hh + 1) * HEAD_DIM)
                    sel_v = (lane_kv >= hh * HEAD_DIM) & (lane_kv < (hh + 1) * HEAD_DIM)
                    s = lax.dot_general(jnp.where(sel_q, qc, 0), kc_, (((1,), (1,)), ((), ())),
                                        preferred_element_type=F32)
                    s = jnp.where(mask, s, NEG_INF)
                    mo = m_old[:, h:h + 1]
                    mn = jnp.maximum(mo, jnp.max(s, axis=-1, keepdims=True))
                    p = jnp.exp(s - mn)
                    alpha = jnp.exp(mo - mn)
                    pv = _dot(p.astype(BF16), jnp.concatenate([jnp.where(sel_v, vc_, 0), ones_kv], axis=1))
                    ln = alpha * l_old[:, h:h + 1] + pv[:, LANES:]
                    contrib = pv[:, :LANES] if contrib is None else contrib + pv[:, :LANES]
                    scale = alpha if scale is None else jnp.where(sel_q, alpha, scale)
                    denom = ln if denom is None else jnp.where(sel_q, ln, denom)
                    m_new = jnp.where(lane == h, mn, m_new)
                    l_new = jnp.where(lane == h, ln, l_new)
                acc = scale * acc + contrib
                if last:
                    o_ref[c, rows, :] = acc / denom
                else:
                    acc_scr[c, rows, :] = acc
            if not last:
                m_scr[rows, :] = m_new
                l_scr[rows, :] = l_new

    for gi, (window, d) in enumerate(DIL_GROUPS):
        @pl.when(g == gi)
        def _(d=d, last=(gi == N_GROUPS - 1)):
            def body(blk, carry):
                block(blk, d, last)
                return carry
            lax.fori_loop(0, seq // (NK * d), body, 0)


def _attn_prompt(q, kv, batch, seq):
    assert seq % (NK * max(d for _, d in DIL_GROUPS)) == 0
    return pl.pallas_call(
        _attn_prompt_kernel,
        grid=(batch, N_GROUPS),
        in_specs=[
            pl.BlockSpec((GROUP_W // LANES, seq, LANES), lambda b, g: (g, b, 0)),
            pl.BlockSpec((2 * GROUP_W // LANES, seq, LANES), lambda b, g: (g, b, 0)),
        ],
        out_specs=pl.BlockSpec((GROUP_W // LANES, seq, LANES), lambda b, g: (0, b, 0)),
        out_shape=jax.ShapeDtypeStruct((GROUP_W // LANES, batch * seq, LANES), F32),
        scratch_shapes=[
            pltpu.VMEM((seq, LANES), F32),
            pltpu.VMEM((seq, LANES), F32),
            pltpu.VMEM((GROUP_W // LANES, seq, LANES), F32),
        ],
        compiler_params=_params("arbitrary", "arbitrary", vmem_limit=ATTN_VMEM_LIMIT),
        name="attn_prompt",
    )(q, kv)


def _shortconv_prompt_kernel(tiles_per_seq, gbu_ref, halo_ref, w_ref, o_ref, scr):
    i = pl.program_id(0)
    tm = gbu_ref.shape[0]
    gb = gbu_ref[:, 0:GROUP_W]
    u = gbu_ref[:, GROUP_W:]
    first = (i % tiles_per_seq) == 0
    scr[0:SUBLANES, :] = jnp.where(first, 0.0, halo_ref[:, GROUP_W:])
    scr[SUBLANES:, :] = u
    w = w_ref[...]
    conv = (w[0:1, :] * scr[pl.ds(SUBLANES - 2, tm), :] + w[1:2, :] * scr[pl.ds(SUBLANES - 1, tm), :]
            + w[2:3, :] * u)
    o_ref[...] = gb * conv


def _shortconv_prompt(gbu, w, mp, seq):
    tm = ROW_TILE
    return pl.pallas_call(
        functools.partial(_shortconv_prompt_kernel, seq // tm),
        grid=(mp // tm,),
        in_specs=[
            pl.BlockSpec((tm, 2 * GROUP_W), lambda i: (i, 0)),
            pl.BlockSpec((SUBLANES, 2 * GROUP_W), lambda i: (jnp.maximum(i * (tm // SUBLANES) - 1, 0), 0)),
            pl.BlockSpec((A_KERNEL, GROUP_W), lambda i: (0, 0)),
        ],
        out_specs=pl.BlockSpec((tm, GROUP_W), lambda i: (i, 0)),
        out_shape=jax.ShapeDtypeStruct((mp, GROUP_W), F32),
        scratch_shapes=[pltpu.VMEM((tm + SUBLANES, GROUP_W), F32)],
        compiler_params=_params("arbitrary"),
        name="shortconv_prompt",
    )(gbu, gbu, w)


def _mix_sample_kernel(q_ref, kn_ref, vn_ref, gbu_ref, st_ref, c0_ref, c1_ref, c2_ref, w_ref,
                       yb_ref, ya_ref, ns_ref):
    sb, t_len = q_ref.shape[0], q_ref.shape[3]
    w = w_ref[...]
    caches = (c0_ref, c1_ref, c2_ref)
    nt_dims = (((1,), (1,)), ((), ()))
    qn = lax.broadcasted_iota(jnp.int32, (t_len, t_len), 0)
    tn = lax.broadcasted_iota(jnp.int32, (t_len, t_len), 1)

    for s in range(sb):
        for h in range(HEADS):
            sc_parts, v_parts = [], []
            for gi, (window, d) in enumerate(DIL_GROUPS):
                c_ref = caches[gi]
                n_pos = c_ref.shape[-1]
                qg = q_ref[s, gi, h].astype(BF16)
                qc = lax.broadcasted_iota(jnp.int32, (t_len, n_pos), 0)
                pc = lax.broadcasted_iota(jnp.int32, (t_len, n_pos), 1)
                sc = _dot(qg, c_ref[s, 0, h].astype(BF16))
                sc_parts.append(jnp.where((((pc - qc) & (d - 1)) == 0) & (pc >= qc), sc, NEG_INF))
                v_parts.append((c_ref[s, 1, h].astype(BF16), True))
                sn = lax.dot_general(qg, kn_ref[s, gi, h].astype(BF16), nt_dims, preferred_element_type=F32)
                sc_parts.append(jnp.where((tn <= qn) & (((qn - tn) & (d - 1)) == 0), sn, NEG_INF))
                v_parts.append((vn_ref[s, gi, h].astype(BF16), False))
            mx = functools.reduce(jnp.maximum, [jnp.max(p, axis=-1, keepdims=True) for p in sc_parts])
            den = jnp.zeros((t_len, 1), F32)
            acc = jnp.zeros((t_len, HEAD_DIM), F32)
            for sc, (v, transposed) in zip(sc_parts, v_parts):
                p = jnp.exp(sc - mx)
                den = den + jnp.sum(p, axis=-1, keepdims=True)
                pb = p.astype(BF16)
                acc = acc + (lax.dot_general(pb, v, nt_dims, preferred_element_type=F32) if transposed
                             else _dot(pb, v))
            yb_ref[s, h] = acc / den
        gbu = gbu_ref[s]
        gb = gbu[:, 0:GROUP_W]
        u = gbu[:, GROUP_W:]
        st = st_ref[s]
        ext = [st[0:1], st[1:2]] + [u[t:t + 1] for t in range(t_len)]
        for t in range(t_len):
            conv = w[0:1] * ext[t] + w[1:2] * ext[t + 1] + w[2:3] * ext[t + 2]
            ya_ref[s, t:t + 1, :] = gb[t:t + 1] * conv
        for r in range(A_KERNEL - 1):
            ns_ref[s, r:r + 1, :] = ext[t_len + r]


def _mix_sample(q5, kn, vn, gbu3, st, c0, c1, c2, w):
    n, t_len = q5.shape[0], q5.shape[3]
    sb = ATTN_SEQ_BLOCK
    assert t_len >= A_KERNEL - 1 and n % sb == 0
    for c, (window, d) in zip((c0, c1, c2), DIL_GROUPS):
        assert c.shape[-1] == NK * d and d & (d - 1) == 0
    qspec = pl.BlockSpec((sb, N_GROUPS, HEADS, t_len, HEAD_DIM), lambda i: (i, 0, 0, 0, 0))
    cspec = lambda c: pl.BlockSpec((sb,) + c.shape[1:], lambda i: (i, 0, 0, 0, 0))
    return pl.pallas_call(
        _mix_sample_kernel,
        grid=(n // sb,),
        in_specs=[
            qspec, qspec, qspec,
            pl.BlockSpec((sb, t_len, 2 * GROUP_W), lambda i: (i, 0, 0)),
            pl.BlockSpec((sb, A_KERNEL - 1, GROUP_W), lambda i: (i, 0, 0)),
            cspec(c0), cspec(c1), cspec(c2),
            pl.BlockSpec((A_KERNEL, GROUP_W), lambda i: (0, 0)),
        ],
        out_specs=[
            pl.BlockSpec((sb, HEADS, t_len, HEAD_DIM), lambda i: (i, 0, 0, 0)),
            pl.BlockSpec((sb, t_len, GROUP_W), lambda i: (i, 0, 0)),
            pl.BlockSpec((sb, A_KERNEL - 1, GROUP_W), lambda i: (i, 0, 0)),
        ],
        out_shape=[
            jax.ShapeDtypeStruct((n, HEADS, t_len, HEAD_DIM), F32),
            jax.ShapeDtypeStruct((n, t_len, GROUP_W), F32),
            jax.ShapeDtypeStruct((n, A_KERNEL - 1, GROUP_W), F32),
        ],
        compiler_params=_params("arbitrary"),
        name="mix_sample",
    )(q5, kn, vn, gbu3, st, c0, c1, c2, w)


def _route(h2, wr_ref, br_ref, xs_ref, rinfo_ref, gtab_ref):
    tm = h2.shape[0]
    h_hi = h2.astype(BF16)
    h_lo = (h2 - h_hi.astype(F32)).astype(BF16)
    nt_dims = (((1,), (1,)), ((), ()))
    w_hi, w_lo = wr_ref[0], wr_ref[1]
    lt = (lax.dot_general(w_hi, h_hi, nt_dims, preferred_element_type=F32)
          + lax.dot_general(w_hi, h_lo, nt_dims, preferred_element_type=F32)
          + lax.dot_general(w_lo, h_hi, nt_dims, preferred_element_type=F32)) + br_ref[...]
    rg = lax.broadcasted_iota(jnp.int32, (SUBLANES, tm), 0)
    gl = jnp.where(rg < N_EXPERT_GROUPS, lt[0:SUBLANES], NEG_INF)
    ge = jnp.exp(gl - jnp.max(gl, axis=0, keepdims=True))
    gp = ge / jnp.sum(ge, axis=0, keepdims=True)
    g_val = jnp.max(gp, axis=0, keepdims=True)
    g_idx = jnp.min(jnp.where(gp == g_val, rg, SUBLANES), axis=0, keepdims=True)
    re = lax.broadcasted_iota(jnp.int32, (N_EXPERTS, tm), 0)
    sel = (re // EXPERTS_PER_GROUP) == g_idx
    el = jnp.where(sel, lt[ROUTER_EXPERT_ROW0:ROUTER_EXPERT_ROW0 + N_EXPERTS], NEG_INF)
    ee = jnp.exp(el - jnp.max(el, axis=0, keepdims=True))
    ep = jnp.where(sel, ee / jnp.sum(ee, axis=0, keepdims=True), -1.0)
    v1 = jnp.max(ep, axis=0, keepdims=True)
    i1 = jnp.min(jnp.where(ep == v1, re, N_EXPERTS), axis=0, keepdims=True)
    ep2 = jnp.where(re == i1, -1.0, ep)
    v2 = jnp.max(ep2, axis=0, keepdims=True)
    i2 = jnp.min(jnp.where(ep2 == v2, re, N_EXPERTS), axis=0, keepdims=True)
    scale = g_val / (v1 + v2)

    oh = [(re == i1).astype(F32), (re == i2).astype(F32)]
    cmat = (oh[0] + oh[1]).astype(BF16)
    earlier = (lax.broadcasted_iota(jnp.int32, (tm, tm), 0)
               < lax.broadcasted_iota(jnp.int32, (tm, tm), 1)).astype(BF16)
    before = _dot(cmat, earlier)
    cnt_row = lax.dot_general(jnp.ones((SUBLANES, tm), BF16), cmat, nt_dims, preferred_element_type=F32)
    grp_row = ((cnt_row.astype(jnp.int32) + (SUBLANES - 1)) // SUBLANES).astype(F32)
    lower = (lax.broadcasted_iota(jnp.int32, (N_EXPERTS, N_EXPERTS), 0)
             < lax.broadcasted_iota(jnp.int32, (N_EXPERTS, N_EXPERTS), 1)).astype(BF16)
    goff_row = _dot(grp_row.astype(BF16), lower)
    slot_rows = lax.broadcasted_iota(jnp.int32, (TILE_SLOTS, tm), 0)
    place = None
    pos = []
    for k in range(2):
        rank = jnp.sum(oh[k] * before, axis=0, keepdims=True)
        seg = _dot(goff_row.astype(BF16), oh[k].astype(BF16))[0:1]
        pos.append(seg * SUBLANES + rank)
        hit = slot_rows == pos[k].astype(jnp.int32)
        place = hit if place is None else place | hit
    xs_ref[...] = _dot(jnp.where(place, 1.0, 0.0).astype(BF16), h_hi)
    rl = lax.broadcasted_iota(jnp.int32, (LANES, tm), 0)
    info_t = jnp.where(rl == 0, v1 * scale, jnp.where(rl == 1, v2 * scale,
                       jnp.where(rl == 2, pos[0], jnp.where(rl == 3, pos[1], 0.0))))
    rinfo_ref[...] = info_t.T
    spread = (lax.broadcasted_iota(jnp.int32, (N_EXPERTS, LANES), 0)
              == lax.broadcasted_iota(jnp.int32, (N_EXPERTS, LANES), 1)).astype(BF16)
    gtab_ref[...] = _dot(grp_row.astype(BF16), spread).astype(jnp.int32)


ROUTER_EXPERT_ROW0 = SUBLANES
ROUTER_ROWS = 48
TILE_GROUPS = -(-(2 * ROW_TILE + (SUBLANES - 1) * N_EXPERTS) // SUBLANES)
TILE_SLOTS = TILE_GROUPS * SUBLANES


def _router_specs(tm, d):
    ins = [pl.BlockSpec((1, d), lambda i: (0, 0)),
           pl.BlockSpec((2, ROUTER_ROWS, d), lambda i: (0, 0, 0)),
           pl.BlockSpec((ROUTER_ROWS, 1), lambda i: (0, 0))]
    outs = [pl.BlockSpec((TILE_SLOTS, d), lambda i: (i, 0)),
            pl.BlockSpec((tm, LANES), lambda i: (i, 0)),
            pl.BlockSpec((None, SUBLANES, LANES), lambda i: (i, 0, 0))]
    return ins, outs


def _router_shapes(m, d):
    nb = m // ROW_TILE
    return [jax.ShapeDtypeStruct((nb * TILE_SLOTS, d), F32), jax.ShapeDtypeStruct((m, LANES), F32),
            jax.ShapeDtypeStruct((nb, SUBLANES, LANES), jnp.int32)]


def _outproj_kernel(n_prompt_tiles, xp_ref, xs_ref, ya_ref, yb_ref, ys_ref, wo_ref, g_ref, wr_ref, br_ref,
                    x1_ref, xs_out_ref, rinfo_ref, gtab_ref):
    i = pl.program_id(0)
    is_p = i < n_prompt_tiles
    x = jnp.where(is_p, xp_ref[...], xs_ref[...])
    ymix = jnp.where(is_p, jnp.concatenate([ya_ref[...], _load_chunks(yb_ref)], axis=-1), ys_ref[...])
    x1 = x + _dot(ymix.astype(BF16), wo_ref[...])
    x1_ref[...] = x1
    _route(_rms(x1, g_ref[...]), wr_ref, br_ref, xs_out_ref, rinfo_ref, gtab_ref)


def _outproj(xp, xs, ya_p, yb_p, ymix_s, wo, g, wr, br):
    mp, d = xp.shape
    ms = xs.shape[0]
    tm = ROW_TILE
    npt, nst = mp // tm, ms // tm
    m = mp + ms
    r_in, r_out = _router_specs(tm, d)
    pmap = lambda i: (jnp.minimum(i, npt - 1), 0)
    smap = lambda i: (jnp.maximum(i - npt, 0), 0)
    return pl.pallas_call(
        functools.partial(_outproj_kernel, npt),
        grid=(npt + nst,),
        in_specs=[
            pl.BlockSpec((tm, d), pmap),
            pl.BlockSpec((tm, d), smap),
            pl.BlockSpec((tm, GROUP_W), pmap),
            pl.BlockSpec((GROUP_W // LANES, tm, LANES), lambda i: (0, jnp.minimum(i, npt - 1), 0)),
            pl.BlockSpec((tm, 2 * GROUP_W), smap),
            pl.BlockSpec((2 * GROUP_W, d), lambda i: (0, 0)),
        ] + r_in,
        out_specs=[pl.BlockSpec((tm, d), lambda i: (i, 0))] + r_out,
        out_shape=[jax.ShapeDtypeStruct((m, d), F32)] + _router_shapes(m, d),
        compiler_params=_params("arbitrary"),
        name="outproj_router",
    )(xp, xs, ya_p, yb_p, ymix_s, wo, g, wr, br)


def _experts_kernel(te_ref, src_ref, dst_ref, nt_ref, h_hbm, wg_ref, wu_ref, wd_ref, y_hbm, xbuf, ybuf, gsem, ssem,
                    *, trash_row):
    i = pl.program_id(0)
    nt = nt_ref[0]
    ng = CHUNK_GROUPS
    slot = i % 2
    other = 1 - slot

    def group_rows(r):
        return pl.ds(r if isinstance(r, int) else pl.multiple_of(r, SUBLANES), SUBLANES)

    def gather_copy(p, sl, j):
        return pltpu.make_async_copy(h_hbm.at[group_rows(src_ref[p]), :],
                                     xbuf.at[sl, group_rows(j * SUBLANES), :], gsem.at[sl])

    def scatter_copy(p, sl, j):
        return pltpu.make_async_copy(ybuf.at[sl, group_rows(j * SUBLANES), :],
                                     y_hbm.at[group_rows(dst_ref[p]), :], ssem.at[sl])

    def wait_gather(sl):
        pltpu.make_async_copy(h_hbm.at[pl.ds(0, ng * SUBLANES), :], xbuf.at[sl], gsem.at[sl]).wait()

    def wait_scatter(sl):
        pltpu.make_async_copy(ybuf.at[sl], y_hbm.at[pl.ds(0, ng * SUBLANES), :], ssem.at[sl]).wait()

    @pl.when(i == 0)
    def _():
        ybuf[...] = jnp.zeros(ybuf.shape, F32)
        init = pltpu.make_async_copy(ybuf.at[0], y_hbm.at[pl.ds(trash_row, ng * SUBLANES), :], ssem.at[0])
        init.start()
        init.wait()

        def tail_copy(b):
            first = pl.multiple_of(b * TILE_SLOTS + 2 * ROW_TILE, SUBLANES)
            return pltpu.make_async_copy(ybuf.at[0, pl.ds(0, TILE_SLOTS - 2 * ROW_TILE), :],
                                         y_hbm.at[pl.ds(first, TILE_SLOTS - 2 * ROW_TILE), :], ssem.at[0])

        def start_body(b, c):
            tail_copy(b).start()
            return c

        def wait_body(b, c):
            tail_copy(b).wait()
            return c
        lax.fori_loop(0, trash_row // TILE_SLOTS, start_body, 0)
        lax.fori_loop(0, trash_row // TILE_SLOTS, wait_body, 0)
        for j in range(ng):
            gather_copy(j, 0, j).start()

    @pl.when(i < nt)
    def _():
        wait_gather(slot)

        @pl.when(i >= 1)
        def _():
            wait_scatter(slot)

        base = i * ng
        for j in range(ng):
            gather_copy(base + ng + j, other, j).start(priority=j % 2)
            scatter_copy(base + j, other, j).start(priority=(j + 1) % 2)
        x = xbuf[slot].astype(BF16)
        gate = _dot(x, wg_ref[...].astype(BF16))
        up = _dot(x, wu_ref[...].astype(BF16))
        act = (gate * jax.nn.sigmoid(gate) * up).astype(BF16)
        ybuf[slot] = _dot(act, wd_ref[...].astype(BF16))

    @pl.when(i == nt)
    def _():
        wait_gather(slot)
        wait_scatter(slot)
        for j in range(ng):
            scatter_copy(i * ng + j, other, j).start()
        wait_scatter(other)


def _experts(xs, tile_expert, src, dst, n_tiles, wg, wu, wd, layer):
    ng = CHUNK_GROUPS
    nt_max = tile_expert.shape[0]
    d, f = wg.shape[2], wg.shape[3]
    n_rows = xs.shape[0] + ng * SUBLANES
    wmap = lambda i, te_ref, src_ref, dst_ref, nt_ref: (layer, te_ref[i], 0, 0)
    grid_spec = pltpu.PrefetchScalarGridSpec(
        num_scalar_prefetch=4,
        grid=(nt_max,),
        in_specs=[
            pl.BlockSpec(memory_space=pl.ANY),
            pl.BlockSpec((None, None, d, f), wmap),
            pl.BlockSpec((None, None, d, f), wmap),
            pl.BlockSpec((None, None, f, d), wmap),
        ],
        out_specs=pl.BlockSpec(memory_space=pl.ANY),
        scratch_shapes=[
            pltpu.VMEM((2, ng * SUBLANES, d), F32),
            pltpu.VMEM((2, ng * SUBLANES, d), F32),
            pltpu.SemaphoreType.DMA((2,)),
            pltpu.SemaphoreType.DMA((2,)),
        ],
    )
    return pl.pallas_call(
        functools.partial(_experts_kernel, trash_row=xs.shape[0]),
        grid_spec=grid_spec,
        out_shape=jax.ShapeDtypeStruct((n_rows, d), F32),
        compiler_params=_params("arbitrary"),
        name="experts",
    )(tile_expert, src, dst, n_tiles, xs, wg, wu, wd)


def _plan_chunks(gtab):
    ng = CHUNK_GROUPS
    grp = gtab[:, 0, :N_EXPERTS]
    nb = grp.shape[0]
    nt_max = -(-nb * TILE_GROUPS // ng) + N_EXPERTS + 1
    total = jnp.sum(grp, axis=0)
    chunks = (total + ng - 1) // ng
    chunk_end = jnp.cumsum(chunks)
    chunk_start = chunk_end - chunks
    n_chunks = chunk_end[-1]
    seg_off = jnp.cumsum(grp, axis=1) - grp
    seg_end = jnp.cumsum(grp, axis=0)
    seg_start = seg_end - grp
    cidx = jnp.arange(nt_max, dtype=jnp.int32)
    chunk_expert = jnp.sum((cidx[:, None] >= chunk_end[None, :]).astype(jnp.int32), axis=1)
    last_expert = jnp.max(jnp.where(total > 0, jnp.arange(N_EXPERTS), 0))
    chunk_expert = jnp.where(cidx < n_chunks, jnp.minimum(chunk_expert, N_EXPERTS - 1), last_expert).astype(jnp.int32)
    within = (cidx - chunk_start[chunk_expert])[:, None] * ng + jnp.arange(ng, dtype=jnp.int32)[None, :]
    valid = (within < total[chunk_expert][:, None]) & (cidx < n_chunks)[:, None]
    ends = seg_end.T[chunk_expert]
    tile = jnp.minimum(jnp.sum((within[:, :, None] >= ends[:, None, :]).astype(jnp.int32), axis=-1), nb - 1)
    e2 = jnp.broadcast_to(chunk_expert[:, None], tile.shape)
    group = tile * TILE_GROUPS + seg_off[tile, e2] + within - seg_start[tile, e2]
    pad_dst = (nb * TILE_GROUPS + jnp.arange(ng, dtype=jnp.int32)) * SUBLANES
    src = jnp.where(valid, group * SUBLANES, 0).astype(jnp.int32).reshape(nt_max * ng)
    dst = jnp.where(valid, group * SUBLANES, pad_dst[None, :]).astype(jnp.int32).reshape(nt_max * ng)
    dst = jnp.concatenate([pad_dst, dst])
    return chunk_expert, src, dst, n_chunks.astype(jnp.int32).reshape(1)


def _combine(x_ref, y_ref, rinfo_ref):
    tm = x_ref.shape[0]
    n_slots = y_ref.shape[0]
    r = rinfo_ref[...]
    y = y_ref[...].astype(BF16)
    col = lax.broadcasted_iota(jnp.int32, (tm, n_slots), 1)
    out = x_ref[...]
    for k in range(2):
        pick = jnp.where(col == r[:, 2 + k:3 + k].astype(jnp.int32), 1.0, 0.0).astype(BF16)
        out = out + r[:, k:k + 1] * _dot(pick, y)
    return out


def _combine_specs(tm, d):
    return [
        pl.BlockSpec((tm, d), lambda i: (i, 0)),
        pl.BlockSpec((TILE_SLOTS, d), lambda i: (i, 0)),
        pl.BlockSpec((tm, LANES), lambda i: (i, 0)),
    ]


def _combine_pw1_kernel(x_ref, y_ref, rinfo_ref, g_ref, w_ref, b_ref, x2_ref, u_ref):
    x2 = _combine(x_ref, y_ref, rinfo_ref)
    x2_ref[...] = x2
    d = x2.shape[1]
    z = _dot(_rms(x2, g_ref[...]).astype(BF16), w_ref[...]) + b_ref[...]
    u_ref[...] = z[:, :d] * jax.nn.sigmoid(z[:, d:])


def _combine_pw1(x1, y2, rinfo, g, w, b):
    m, d = x1.shape
    tm = ROW_TILE
    return pl.pallas_call(
        _combine_pw1_kernel,
        grid=(m // tm,),
        in_specs=_combine_specs(tm, d) + [
            pl.BlockSpec((1, d), lambda i: (0, 0)),
            pl.BlockSpec((d, 2 * d), lambda i: (0, 0)),
            pl.BlockSpec((1, 2 * d), lambda i: (0, 0)),
        ],
        out_specs=[pl.BlockSpec((tm, d), lambda i: (i, 0)), pl.BlockSpec((tm, d), lambda i: (i, 0))],
        out_shape=[jax.ShapeDtypeStruct((m, d), F32), jax.ShapeDtypeStruct((m, d), F32)],
        compiler_params=_params("arbitrary"),
        name="combine_pw1",
    )(x1, y2, rinfo, g, w, b)


def _combine_final_kernel(n_prompt_tiles, x_ref, y_ref, rinfo_ref, g_ref, yp_ref, ys_ref):
    i = pl.program_id(0)
    y = _rms(_combine(x_ref, y_ref, rinfo_ref), g_ref[...])

    @pl.when(i < n_prompt_tiles)
    def _():
        yp_ref[...] = y

    @pl.when(i >= n_prompt_tiles)
    def _():
        ys_ref[...] = y


def _combine_final(x3, y2, rinfo, g, mp):
    m, d = x3.shape
    tm = ROW_TILE
    npt = mp // tm
    return pl.pallas_call(
        functools.partial(_combine_final_kernel, npt),
        grid=(m // tm,),
        in_specs=_combine_specs(tm, d) + [pl.BlockSpec((1, d), lambda i: (0, 0))],
        out_specs=[pl.BlockSpec((tm, d), lambda i: (jnp.minimum(i, npt - 1), 0)),
                   pl.BlockSpec((tm, d), lambda i: (jnp.maximum(i - npt, 0), 0))],
        out_shape=[jax.ShapeDtypeStruct((mp, d), F32), jax.ShapeDtypeStruct((m - mp, d), F32)],
        compiler_params=_params("arbitrary"),
        name="combine_final",
    )(x3, y2, rinfo, g)


def _dwconv_prompt_kernel(tiles_per_seq, u_ref, halo_ref, w_ref, b_ref, c_ref, scr):
    i = pl.program_id(0)
    tm = u_ref.shape[0]
    first = (i % tiles_per_seq) == 0
    scr[0, 0:CONV_HALO, :] = jnp.where(first, 0.0, halo_ref[...])
    scr[0, CONV_HALO:, :] = u_ref[...]
    n_shift = tm + CONV_HALO - SUBLANES
    for s in range(1, SUBLANES):
        scr[s, 0:n_shift, :] = scr[0, pl.ds(s, n_shift), :]
    acc = jnp.broadcast_to(b_ref[...], c_ref.shape)
    for k in range(C_KERNEL):
        off = CONV_HALO - (C_KERNEL - 1) + k
        acc = acc + w_ref[k:k + 1, :] * scr[off % SUBLANES, pl.ds(off - off % SUBLANES, tm), :]
    c_ref[...] = acc


def _dwconv_prompt(u, w, b, mp, seq):
    tm = ROW_TILE
    d = u.shape[1]
    return pl.pallas_call(
        functools.partial(_dwconv_prompt_kernel, seq // tm),
        grid=(mp // tm,),
        in_specs=[
            pl.BlockSpec((tm, d), lambda i: (i, 0)),
            pl.BlockSpec((CONV_HALO, d), lambda i: (jnp.maximum(i * (tm // CONV_HALO) - 1, 0), 0)),
            pl.BlockSpec((C_KERNEL, d), lambda i: (0, 0)),
            pl.BlockSpec((1, d), lambda i: (0, 0)),
        ],
        out_specs=pl.BlockSpec((tm, d), lambda i: (i, 0)),
        out_shape=jax.ShapeDtypeStruct((mp, d), F32),
        scratch_shapes=[pltpu.VMEM((SUBLANES, tm + CONV_HALO, d), F32)],
        compiler_params=_params("arbitrary"),
        name="dwconv_prompt",
    )(u, u, w, b)


def _dwconv_sample_kernel(st_ref, u_ref, w_ref, b_ref, c_ref, ns_ref):
    n_state, t_len = st_ref.shape[0], u_ref.shape[0]
    rows = [st_ref[r] for r in range(n_state)] + [u_ref[t] for t in range(t_len)]
    for t in range(t_len):
        acc = jnp.broadcast_to(b_ref[...], rows[0].shape)
        for k in range(C_KERNEL):
            acc = acc + w_ref[k:k + 1, :] * rows[t + k]
        c_ref[t] = acc
    for r in range(n_state):
        ns_ref[r] = rows[r + t_len]


def _dwconv_sample(st, u3, w, b):
    n_state, n, d = st.shape
    t_len = u3.shape[0]
    sb = SEQ_BLOCK
    assert n_state == C_KERNEL - 1 and n % sb == 0
    return pl.pallas_call(
        _dwconv_sample_kernel,
        grid=(n // sb,),
        in_specs=[
            pl.BlockSpec((n_state, sb, d), lambda i: (0, i, 0)),
            pl.BlockSpec((t_len, sb, d), lambda i: (0, i, 0)),
            pl.BlockSpec((C_KERNEL, d), lambda i: (0, 0)),
            pl.BlockSpec((1, d), lambda i: (0, 0)),
        ],
        out_specs=[
            pl.BlockSpec((t_len, sb, d), lambda i: (0, i, 0)),
            pl.BlockSpec((n_state, sb, d), lambda i: (0, i, 0)),
        ],
        out_shape=[jax.ShapeDtypeStruct((t_len, n, d), F32), jax.ShapeDtypeStruct((n_state, n, d), F32)],
        compiler_params=_params("arbitrary"),
        name="dwconv_sample",
    )(st, u3, w, b)


def _conf_tail_kernel(n_prompt_tiles, x_ref, cp_ref, cs_ref, lg_ref, lb_ref, w_ref, b_ref, g_ref, wr_ref, br_ref,
                      x3_ref, xs_out_ref, rinfo_ref, gtab_ref):
    i = pl.program_id(0)
    c = jnp.where(i < n_prompt_tiles, cp_ref[...], cs_ref[...])
    mu = jnp.mean(c, axis=-1, keepdims=True)
    cc = c - mu
    var = jnp.mean(cc * cc, axis=-1, keepdims=True)
    y = cc * lax.rsqrt(var + LN_EPS) * lg_ref[...] + lb_ref[...]
    y = y * jax.nn.sigmoid(y)
    x3 = x_ref[...] + _dot(y.astype(BF16), w_ref[...]) + b_ref[...]
    x3_ref[...] = x3
    _route(_rms(x3, g_ref[...]), wr_ref, br_ref, xs_out_ref, rinfo_ref, gtab_ref)


def _conf_tail(x2, c_p, c_s, ln_g, ln_b, w, b, g, wr, br):
    m, d = x2.shape
    tm = ROW_TILE
    npt = c_p.shape[0] // tm
    r_in, r_out = _router_specs(tm, d)
    vec = pl.BlockSpec((1, d), lambda i: (0, 0))
    return pl.pallas_call(
        functools.partial(_conf_tail_kernel, npt),
        grid=(m // tm,),
        in_specs=[
            pl.BlockSpec((tm, d), lambda i: (i, 0)),
            pl.BlockSpec((tm, d), lambda i: (jnp.minimum(i, npt - 1), 0)),
            pl.BlockSpec((tm, d), lambda i: (jnp.maximum(i - npt, 0), 0)),
            vec, vec,
            pl.BlockSpec((d, d), lambda i: (0, 0)),
            vec,
        ] + r_in,
        out_specs=[pl.BlockSpec((tm, d), lambda i: (i, 0))] + r_out,
        out_shape=[jax.ShapeDtypeStruct((m, d), F32)] + _router_shapes(m, d),
        compiler_params=_params("arbitrary"),
        name="conf_tail_router",
    )(x2, c_p, c_s, ln_g, ln_b, w, b, g, wr, br)


def _router_weights(w_rg, b_rg, w_re, b_re):
    d = w_rg.shape[0]
    gpad = ROUTER_EXPERT_ROW0 - N_EXPERT_GROUPS
    epad = ROUTER_ROWS - ROUTER_EXPERT_ROW0 - N_EXPERTS
    wt = jnp.concatenate([w_rg.T, jnp.zeros((gpad, d), F32), w_re.T, jnp.zeros((epad, d), F32)], axis=0)
    w_hi = wt.astype(BF16)
    w_lo = (wt - w_hi.astype(F32)).astype(BF16)
    br = jnp.concatenate([b_rg, jnp.zeros((gpad,), F32), b_re, jnp.zeros((epad,), F32)])[:, None]
    return jnp.stack([w_hi, w_lo]), br


def _moe(xs, gtab, wg, wu, wd, layer):
    chunk_expert, src, dst, n_chunks = _plan_chunks(gtab)
    return _experts(xs, chunk_expert, src, dst, n_chunks, wg, wu, wd, layer)


def kernel(x_prompt, x_sample, state_shortconv, cache_kv_w128, cache_kv_w512, cache_kv_w2048, state_conformer,
           g_mix, g_ffn, g_final, w_in, conv_a_w, w_out, w_pw1, b_pw1, dw_w, dw_b, ln_g, ln_b, w_pw2, b_pw2,
           w_router_group, b_router_group, w_router_expert, b_router_expert, w_gate, w_up, w_down):
    batch, seq, d = x_prompt.shape
    n_dec, t_dec, _ = x_sample.shape
    mp, ms = batch * seq, n_dec * t_dec
    assert g_mix.shape[0] == 2 and mp % ROW_TILE == 0 and ms % ROW_TILE == 0 and seq % ROW_TILE == 0
    xp = x_prompt.reshape(mp, d)
    xs = x_sample.reshape(ms, d)
    slab = (HEADS, HEAD_DIM)

    aw, qw = 3 * GROUP_W, N_GROUPS * GROUP_W
    w0 = w_in[0]
    kcols = w0[:, aw + qw:aw + 2 * qw].reshape(d, N_GROUPS, GROUP_W)
    vcols = w0[:, aw + 2 * qw:].reshape(d, N_GROUPS, GROUP_W)
    w_perm = jnp.concatenate([w0[:, :aw + qw], jnp.stack([kcols, vcols], axis=2).reshape(d, 2 * qw)], axis=1)
    q, kv, gbu = _inproj(xp, xs, g_mix[0][None, :], w_perm.astype(BF16))

    yb_p = _attn_prompt(q, kv, batch, seq)
    ya_p = _shortconv_prompt(gbu, conv_a_w[0], mp, seq)

    hpc = LANES // HEAD_DIM
    q_s = jnp.transpose(q[:, mp:].reshape(N_GROUPS, HEADS // hpc, n_dec, t_dec, hpc, HEAD_DIM),
                        (2, 0, 1, 4, 3, 5)).reshape(n_dec, N_GROUPS, HEADS, t_dec, HEAD_DIM)
    kv_s = jnp.transpose(kv[:, mp:].reshape(N_GROUPS, 2, HEADS // hpc, n_dec, t_dec, hpc, HEAD_DIM),
                         (1, 3, 0, 2, 5, 4, 6)).reshape(2, n_dec, N_GROUPS, HEADS, t_dec, HEAD_DIM)
    gbu_s = gbu[mp:].reshape(n_dec, t_dec, 2 * GROUP_W)
    c0, c1, c2 = (jnp.transpose(c[0], (0, 2, 3, 4, 1)) for c in (cache_kv_w128, cache_kv_w512, cache_kv_w2048))
    yb_s, ya_s, s_sc = _mix_sample(q_s, kv_s[0], kv_s[1], gbu_s, state_shortconv[0], c0, c1, c2, conv_a_w[0])
    ymix_s = jnp.concatenate([ya_s.reshape(ms, GROUP_W),
                              jnp.transpose(yb_s, (0, 2, 1, 3)).reshape(ms, GROUP_W)], axis=-1)

    wr0, br0 = _router_weights(w_router_group[0], b_router_group[0], w_router_expert[0], b_router_expert[0])
    x1, xsort0, rinfo0, gtab0 = _outproj(xp, xs, ya_p, yb_p, ymix_s, w_out[0].astype(BF16), g_ffn[0][None, :],
                                         wr0, br0)
    y2 = _moe(xsort0, gtab0, w_gate, w_up, w_down, 0)

    x2, u = _combine_pw1(x1, y2, rinfo0, g_mix[1][None, :], w_pw1[0].astype(BF16), b_pw1[0][None, :])
    c_p = _dwconv_prompt(u, dw_w[0], dw_b[0][None, :], mp, seq)
    c_s3, s_cf = _dwconv_sample(jnp.transpose(state_conformer[0], (1, 0, 2)),
                                jnp.transpose(u[mp:].reshape(n_dec, t_dec, d), (1, 0, 2)), dw_w[0], dw_b[0][None, :])
    c_s = jnp.transpose(c_s3, (1, 0, 2)).reshape(ms, d)
    wr1, br1 = _router_weights(w_router_group[1], b_router_group[1], w_router_expert[1], b_router_expert[1])
    x3, xsort1, rinfo1, gtab1 = _conf_tail(x2, c_p, c_s, ln_g[0][None, :], ln_b[0][None, :],
                                           w_pw2[0].astype(BF16), b_pw2[0][None, :], g_ffn[1][None, :], wr1, br1)
    y2b = _moe(xsort1, gtab1, w_gate, w_up, w_down, 1)
    y_p, y_s = _combine_final(x3, y2b, rinfo1, g_final[None, :], mp)

    cpg = 2 * GROUP_W // LANES
    kv_p = kv[:, :mp].reshape(N_GROUPS, cpg, batch, seq, LANES)
    p_kv = [jnp.transpose(kv_p[gi, :, :, seq - min(w, seq):], (1, 2, 0, 3)).reshape(1, batch, min(w, seq), 2, *slab)
            for gi, (w, _) in enumerate(DIL_GROUPS)]
    s_kv = [jnp.transpose(kv_s[:, :, gi], (1, 3, 0, 2, 4))[None] for gi in range(N_GROUPS)]
    u_p = gbu[:mp, GROUP_W:].reshape(batch, seq, GROUP_W)
    p_sc = u_p[:, seq - (A_KERNEL - 1):][None]
    p_cf = u[:mp].reshape(batch, seq, d)[:, seq - (C_KERNEL - 1):][None]
    return (y_p.reshape(batch, seq, d), y_s.reshape(n_dec, t_dec, d), p_sc, p_kv[0], p_kv[1], p_kv[2], p_cf,
            s_sc[None], s_kv[0], s_kv[1], s_kv[2], jnp.transpose(s_cf, (1, 0, 2))[None])
```

```python
import functools

import jax
import jax.numpy as jnp
from jax import lax
from jax.experimental import pallas as pl
from jax.experimental.pallas import tpu as pltpu

F32 = jnp.float32
BF16 = jnp.bfloat16

DIL_GROUPS = ((128, 1), (512, 4), (2048, 16))
N_GROUPS = len(DIL_GROUPS)
HEADS = 4
HEAD_DIM = 64
GROUP_W = HEADS * HEAD_DIM
NK = DIL_GROUPS[0][0] // DIL_GROUPS[0][1]
A_KERNEL = 3
C_KERNEL = 31
N_EXPERT_GROUPS = 4
EXPERTS_PER_GROUP = 8
N_EXPERTS = N_EXPERT_GROUPS * EXPERTS_PER_GROUP
RMS_EPS = 1e-6
LN_EPS = 1e-5
NEG_INF = -1e30

SUBLANES = 8
LANES = 128
VMEM_LIMIT = 48 * 1024 * 1024
ATTN_VMEM_LIMIT = 60 * 1024 * 1024

ROW_TILE = 256
CHUNK_GROUPS = 32
CONV_HALO = 32
SEQ_BLOCK = 8
ATTN_SEQ_BLOCK = 2


def _params(*sem, vmem_limit=VMEM_LIMIT):
    return pltpu.CompilerParams(dimension_semantics=sem, vmem_limit_bytes=vmem_limit)


def _rms(x, g):
    return x * lax.rsqrt(jnp.mean(x * x, axis=-1, keepdims=True) + RMS_EPS) * g


def _dot(a, b):
    return jnp.dot(a, b, preferred_element_type=F32)


def _rows_to_lanes(ref, n_rows, base=None):
    parts = []
    for s in range(SUBLANES):
        idx = pl.ds(s, n_rows, stride=SUBLANES)
        parts.append(ref[idx, :] if base is None else ref[base, idx, :])
    return jnp.concatenate(parts, axis=-1)


def _lanes_to_rows(ref, val, base=None):
    n_rows = val.shape[0]
    for s in range(SUBLANES):
        idx = pl.ds(s, n_rows, stride=SUBLANES)
        piece = val[:, s * LANES:(s + 1) * LANES]
        if base is None:
            ref[idx, :] = piece
        else:
            ref[base, idx, :] = piece


def _inproj_kernel(n_prompt_tiles, xp_ref, xs_ref, g_ref, w_ref, q_ref, kv_ref, gbu_ref):
    i = pl.program_id(0)
    x = jnp.where(i < n_prompt_tiles, xp_ref[...], xs_ref[...])
    h = _rms(x, g_ref[...]).astype(BF16)
    aw = 3 * GROUP_W
    a = _dot(h, w_ref[:, 0:aw])
    gbu_ref[:, 0:GROUP_W] = a[:, 0:GROUP_W]
    gbu_ref[:, GROUP_W:2 * GROUP_W] = a[:, GROUP_W:2 * GROUP_W] * a[:, 2 * GROUP_W:aw]
    qw = N_GROUPS * GROUP_W
    _store_chunks(q_ref, _dot(h, w_ref[:, aw:aw + qw]) * (HEAD_DIM ** -0.5))
    _store_chunks(kv_ref, _dot(h, w_ref[:, aw + qw:]))


def _store_chunks(ref, val):
    for c in range(val.shape[1] // LANES):
        ref[c] = val[:, c * LANES:(c + 1) * LANES]


def _load_chunks(ref, rows=None):
    n = ref.shape[0]
    return jnp.concatenate([ref[c] if rows is None else ref[c, rows, :] for c in range(n)], axis=-1)


def _inproj(xp, xs, g, w):
    mp, d = xp.shape
    ms = xs.shape[0]
    tm = ROW_TILE
    npt, nst = mp // tm, ms // tm
    m = mp + ms
    ncols = w.shape[1]
    qw = N_GROUPS * GROUP_W
    return pl.pallas_call(
        functools.partial(_inproj_kernel, npt),
        grid=(npt + nst,),
        in_specs=[
            pl.BlockSpec((tm, d), lambda i: (jnp.minimum(i, npt - 1), 0)),
            pl.BlockSpec((tm, d), lambda i: (jnp.maximum(i - npt, 0), 0)),
            pl.BlockSpec((1, d), lambda i: (0, 0)),
            pl.BlockSpec((d, ncols), lambda i: (0, 0)),
        ],
        out_specs=[
            pl.BlockSpec((qw // LANES, tm, LANES), lambda i: (0, i, 0)),
            pl.BlockSpec((2 * qw // LANES, tm, LANES), lambda i: (0, i, 0)),
            pl.BlockSpec((tm, 2 * GROUP_W), lambda i: (i, 0)),
        ],
        out_shape=[
            jax.ShapeDtypeStruct((qw // LANES, m, LANES), F32),
            jax.ShapeDtypeStruct((2 * qw // LANES, m, LANES), F32),
            jax.ShapeDtypeStruct((m, 2 * GROUP_W), F32),
        ],
        compiler_params=_params("arbitrary"),
        name="inproj",
    )(xp, xs, g, w)


def _attn_prompt_kernel(q_ref, kv_ref, o_ref, m_scr, l_scr, acc_scr):
    g = pl.program_id(1)
    seq = q_ref.shape[1]

    @pl.when(g == 0)
    def _():
        m_scr[...] = jnp.full(m_scr.shape, NEG_INF, F32)
        l_scr[...] = jnp.zeros(l_scr.shape, F32)
        acc_scr[...] = jnp.zeros(acc_scr.shape, F32)

    qi = lax.broadcasted_iota(jnp.int32, (NK, 2 * NK), 0)
    kc = lax.broadcasted_iota(jnp.int32, (NK, 2 * NK), 1)
    band = (kc >= qi) & (kc <= qi + NK)
    lane = lax.broadcasted_iota(jnp.int32, (NK, LANES), 1)
    lane_kv = lax.broadcasted_iota(jnp.int32, (2 * NK, LANES), 1)
    ones_kv = jnp.ones((2 * NK, LANES), BF16)
    n_chunks = GROUP_W // LANES
    hpc = LANES // HEAD_DIM

    def block(blk, d, last):
        span = NK * d
        base = pl.multiple_of(blk * span, span)
        pbase = pl.multiple_of(jnp.maximum(blk - 1, 0) * span, span)
        mask = band & ((blk > 0) | (kc >= NK))
        for r in range(d):
            rows = pl.ds(base + r, NK, stride=d) if d > 1 else pl.ds(base, NK)
            prows = pl.ds(pbase + r, NK, stride=d) if d > 1 else pl.ds(pbase, NK)
            m_old = m_scr[rows, :]
            l_old = l_scr[rows, :]
            m_new, l_new = m_old, l_old
            for c in range(GROUP_W // LANES):
                qc = q_ref[c, rows, :].astype(BF16)
                kc_ = jnp.concatenate([kv_ref[c, prows, :], kv_ref[c, rows, :]], axis=0).astype(BF16)
                vc_ = jnp.concatenate([kv_ref[n_chunks + c, prows, :], kv_ref[n_chunks + c, rows, :]],
                                      axis=0).astype(BF16)
                acc = acc_scr[c, rows, :]
                scale, denom, contrib = None, None, None
                for hh in range(hpc):
                    h = c * hpc + hh
                    sel_q = (lane >= hh * HEAD_DIM) & (lane < (hh + 1) * HEAD_DIM)
                    sel_v = (lane_kv >= hh * HEAD_DIM) & (lane_kv < (hh + 1) * HEAD_DIM)
                    s = lax.dot_general(jnp.where(sel_q, qc, 0), kc_, (((1,), (1,)), ((), ())),
                                        preferred_element_type=F32)
                    s = jnp.where(mask, s, NEG_INF)
                    mo = m_old[:, h:h + 1]
                    mn = jnp.maximum(mo, jnp.max(s, axis=-1, keepdims=True))
                    p = jnp.exp(s - mn)
                    alpha = jnp.exp(mo - mn)
                    pv = _dot(p.astype(BF16), jnp.concatenate([jnp.where(sel_v, vc_, 0), ones_kv], axis=1))
                    ln = alpha * l_old[:, h:h + 1] + pv[:, LANES:]
                    contrib = pv[:, :LANES] if contrib is None else contrib + pv[:, :LANES]
                    scale = alpha if scale is None else jnp.where(sel_q, alpha, scale)
                    denom = ln if denom is None else jnp.where(sel_q, ln, denom)
                    m_new = jnp.where(lane == h, mn, m_new)
                    l_new = jnp.where(lane == h, ln, l_new)
                acc = scale * acc + contrib
                if last:
                    o_ref[c, rows, :] = acc / denom
                else:
                    acc_scr[c, rows, :] = acc
            if not last:
                m_scr[rows, :] = m_new
                l_scr[rows, :] = l_new

    for gi, (window, d) in enumerate(DIL_GROUPS):
        @pl.when(g == gi)
        def _(d=d, last=(gi == N_GROUPS - 1)):
            def body(blk, carry):
                block(blk, d, last)
                return carry
            lax.fori_loop(0, seq // (NK * d), body, 0)


def _attn_prompt(q, kv, batch, seq):
    assert seq % (NK * max(d for _, d in DIL_GROUPS)) == 0
    return pl.pallas_call(
        _attn_prompt_kernel,
        grid=(batch, N_GROUPS),
        in_specs=[
            pl.BlockSpec((GROUP_W // LANES, seq, LANES), lambda b, g: (g, b, 0)),
            pl.BlockSpec((2 * GROUP_W // LANES, seq, LANES), lambda b, g: (g, b, 0)),
        ],
        out_specs=pl.BlockSpec((GROUP_W // LANES, seq, LANES), lambda b, g: (0, b, 0)),
        out_shape=jax.ShapeDtypeStruct((GROUP_W // LANES, batch * seq, LANES), F32),
        scratch_shapes=[
            pltpu.VMEM((seq, LANES), F32),
            pltpu.VMEM((seq, LANES), F32),
            pltpu.VMEM((GROUP_W // LANES, seq, LANES), F32),
        ],
        compiler_params=_params("arbitrary", "arbitrary", vmem_limit=ATTN_VMEM_LIMIT),
        name="attn_prompt",
    )(q, kv)


def _shortconv_prompt_kernel(tiles_per_seq, gbu_ref, halo_ref, w_ref, o_ref, scr):
    i = pl.program_id(0)
    tm = gbu_ref.shape[0]
    gb = gbu_ref[:, 0:GROUP_W]
    u = gbu_ref[:, GROUP_W:]
    first = (i % tiles_per_seq) == 0
    scr[0:SUBLANES, :] = jnp.where(first, 0.0, halo_ref[:, GROUP_W:])
    scr[SUBLANES:, :] = u
    w = w_ref[...]
    conv = (w[0:1, :] * scr[pl.ds(SUBLANES - 2, tm), :] + w[1:2, :] * scr[pl.ds(SUBLANES - 1, tm), :]
            + w[2:3, :] * u)
    o_ref[...] = gb * conv


def _shortconv_prompt(gbu, w, mp, seq):
    tm = ROW_TILE
    return pl.pallas_call(
        functools.partial(_shortconv_prompt_kernel, seq // tm),
        grid=(mp // tm,),
        in_specs=[
            pl.BlockSpec((tm, 2 * GROUP_W), lambda i: (i, 0)),
            pl.BlockSpec((SUBLANES, 2 * GROUP_W), lambda i: (jnp.maximum(i * (tm // SUBLANES) - 1, 0), 0)),
            pl.BlockSpec((A_KERNEL, GROUP_W), lambda i: (0, 0)),
        ],
        out_specs=pl.BlockSpec((tm, GROUP_W), lambda i: (i, 0)),
        out_shape=jax.ShapeDtypeStruct((mp, GROUP_W), F32),
        scratch_shapes=[pltpu.VMEM((tm + SUBLANES, GROUP_W), F32)],
        compiler_params=_params("arbitrary"),
        name="shortconv_prompt",
    )(gbu, gbu, w)


def _mix_sample_kernel(q_ref, kn_ref, vn_ref, gbu_ref, st_ref, c0_ref, c1_ref, c2_ref, w_ref,
                       yb_ref, ya_ref, ns_ref):
    sb, t_len = q_ref.shape[0], q_ref.shape[3]
    w = w_ref[...]
    caches = (c0_ref, c1_ref, c2_ref)
    nt_dims = (((1,), (1,)), ((), ()))
    qn = lax.broadcasted_iota(jnp.int32, (t_len, t_len), 0)
    tn = lax.broadcasted_iota(jnp.int32, (t_len, t_len), 1)

    for s in range(sb):
        for h in range(HEADS):
            sc_parts, v_parts = [], []
            for gi, (window, d) in enumerate(DIL_GROUPS):
                c_ref = caches[gi]
                n_pos = c_ref.shape[-1]
                qg = q_ref[s, gi, h].astype(BF16)
                qc = lax.broadcasted_iota(jnp.int32, (t_len, n_pos), 0)
                pc = lax.broadcasted_iota(jnp.int32, (t_len, n_pos), 1)
                sc = _dot(qg, c_ref[s, 0, h].astype(BF16))
                sc_parts.append(jnp.where((((pc - qc) & (d - 1)) == 0) & (pc >= qc), sc, NEG_INF))
                v_parts.append((c_ref[s, 1, h].astype(BF16), True))
                sn = lax.dot_general(qg, kn_ref[s, gi, h].astype(BF16), nt_dims, preferred_element_type=F32)
                sc_parts.append(jnp.where((tn <= qn) & (((qn - tn) & (d - 1)) == 0), sn, NEG_INF))
                v_parts.append((vn_ref[s, gi, h].astype(BF16), False))
            mx = functools.reduce(jnp.maximum, [jnp.max(p, axis=-1, keepdims=True) for p in sc_parts])
            den = jnp.zeros((t_len, 1), F32)
            acc = jnp.zeros((t_len, HEAD_DIM), F32)
            for sc, (v, transposed) in zip(sc_parts, v_parts):
                p = jnp.exp(sc - mx)
                den = den + jnp.sum(p, axis=-1, keepdims=True)
                pb = p.astype(BF16)
                acc = acc + (lax.dot_general(pb, v, nt_dims, preferred_element_type=F32) if transposed
                             else _dot(pb, v))
            yb_ref[s, h] = acc / den
        gbu = gbu_ref[s]
        gb = gbu[:, 0:GROUP_W]
        u = gbu[:, GROUP_W:]
        st = st_ref[s]
        ext = [st[0:1], st[1:2]] + [u[t:t + 1] for t in range(t_len)]
        for t in range(t_len):
            conv = w[0:1] * ext[t] + w[1:2] * ext[t + 1] + w[2:3] * ext[t + 2]
            ya_ref[s, t:t + 1, :] = gb[t:t + 1] * conv
        for r in range(A_KERNEL - 1):
            ns_ref[s, r:r + 1, :] = ext[t_len + r]


def _mix_sample(q5, kn, vn, gbu3, st, c0, c1, c2, w):
    n, t_len = q5.shape[0], q5.shape[3]
    sb = ATTN_SEQ_BLOCK
    assert t_len >= A_KERNEL - 1 and n % sb == 0
    for c, (window, d) in zip((c0, c1, c2), DIL_GROUPS):
        assert c.shape[-1] == NK * d and d & (d - 1) == 0
    qspec = pl.BlockSpec((sb, N_GROUPS, HEADS, t_len, HEAD_DIM), lambda i: (i, 0, 0, 0, 0))
    cspec = lambda c: pl.BlockSpec((sb,) + c.shape[1:], lambda i: (i, 0, 0, 0, 0))
    return pl.pallas_call(
        _mix_sample_kernel,
        grid=(n // sb,),
        in_specs=[
            qspec, qspec, qspec,
            pl.BlockSpec((sb, t_len, 2 * GROUP_W), lambda i: (i, 0, 0)),
            pl.BlockSpec((sb, A_KERNEL - 1, GROUP_W), lambda i: (i, 0, 0)),
            cspec(c0), cspec(c1), cspec(c2),
            pl.BlockSpec((A_KERNEL, GROUP_W), lambda i: (0, 0)),
        ],
        out_specs=[
            pl.BlockSpec((sb, HEADS, t_len, HEAD_DIM), lambda i: (i, 0, 0, 0)),
            pl.BlockSpec((sb, t_len, GROUP_W), lambda i: (i, 0, 0)),
            pl.BlockSpec((sb, A_KERNEL - 1, GROUP_W), lambda i: (i, 0, 0)),
        ],
        out_shape=[
            jax.ShapeDtypeStruct((n, HEADS, t_len, HEAD_DIM), F32),
            jax.ShapeDtypeStruct((n, t_len, GROUP_W), F32),
            jax.ShapeDtypeStruct((n, A_KERNEL - 1, GROUP_W), F32),
        ],
        compiler_params=_params("arbitrary"),
        name="mix_sample",
    )(q5, kn, vn, gbu3, st, c0, c1, c2, w)


def _route(h2, wr_ref, br_ref, xs_ref, rinfo_ref, gtab_ref):
    tm = h2.shape[0]
    h_hi = h2.astype(BF16)
    h_lo = (h2 - h_hi.astype(F32)).astype(BF16)
    nt_dims = (((1,), (1,)), ((), ()))
    w_hi, w_lo = wr_ref[0], wr_ref[1]
    lt = (lax.dot_general(w_hi, h_hi, nt_dims, preferred_element_type=F32)
          + lax.dot_general(w_hi, h_lo, nt_dims, preferred_element_type=F32)
          + lax.dot_general(w_lo, h_hi, nt_dims, preferred_element_type=F32)) + br_ref[...]
    rg = lax.broadcasted_iota(jnp.int32, (SUBLANES, tm), 0)
    gl = jnp.where(rg < N_EXPERT_GROUPS, lt[0:SUBLANES], NEG_INF)
    ge = jnp.exp(gl - jnp.max(gl, axis=0, keepdims=True))
    gp = ge / jnp.sum(ge, axis=0, keepdims=True)
    g_val = jnp.max(gp, axis=0, keepdims=True)
    g_idx = jnp.min(jnp.where(gp == g_val, rg, SUBLANES), axis=0, keepdims=True)
    re = lax.broadcasted_iota(jnp.int32, (N_EXPERTS, tm), 0)
    sel = (re // EXPERTS_PER_GROUP) == g_idx
    el = jnp.where(sel, lt[ROUTER_EXPERT_ROW0:ROUTER_EXPERT_ROW0 + N_EXPERTS], NEG_INF)
    ee = jnp.exp(el - jnp.max(el, axis=0, keepdims=True))
    ep = jnp.where(sel, ee / jnp.sum(ee, axis=0, keepdims=True), -1.0)
    v1 = jnp.max(ep, axis=0, keepdims=True)
    i1 = jnp.min(jnp.where(ep == v1, re, N_EXPERTS), axis=0, keepdims=True)
    ep2 = jnp.where(re == i1, -1.0, ep)
    v2 = jnp.max(ep2, axis=0, keepdims=True)
    i2 = jnp.min(jnp.where(ep2 == v2, re, N_EXPERTS), axis=0, keepdims=True)
    scale = g_val / (v1 + v2)

    oh = [(re == i1).astype(F32), (re == i2).astype(F32)]
    cmat = (oh[0] + oh[1]).astype(BF16)
    earlier = (lax.broadcasted_iota(jnp.int32, (tm, tm), 0)
               < lax.broadcasted_iota(jnp.int32, (tm, tm), 1)).astype(BF16)
    before = _dot(cmat, earlier)
    cnt_row = lax.dot_general(jnp.ones((SUBLANES, tm), BF16), cmat, nt_dims, preferred_element_type=F32)
    grp_row = ((cnt_row.astype(jnp.int32) + (SUBLANES - 1)) // SUBLANES).astype(F32)
    lower = (lax.broadcasted_iota(jnp.int32, (N_EXPERTS, N_EXPERTS), 0)
             < lax.broadcasted_iota(jnp.int32, (N_EXPERTS, N_EXPERTS), 1)).astype(BF16)
    goff_row = _dot(grp_row.astype(BF16), lower)
    slot_rows = lax.broadcasted_iota(jnp.int32, (TILE_SLOTS, tm), 0)
    place = None
    pos = []
    for k in range(2):
        rank = jnp.sum(oh[k] * before, axis=0, keepdims=True)
        seg = _dot(goff_row.astype(BF16), oh[k].astype(BF16))[0:1]
        pos.append(seg * SUBLANES + rank)
        hit = slot_rows == pos[k].astype(jnp.int32)
        place = hit if place is None else place | hit
    xs_ref[...] = _dot(jnp.where(place, 1.0, 0.0).astype(BF16), h_hi)
    rl = lax.broadcasted_iota(jnp.int32, (LANES, tm), 0)
    info_t = jnp.where(rl == 0, v1 * scale, jnp.where(rl == 1, v2 * scale,
                       jnp.where(rl == 2, pos[0], jnp.where(rl == 3, pos[1], 0.0))))
    rinfo_ref[...] = info_t.T
    spread = (lax.broadcasted_iota(jnp.int32, (N_EXPERTS, LANES), 0)
              == lax.broadcasted_iota(jnp.int32, (N_EXPERTS, LANES), 1)).astype(BF16)
    gtab_ref[...] = _dot(grp_row.astype(BF16), spread).astype(jnp.int32)


ROUTER_EXPERT_ROW0 = SUBLANES
ROUTER_ROWS = 48
TILE_GROUPS = -(-(2 * ROW_TILE + (SUBLANES - 1) * N_EXPERTS) // SUBLANES)
TILE_SLOTS = TILE_GROUPS * SUBLANES


def _router_specs(tm, d):
    ins = [pl.BlockSpec((1, d), lambda i: (0, 0)),
           pl.BlockSpec((2, ROUTER_ROWS, d), lambda i: (0, 0, 0)),
           pl.BlockSpec((ROUTER_ROWS, 1), lambda i: (0, 0))]
    outs = [pl.BlockSpec((TILE_SLOTS, d), lambda i: (i, 0)),
            pl.BlockSpec((tm, LANES), lambda i: (i, 0)),
            pl.BlockSpec((None, SUBLANES, LANES), lambda i: (i, 0, 0))]
    return ins, outs


def _router_shapes(m, d):
    nb = m // ROW_TILE
    return [jax.ShapeDtypeStruct((nb * TILE_SLOTS, d), F32), jax.ShapeDtypeStruct((m, LANES), F32),
            jax.ShapeDtypeStruct((nb, SUBLANES, LANES), jnp.int32)]


def _outproj_kernel(n_prompt_tiles, xp_ref, xs_ref, ya_ref, yb_ref, ys_ref, wo_ref, g_ref, wr_ref, br_ref,
                    x1_ref, xs_out_ref, rinfo_ref, gtab_ref):
    i = pl.program_id(0)
    is_p = i < n_prompt_tiles
    x = jnp.where(is_p, xp_ref[...], xs_ref[...])
    ymix = jnp.where(is_p, jnp.concatenate([ya_ref[...], _load_chunks(yb_ref)], axis=-1), ys_ref[...])
    x1 = x + _dot(ymix.astype(BF16), wo_ref[...])
    x1_ref[...] = x1
    _route(_rms(x1, g_ref[...]), wr_ref, br_ref, xs_out_ref, rinfo_ref, gtab_ref)


def _outproj(xp, xs, ya_p, yb_p, ymix_s, wo, g, wr, br):
    mp, d = xp.shape
    ms = xs.shape[0]
    tm = ROW_TILE
    npt, nst = mp // tm, ms // tm
    m = mp + ms
    r_in, r_out = _router_specs(tm, d)
    pmap = lambda i: (jnp.minimum(i, npt - 1), 0)
    smap = lambda i: (jnp.maximum(i - npt, 0), 0)
    return pl.pallas_call(
        functools.partial(_outproj_kernel, npt),
        grid=(npt + nst,),
        in_specs=[
            pl.BlockSpec((tm, d), pmap),
            pl.BlockSpec((tm, d), smap),
            pl.BlockSpec((tm, GROUP_W), pmap),
            pl.BlockSpec((GROUP_W // LANES, tm, LANES), lambda i: (0, jnp.minimum(i, npt - 1), 0)),
            pl.BlockSpec((tm, 2 * GROUP_W), smap),
            pl.BlockSpec((2 * GROUP_W, d), lambda i: (0, 0)),
        ] + r_in,
        out_specs=[pl.BlockSpec((tm, d), lambda i: (i, 0))] + r_out,
        out_shape=[jax.ShapeDtypeStruct((m, d), F32)] + _router_shapes(m, d),
        compiler_params=_params("arbitrary"),
        name="outproj_router",
    )(xp, xs, ya_p, yb_p, ymix_s, wo, g, wr, br)


def _experts_kernel(te_ref, src_ref, dst_ref, nt_ref, h_hbm, wg_ref, wu_ref, wd_ref, y_hbm, xbuf, ybuf, gsem, ssem,
                    *, trash_row):
    i = pl.program_id(0)
    nt = nt_ref[0]
    ng = CHUNK_GROUPS
    slot = i % 2
    other = 1 - slot

    def group_rows(r):
        return pl.ds(r if isinstance(r, int) else pl.multiple_of(r, SUBLANES), SUBLANES)

    def gather_copy(p, sl, j):
        return pltpu.make_async_copy(h_hbm.at[group_rows(src_ref[p]), :],
                                     xbuf.at[sl, group_rows(j * SUBLANES), :], gsem.at[sl])

    def scatter_copy(p, sl, j):
        return pltpu.make_async_copy(ybuf.at[sl, group_rows(j * SUBLANES), :],
                                     y_hbm.at[group_rows(dst_ref[p]), :], ssem.at[sl])

    def wait_gather(sl):
        pltpu.make_async_copy(h_hbm.at[pl.ds(0, ng * SUBLANES), :], xbuf.at[sl], gsem.at[sl]).wait()

    def wait_scatter(sl):
        pltpu.make_async_copy(ybuf.at[sl], y_hbm.at[pl.ds(0, ng * SUBLANES), :], ssem.at[sl]).wait()

    @pl.when(i == 0)
    def _():
        ybuf[...] = jnp.zeros(ybuf.shape, F32)
        init = pltpu.make_async_copy(ybuf.at[0], y_hbm.at[pl.ds(trash_row, ng * SUBLANES), :], ssem.at[0])
        init.start()
        init.wait()

        def tail_copy(b):
            first = pl.multiple_of(b * TILE_SLOTS + 2 * ROW_TILE, SUBLANES)
            return pltpu.make_async_copy(ybuf.at[0, pl.ds(0, TILE_SLOTS - 2 * ROW_TILE), :],
                                         y_hbm.at[pl.ds(first, TILE_SLOTS - 2 * ROW_TILE), :], ssem.at[0])

        def start_body(b, c):
            tail_copy(b).start()
            return c

        def wait_body(b, c):
            tail_copy(b).wait()
            return c
        lax.fori_loop(0, trash_row // TILE_SLOTS, start_body, 0)
        lax.fori_loop(0, trash_row // TILE_SLOTS, wait_body, 0)
        for j in range(ng):
            gather_copy(j, 0, j).start()

    @pl.when(i < nt)
    def _():
        wait_gather(slot)

        @pl.when(i >= 1)
        def _():
            wait_scatter(slot)

        base = i * ng
        for j in range(ng):
            gather_copy(base + ng + j, other, j).start(priority=j % 2)
            scatter_copy(base + j, other, j).start(priority=(j + 1) % 2)
        x = xbuf[slot].astype(BF16)
        gate = _dot(x, wg_ref[...].astype(BF16))
        up = _dot(x, wu_ref[...].astype(BF16))
        act = (gate * jax.nn.sigmoid(gate) * up).astype(BF16)
        ybuf[slot] = _dot(act, wd_ref[...].astype(BF16))

    @pl.when(i == nt)
    def _():
        wait_gather(slot)
        wait_scatter(slot)
        for j in range(ng):
            scatter_copy(i * ng + j, other, j).start()
        wait_scatter(other)


def _experts(xs, tile_expert, src, dst, n_tiles, wg, wu, wd, layer):
    ng = CHUNK_GROUPS
    nt_max = tile_expert.shape[0]
    d, f = wg.shape[2], wg.shape[3]
    n_rows = xs.shape[0] + ng * SUBLANES
    wmap = lambda i, te_ref, src_ref, dst_ref, nt_ref: (layer, te_ref[i], 0, 0)
    grid_spec = pltpu.PrefetchScalarGridSpec(
        num_scalar_prefetch=4,
        grid=(nt_max,),
        in_specs=[
            pl.BlockSpec(memory_space=pl.ANY),
            pl.BlockSpec((None, None, d, f), wmap),
            pl.BlockSpec((None, None, d, f), wmap),
            pl.BlockSpec((None, None, f, d), wmap),
        ],
        out_specs=pl.BlockSpec(memory_space=pl.ANY),
        scratch_shapes=[
            pltpu.VMEM((2, ng * SUBLANES, d), F32),
            pltpu.VMEM((2, ng * SUBLANES, d), F32),
            pltpu.SemaphoreType.DMA((2,)),
            pltpu.SemaphoreType.DMA((2,)),
        ],
    )
    return pl.pallas_call(
        functools.partial(_experts_kernel, trash_row=xs.shape[0]),
        grid_spec=grid_spec,
        out_shape=jax.ShapeDtypeStruct((n_rows, d), F32),
        compiler_params=_params("arbitrary"),
        name="experts",
    )(tile_expert, src, dst, n_tiles, xs, wg, wu, wd)


def _plan_chunks(gtab):
    ng = CHUNK_GROUPS
    grp = gtab[:, 0, :N_EXPERTS]
    nb = grp.shape[0]
    nt_max = -(-nb * TILE_GROUPS // ng) + N_EXPERTS + 1
    total = jnp.sum(grp, axis=0)
    chunks = (total + ng - 1) // ng
    chunk_end = jnp.cumsum(chunks)
    chunk_start = chunk_end - chunks
    n_chunks = chunk_end[-1]
    seg_off = jnp.cumsum(grp, axis=1) - grp
    seg_end = jnp.cumsum(grp, axis=0)
    seg_start = seg_end - grp
    cidx = jnp.arange(nt_max, dtype=jnp.int32)
    chunk_expert = jnp.sum((cidx[:, None] >= chunk_end[None, :]).astype(jnp.int32), axis=1)
    last_expert = jnp.max(jnp.where(total > 0, jnp.arange(N_EXPERTS), 0))
    chunk_expert = jnp.where(cidx < n_chunks, jnp.minimum(chunk_expert, N_EXPERTS - 1), last_expert).astype(jnp.int32)
    is_e = chunk_expert[:, None] == jnp.arange(N_EXPERTS, dtype=jnp.int32)[None, :]
    pick = lambda per_expert: jnp.sum(jnp.where(is_e, per_expert[None, :], 0), axis=1)
    pick_rows = lambda table: jnp.sum(jnp.where(is_e[:, :, None], table.T[None, :, :], 0), axis=1)
    within = (cidx - pick(chunk_start))[:, None] * ng + jnp.arange(ng, dtype=jnp.int32)[None, :]
    valid = (within < pick(total)[:, None]) & (cidx < n_chunks)[:, None]
    ends = pick_rows(seg_end)
    base = pick_rows(jnp.arange(nb, dtype=jnp.int32)[:, None] * TILE_GROUPS + seg_off - seg_start)
    step = jnp.concatenate([base[:, :1], base[:, 1:] - base[:, :-1]], axis=1)
    passed = jnp.concatenate([jnp.ones((nt_max, ng, 1), bool), within[:, :, None] >= ends[:, None, :-1]], axis=2)
    group = within + jnp.sum(jnp.where(passed, step[:, None, :], 0), axis=2)
    pad_dst = (nb * TILE_GROUPS + jnp.arange(ng, dtype=jnp.int32)) * SUBLANES
    src = jnp.where(valid, group * SUBLANES, 0).astype(jnp.int32).reshape(nt_max * ng)
    dst = jnp.where(valid, group * SUBLANES, pad_dst[None, :]).astype(jnp.int32).reshape(nt_max * ng)
    dst = jnp.concatenate([pad_dst, dst])
    return chunk_expert, src, dst, n_chunks.astype(jnp.int32).reshape(1)


def _combine(x_ref, y_ref, rinfo_ref):
    tm = x_ref.shape[0]
    n_slots = y_ref.shape[0]
    r = rinfo_ref[...]
    y = y_ref[...].astype(BF16)
    col = lax.broadcasted_iota(jnp.int32, (tm, n_slots), 1)
    out = x_ref[...]
    for k in range(2):
        pick = jnp.where(col == r[:, 2 + k:3 + k].astype(jnp.int32), 1.0, 0.0).astype(BF16)
        out = out + r[:, k:k + 1] * _dot(pick, y)
    return out


def _combine_specs(tm, d):
    return [
        pl.BlockSpec((tm, d), lambda i: (i, 0)),
        pl.BlockSpec((TILE_SLOTS, d), lambda i: (i, 0)),
        pl.BlockSpec((tm, LANES), lambda i: (i, 0)),
    ]


def _combine_pw1_kernel(x_ref, y_ref, rinfo_ref, g_ref, w_ref, b_ref, x2_ref, u_ref):
    x2 = _combine(x_ref, y_ref, rinfo_ref)
    x2_ref[...] = x2
    d = x2.shape[1]
    z = _dot(_rms(x2, g_ref[...]).astype(BF16), w_ref[...]) + b_ref[...]
    u_ref[...] = z[:, :d] * jax.nn.sigmoid(z[:, d:])


def _combine_pw1(x1, y2, rinfo, g, w, b):
    m, d = x1.shape
    tm = ROW_TILE
    return pl.pallas_call(
        _combine_pw1_kernel,
        grid=(m // tm,),
        in_specs=_combine_specs(tm, d) + [
            pl.BlockSpec((1, d), lambda i: (0, 0)),
            pl.BlockSpec((d, 2 * d), lambda i: (0, 0)),
            pl.BlockSpec((1, 2 * d), lambda i: (0, 0)),
        ],
        out_specs=[pl.BlockSpec((tm, d), lambda i: (i, 0)), pl.BlockSpec((tm, d), lambda i: (i, 0))],
        out_shape=[jax.ShapeDtypeStruct((m, d), F32), jax.ShapeDtypeStruct((m, d), F32)],
        compiler_params=_params("arbitrary"),
        name="combine_pw1",
    )(x1, y2, rinfo, g, w, b)


def _combine_final_kernel(n_prompt_tiles, x_ref, y_ref, rinfo_ref, g_ref, yp_ref, ys_ref):
    i = pl.program_id(0)
    y = _rms(_combine(x_ref, y_ref, rinfo_ref), g_ref[...])

    @pl.when(i < n_prompt_tiles)
    def _():
        yp_ref[...] = y

    @pl.when(i >= n_prompt_tiles)
    def _():
        ys_ref[...] = y


def _combine_final(x3, y2, rinfo, g, mp):
    m, d = x3.shape
    tm = ROW_TILE
    npt = mp // tm
    return pl.pallas_call(
        functools.partial(_combine_final_kernel, npt),
        grid=(m // tm,),
        in_specs=_combine_specs(tm, d) + [pl.BlockSpec((1, d), lambda i: (0, 0))],
        out_specs=[pl.BlockSpec((tm, d), lambda i: (jnp.minimum(i, npt - 1), 0)),
                   pl.BlockSpec((tm, d), lambda i: (jnp.maximum(i - npt, 0), 0))],
        out_shape=[jax.ShapeDtypeStruct((mp, d), F32), jax.ShapeDtypeStruct((m - mp, d), F32)],
        compiler_params=_params("arbitrary"),
        name="combine_final",
    )(x3, y2, rinfo, g)


def _dwconv_prompt_kernel(tiles_per_seq, u_ref, halo_ref, w_ref, b_ref, c_ref, scr):
    i = pl.program_id(0)
    tm = u_ref.shape[0]
    first = (i % tiles_per_seq) == 0
    scr[0, 0:CONV_HALO, :] = jnp.where(first, 0.0, halo_ref[...])
    scr[0, CONV_HALO:, :] = u_ref[...]
    n_shift = tm + CONV_HALO - SUBLANES
    for s in range(1, SUBLANES):
        scr[s, 0:n_shift, :] = scr[0, pl.ds(s, n_shift), :]
    acc = jnp.broadcast_to(b_ref[...], c_ref.shape)
    for k in range(C_KERNEL):
        off = CONV_HALO - (C_KERNEL - 1) + k
        acc = acc + w_ref[k:k + 1, :] * scr[off % SUBLANES, pl.ds(off - off % SUBLANES, tm), :]
    c_ref[...] = acc


def _dwconv_prompt(u, w, b, mp, seq):
    tm = ROW_TILE
    d = u.shape[1]
    return pl.pallas_call(
        functools.partial(_dwconv_prompt_kernel, seq // tm),
        grid=(mp // tm,),
        in_specs=[
            pl.BlockSpec((tm, d), lambda i: (i, 0)),
            pl.BlockSpec((CONV_HALO, d), lambda i: (jnp.maximum(i * (tm // CONV_HALO) - 1, 0), 0)),
            pl.BlockSpec((C_KERNEL, d), lambda i: (0, 0)),
            pl.BlockSpec((1, d), lambda i: (0, 0)),
        ],
        out_specs=pl.BlockSpec((tm, d), lambda i: (i, 0)),
        out_shape=jax.ShapeDtypeStruct((mp, d), F32),
        scratch_shapes=[pltpu.VMEM((SUBLANES, tm + CONV_HALO, d), F32)],
        compiler_params=_params("arbitrary"),
        name="dwconv_prompt",
    )(u, u, w, b)


def _dwconv_sample_kernel(st_ref, u_ref, w_ref, b_ref, c_ref, ns_ref):
    n_state, t_len = st_ref.shape[0], u_ref.shape[0]
    rows = [st_ref[r] for r in range(n_state)] + [u_ref[t] for t in range(t_len)]
    for t in range(t_len):
        acc = jnp.broadcast_to(b_ref[...], rows[0].shape)
        for k in range(C_KERNEL):
            acc = acc + w_ref[k:k + 1, :] * rows[t + k]
        c_ref[t] = acc
    for r in range(n_state):
        ns_ref[r] = rows[r + t_len]


def _dwconv_sample(st, u3, w, b):
    n_state, n, d = st.shape
    t_len = u3.shape[0]
    sb = SEQ_BLOCK
    assert n_state == C_KERNEL - 1 and n % sb == 0
    return pl.pallas_call(
        _dwconv_sample_kernel,
        grid=(n // sb,),
        in_specs=[
            pl.BlockSpec((n_state, sb, d), lambda i: (0, i, 0)),
            pl.BlockSpec((t_len, sb, d), lambda i: (0, i, 0)),
            pl.BlockSpec((C_KERNEL, d), lambda i: (0, 0)),
            pl.BlockSpec((1, d), lambda i: (0, 0)),
        ],
        out_specs=[
            pl.BlockSpec((t_len, sb, d), lambda i: (0, i, 0)),
            pl.BlockSpec((n_state, sb, d), lambda i: (0, i, 0)),
        ],
        out_shape=[jax.ShapeDtypeStruct((t_len, n, d), F32), jax.ShapeDtypeStruct((n_state, n, d), F32)],
        compiler_params=_params("arbitrary"),
        name="dwconv_sample",
    )(st, u3, w, b)


def _conf_tail_kernel(n_prompt_tiles, x_ref, cp_ref, cs_ref, lg_ref, lb_ref, w_ref, b_ref, g_ref, wr_ref, br_ref,
                      x3_ref, xs_out_ref, rinfo_ref, gtab_ref):
    i = pl.program_id(0)
    c = jnp.where(i < n_prompt_tiles, cp_ref[...], cs_ref[...])
    mu = jnp.mean(c, axis=-1, keepdims=True)
    cc = c - mu
    var = jnp.mean(cc * cc, axis=-1, keepdims=True)
    y = cc * lax.rsqrt(var + LN_EPS) * lg_ref[...] + lb_ref[...]
    y = y * jax.nn.sigmoid(y)
    x3 = x_ref[...] + _dot(y.astype(BF16), w_ref[...]) + b_ref[...]
    x3_ref[...] = x3
    _route(_rms(x3, g_ref[...]), wr_ref, br_ref, xs_out_ref, rinfo_ref, gtab_ref)


def _conf_tail(x2, c_p, c_s, ln_g, ln_b, w, b, g, wr, br):
    m, d = x2.shape
    tm = ROW_TILE
    npt = c_p.shape[0] // tm
    r_in, r_out = _router_specs(tm, d)
    vec = pl.BlockSpec((1, d), lambda i: (0, 0))
    return pl.pallas_call(
        functools.partial(_conf_tail_kernel, npt),
        grid=(m // tm,),
        in_specs=[
            pl.BlockSpec((tm, d), lambda i: (i, 0)),
            pl.BlockSpec((tm, d), lambda i: (jnp.minimum(i, npt - 1), 0)),
            pl.BlockSpec((tm, d), lambda i: (jnp.maximum(i - npt, 0), 0)),
            vec, vec,
            pl.BlockSpec((d, d), lambda i: (0, 0)),
            vec,
        ] + r_in,
        out_specs=[pl.BlockSpec((tm, d), lambda i: (i, 0))] + r_out,
        out_shape=[jax.ShapeDtypeStruct((m, d), F32)] + _router_shapes(m, d),
        compiler_params=_params("arbitrary"),
        name="conf_tail_router",
    )(x2, c_p, c_s, ln_g, ln_b, w, b, g, wr, br)


def _router_weights(w_rg, b_rg, w_re, b_re):
    d = w_rg.shape[0]
    gpad = ROUTER_EXPERT_ROW0 - N_EXPERT_GROUPS
    epad = ROUTER_ROWS - ROUTER_EXPERT_ROW0 - N_EXPERTS
    wt = jnp.concatenate([w_rg.T, jnp.zeros((gpad, d), F32), w_re.T, jnp.zeros((epad, d), F32)], axis=0)
    w_hi = wt.astype(BF16)
    w_lo = (wt - w_hi.astype(F32)).astype(BF16)
    br = jnp.concatenate([b_rg, jnp.zeros((gpad,), F32), b_re, jnp.zeros((epad,), F32)])[:, None]
    return jnp.stack([w_hi, w_lo]), br


def _moe(xs, gtab, wg, wu, wd, layer):
    chunk_expert, src, dst, n_chunks = _plan_chunks(gtab)
    return _experts(xs, chunk_expert, src, dst, n_chunks, wg, wu, wd, layer)


def kernel(x_prompt, x_sample, state_shortconv, cache_kv_w128, cache_kv_w512, cache_kv_w2048, state_conformer,
           g_mix, g_ffn, g_final, w_in, conv_a_w, w_out, w_pw1, b_pw1, dw_w, dw_b, ln_g, ln_b, w_pw2, b_pw2,
           w_router_group, b_router_group, w_router_expert, b_router_expert, w_gate, w_up, w_down):
    batch, seq, d = x_prompt.shape
    n_dec, t_dec, _ = x_sample.shape
    mp, ms = batch * seq, n_dec * t_dec
    assert g_mix.shape[0] == 2 and mp % ROW_TILE == 0 and ms % ROW_TILE == 0 and seq % ROW_TILE == 0
    xp = x_prompt.reshape(mp, d)
    xs = x_sample.reshape(ms, d)
    slab = (HEADS, HEAD_DIM)

    aw, qw = 3 * GROUP_W, N_GROUPS * GROUP_W
    w0 = w_in[0]
    kcols = w0[:, aw + qw:aw + 2 * qw].reshape(d, N_GROUPS, GROUP_W)
    vcols = w0[:, aw + 2 * qw:].reshape(d, N_GROUPS, GROUP_W)
    w_perm = jnp.concatenate([w0[:, :aw + qw], jnp.stack([kcols, vcols], axis=2).reshape(d, 2 * qw)], axis=1)
    q, kv, gbu = _inproj(xp, xs, g_mix[0][None, :], w_perm.astype(BF16))

    yb_p = _attn_prompt(q, kv, batch, seq)
    ya_p = _shortconv_prompt(gbu, conv_a_w[0], mp, seq)

    hpc = LANES // HEAD_DIM
    q_s = jnp.transpose(q[:, mp:].reshape(N_GROUPS, HEADS // hpc, n_dec, t_dec, hpc, HEAD_DIM),
                        (2, 0, 1, 4, 3, 5)).reshape(n_dec, N_GROUPS, HEADS, t_dec, HEAD_DIM)
    kv_s = jnp.transpose(kv[:, mp:].reshape(N_GROUPS, 2, HEADS // hpc, n_dec, t_dec, hpc, HEAD_DIM),
                         (1, 3, 0, 2, 5, 4, 6)).reshape(2, n_dec, N_GROUPS, HEADS, t_dec, HEAD_DIM)
    gbu_s = gbu[mp:].reshape(n_dec, t_dec, 2 * GROUP_W)
    c0, c1, c2 = (jnp.transpose(c[0], (0, 2, 3, 4, 1)) for c in (cache_kv_w128, cache_kv_w512, cache_kv_w2048))
    yb_s, ya_s, s_sc = _mix_sample(q_s, kv_s[0], kv_s[1], gbu_s, state_shortconv[0], c0, c1, c2, conv_a_w[0])
    ymix_s = jnp.concatenate([ya_s.reshape(ms, GROUP_W),
                              jnp.transpose(yb_s, (0, 2, 1, 3)).reshape(ms, GROUP_W)], axis=-1)

    wr0, br0 = _router_weights(w_router_group[0], b_router_group[0], w_router_expert[0], b_router_expert[0])
    x1, xsort0, rinfo0, gtab0 = _outproj(xp, xs, ya_p, yb_p, ymix_s, w_out[0].astype(BF16), g_ffn[0][None, :],
                                         wr0, br0)
    y2 = _moe(xsort0, gtab0, w_gate, w_up, w_down, 0)

    x2, u = _combine_pw1(x1, y2, rinfo0, g_mix[1][None, :], w_pw1[0].astype(BF16), b_pw1[0][None, :])
    c_p = _dwconv_prompt(u, dw_w[0], dw_b[0][None, :], mp, seq)
    c_s3, s_cf = _dwconv_sample(jnp.transpose(state_conformer[0], (1, 0, 2)),
                                jnp.transpose(u[mp:].reshape(n_dec, t_dec, d), (1, 0, 2)), dw_w[0], dw_b[0][None, :])
    c_s = jnp.transpose(c_s3, (1, 0, 2)).reshape(ms, d)
    wr1, br1 = _router_weights(w_router_group[1], b_router_group[1], w_router_expert[1], b_router_expert[1])
    x3, xsort1, rinfo1, gtab1 = _conf_tail(x2, c_p, c_s, ln_g[0][None, :], ln_b[0][None, :],
                                           w_pw2[0].astype(BF16), b_pw2[0][None, :], g_ffn[1][None, :], wr1, br1)
    y2b = _moe(xsort1, gtab1, w_gate, w_up, w_down, 1)
    y_p, y_s = _combine_final(x3, y2b, rinfo1, g_final[None, :], mp)

    cpg = 2 * GROUP_W // LANES
    kv_p = kv[:, :mp].reshape(N_GROUPS, cpg, batch, seq, LANES)
    p_kv = [jnp.transpose(kv_p[gi, :, :, seq - min(w, seq):], (1, 2, 0, 3)).reshape(1, batch, min(w, seq), 2, *slab)
            for gi, (w, _) in enumerate(DIL_GROUPS)]
    s_kv = [jnp.transpose(kv_s[:, :, gi], (1, 3, 0, 2, 4))[None] for gi in range(N_GROUPS)]
    u_p = gbu[:mp, GROUP_W:].reshape(batch, seq, GROUP_W)
    p_sc = u_p[:, seq - (A_KERNEL - 1):][None]
    p_cf = u[:mp].reshape(batch, seq, d)[:, seq - (C_KERNEL - 1):][None]
    return (y_p.reshape(batch, seq, d), y_s.reshape(n_dec, t_dec, d), p_sc, p_kv[0], p_kv[1], p_kv[2], p_cf,
            s_sc[None], s_kv[0], s_kv[1], s_kv[2], jnp.transpose(s_cf, (1, 0, 2))[None])
```

```python
import functools

import jax
import jax.numpy as jnp
from jax import lax
from jax.experimental import pallas as pl
from jax.experimental.pallas import tpu as pltpu

F32 = jnp.float32
BF16 = jnp.bfloat16

DIL_GROUPS = ((128, 1), (512, 4), (2048, 16))
N_GROUPS = len(DIL_GROUPS)
HEADS = 4
HEAD_DIM = 64
GROUP_W = HEADS * HEAD_DIM
NK = DIL_GROUPS[0][0] // DIL_GROUPS[0][1]
A_KERNEL = 3
C_KERNEL = 31
N_EXPERT_GROUPS = 4
EXPERTS_PER_GROUP = 8
N_EXPERTS = N_EXPERT_GROUPS * EXPERTS_PER_GROUP
RMS_EPS = 1e-6
LN_EPS = 1e-5
NEG_INF = -1e30

SUBLANES = 8
LANES = 128
VMEM_LIMIT = 48 * 1024 * 1024
ATTN_VMEM_LIMIT = 60 * 1024 * 1024

ROW_TILE = 256
CHUNK_GROUPS = 64
CONV_HALO = 32
SEQ_BLOCK = 8
ATTN_SEQ_BLOCK = 2


def _params(*sem, vmem_limit=VMEM_LIMIT):
    return pltpu.CompilerParams(dimension_semantics=sem, vmem_limit_bytes=vmem_limit)


def _rms(x, g):
    return x * lax.rsqrt(jnp.mean(x * x, axis=-1, keepdims=True) + RMS_EPS) * g


def _dot(a, b):
    return jnp.dot(a, b, preferred_element_type=F32)


def _rows_to_lanes(ref, n_rows, base=None):
    parts = []
    for s in range(SUBLANES):
        idx = pl.ds(s, n_rows, stride=SUBLANES)
        parts.append(ref[idx, :] if base is None else ref[base, idx, :])
    return jnp.concatenate(parts, axis=-1)


def _lanes_to_rows(ref, val, base=None):
    n_rows = val.shape[0]
    for s in range(SUBLANES):
        idx = pl.ds(s, n_rows, stride=SUBLANES)
        piece = val[:, s * LANES:(s + 1) * LANES]
        if base is None:
            ref[idx, :] = piece
        else:
            ref[base, idx, :] = piece


def _inproj_kernel(n_prompt_tiles, xp_ref, xs_ref, g_ref, w_ref, q_ref, kv_ref, gbu_ref):
    i = pl.program_id(0)
    x = jnp.where(i < n_prompt_tiles, xp_ref[...], xs_ref[...])
    h = _rms(x, g_ref[...]).astype(BF16)
    aw = 3 * GROUP_W
    a = _dot(h, w_ref[:, 0:aw])
    gbu_ref[:, 0:GROUP_W] = a[:, 0:GROUP_W]
    gbu_ref[:, GROUP_W:2 * GROUP_W] = a[:, GROUP_W:2 * GROUP_W] * a[:, 2 * GROUP_W:aw]
    qw = N_GROUPS * GROUP_W
    _store_chunks(q_ref, _dot(h, w_ref[:, aw:aw + qw]) * (HEAD_DIM ** -0.5))
    _store_chunks(kv_ref, _dot(h, w_ref[:, aw + qw:]))


def _store_chunks(ref, val):
    for c in range(val.shape[1] // LANES):
        ref[c] = val[:, c * LANES:(c + 1) * LANES]


def _load_chunks(ref, rows=None):
    n = ref.shape[0]
    return jnp.concatenate([ref[c] if rows is None else ref[c, rows, :] for c in range(n)], axis=-1)


def _inproj(xp, xs, g, w):
    mp, d = xp.shape
    ms = xs.shape[0]
    tm = ROW_TILE
    npt, nst = mp // tm, ms // tm
    m = mp + ms
    ncols = w.shape[1]
    qw = N_GROUPS * GROUP_W
    return pl.pallas_call(
        functools.partial(_inproj_kernel, npt),
        grid=(npt + nst,),
        in_specs=[
            pl.BlockSpec((tm, d), lambda i: (jnp.minimum(i, npt - 1), 0)),
            pl.BlockSpec((tm, d), lambda i: (jnp.maximum(i - npt, 0), 0)),
            pl.BlockSpec((1, d), lambda i: (0, 0)),
            pl.BlockSpec((d, ncols), lambda i: (0, 0)),
        ],
        out_specs=[
            pl.BlockSpec((qw // LANES, tm, LANES), lambda i: (0, i, 0)),
            pl.BlockSpec((2 * qw // LANES, tm, LANES), lambda i: (0, i, 0)),
            pl.BlockSpec((tm, 2 * GROUP_W), lambda i: (i, 0)),
        ],
        out_shape=[
            jax.ShapeDtypeStruct((qw // LANES, m, LANES), F32),
            jax.ShapeDtypeStruct((2 * qw // LANES, m, LANES), F32),
            jax.ShapeDtypeStruct((m, 2 * GROUP_W), F32),
        ],
        compiler_params=_params("arbitrary"),
        name="inproj",
    )(xp, xs, g, w)


def _attn_prompt_kernel(q_ref, kv_ref, o_ref, m_scr, l_scr, acc_scr):
    g = pl.program_id(1)
    seq = q_ref.shape[1]

    @pl.when(g == 0)
    def _():
        m_scr[...] = jnp.full(m_scr.shape, NEG_INF, F32)
        l_scr[...] = jnp.zeros(l_scr.shape, F32)
        acc_scr[...] = jnp.zeros(acc_scr.shape, F32)

    qi = lax.broadcasted_iota(jnp.int32, (NK, 2 * NK), 0)
    kc = lax.broadcasted_iota(jnp.int32, (NK, 2 * NK), 1)
    band = (kc >= qi) & (kc <= qi + NK)
    lane = lax.broadcasted_iota(jnp.int32, (NK, LANES), 1)
    lane_kv = lax.broadcasted_iota(jnp.int32, (2 * NK, LANES), 1)
    ones_kv = jnp.ones((2 * NK, LANES), BF16)
    n_chunks = GROUP_W // LANES
    hpc = LANES // HEAD_DIM

    def block(blk, d, last):
        span = NK * d
        base = pl.multiple_of(blk * span, span)
        pbase = pl.multiple_of(jnp.maximum(blk - 1, 0) * span, span)
        mask = band & ((blk > 0) | (kc >= NK))
        for r in range(d):
            rows = pl.ds(base + r, NK, stride=d) if d > 1 else pl.ds(base, NK)
            prows = pl.ds(pbase + r, NK, stride=d) if d > 1 else pl.ds(pbase, NK)
            m_old = m_scr[rows, :]
            l_old = l_scr[rows, :]
            m_new, l_new = m_old, l_old
            for c in range(GROUP_W // LANES):
                qc = q_ref[c, rows, :].astype(BF16)
                kc_ = jnp.concatenate([kv_ref[c, prows, :], kv_ref[c, rows, :]], axis=0).astype(BF16)
                vc_ = jnp.concatenate([kv_ref[n_chunks + c, prows, :], kv_ref[n_chunks + c, rows, :]],
                                      axis=0).astype(BF16)
                acc = acc_scr[c, rows, :]
                scale, denom, contrib = None, None, None
                for hh in range(hpc):
                    h = c * hpc + hh
                    sel_q = (lane >= hh * HEAD_DIM) & (lane < (hh + 1) * HEAD_DIM)
                    sel_v = (lane_kv >= hh * HEAD_DIM) & (lane_kv < (hh + 1) * HEAD_DIM)
                    s = lax.dot_general(jnp.where(sel_q, qc, 0), kc_, (((1,), (1,)), ((), ())),
                                        preferred_element_type=F32)
                    s = jnp.where(mask, s, NEG_INF)
                    mo = m_old[:, h:h + 1]
                    mn = jnp.maximum(mo, jnp.max(s, axis=-1, keepdims=True))
                    p = jnp.exp(s - mn)
                    alpha = jnp.exp(mo - mn)
                    pv = _dot(p.astype(BF16), jnp.concatenate([jnp.where(sel_v, vc_, 0), ones_kv], axis=1))
                    ln = alpha * l_old[:, h:h + 1] + pv[:, LANES:]
                    contrib = pv[:, :LANES] if contrib is None else contrib + pv[:, :LANES]
                    scale = alpha if scale is None else jnp.where(sel_q, alpha, scale)
                    denom = ln if denom is None else jnp.where(sel_q, ln, denom)
                    m_new = jnp.where(lane == h, mn, m_new)
                    l_new = jnp.where(lane == h, ln, l_new)
                acc = scale * acc + contrib
                if last:
                    o_ref[c, rows, :] = acc / denom
                else:
                    acc_scr[c, rows, :] = acc
            if not last:
                m_scr[rows, :] = m_new
                l_scr[rows, :] = l_new

    for gi, (window, d) in enumerate(DIL_GROUPS):
        @pl.when(g == gi)
        def _(d=d, last=(gi == N_GROUPS - 1)):
            def body(blk, carry):
                block(blk, d, last)
                return carry
            lax.fori_loop(0, seq // (NK * d), body, 0)


def _attn_prompt(q, kv, batch, seq):
    assert seq % (NK * max(d for _, d in DIL_GROUPS)) == 0
    return pl.pallas_call(
        _attn_prompt_kernel,
        grid=(batch, N_GROUPS),
        in_specs=[
            pl.BlockSpec((GROUP_W // LANES, seq, LANES), lambda b, g: (g, b, 0)),
            pl.BlockSpec((2 * GROUP_W // LANES, seq, LANES), lambda b, g: (g, b, 0)),
        ],
        out_specs=pl.BlockSpec((GROUP_W // LANES, seq, LANES), lambda b, g: (0, b, 0)),
        out_shape=jax.ShapeDtypeStruct((GROUP_W // LANES, batch * seq, LANES), F32),
        scratch_shapes=[
            pltpu.VMEM((seq, LANES), F32),
            pltpu.VMEM((seq, LANES), F32),
            pltpu.VMEM((GROUP_W // LANES, seq, LANES), F32),
        ],
        compiler_params=_params("arbitrary", "arbitrary", vmem_limit=ATTN_VMEM_LIMIT),
        name="attn_prompt",
    )(q, kv)


def _mix_sample_kernel(q_ref, kn_ref, vn_ref, gbu_ref, st_ref, c0_ref, c1_ref, c2_ref, w_ref,
                       yb_ref, ya_ref, ns_ref):
    sb, t_len = q_ref.shape[0], q_ref.shape[3]
    w = w_ref[...]
    caches = (c0_ref, c1_ref, c2_ref)
    nt_dims = (((1,), (1,)), ((), ()))
    qn = lax.broadcasted_iota(jnp.int32, (t_len, t_len), 0)
    tn = lax.broadcasted_iota(jnp.int32, (t_len, t_len), 1)

    for s in range(sb):
        for h in range(HEADS):
            sc_parts, v_parts = [], []
            for gi, (window, d) in enumerate(DIL_GROUPS):
                c_ref = caches[gi]
                n_pos = c_ref.shape[-1]
                qg = q_ref[s, gi, h].astype(BF16)
                qc = lax.broadcasted_iota(jnp.int32, (t_len, n_pos), 0)
                pc = lax.broadcasted_iota(jnp.int32, (t_len, n_pos), 1)
                sc = _dot(qg, c_ref[s, 0, h].astype(BF16))
                sc_parts.append(jnp.where((((pc - qc) & (d - 1)) == 0) & (pc >= qc), sc, NEG_INF))
                v_parts.append((c_ref[s, 1, h].astype(BF16), True))
                sn = lax.dot_general(qg, kn_ref[s, gi, h].astype(BF16), nt_dims, preferred_element_type=F32)
                sc_parts.append(jnp.where((tn <= qn) & (((qn - tn) & (d - 1)) == 0), sn, NEG_INF))
                v_parts.append((vn_ref[s, gi, h].astype(BF16), False))
            mx = functools.reduce(jnp.maximum, [jnp.max(p, axis=-1, keepdims=True) for p in sc_parts])
            den = jnp.zeros((t_len, 1), F32)
            acc = jnp.zeros((t_len, HEAD_DIM), F32)
            for sc, (v, transposed) in zip(sc_parts, v_parts):
                p = jnp.exp(sc - mx)
                den = den + jnp.sum(p, axis=-1, keepdims=True)
                pb = p.astype(BF16)
                acc = acc + (lax.dot_general(pb, v, nt_dims, preferred_element_type=F32) if transposed
                             else _dot(pb, v))
            yb_ref[s, h] = acc / den
        gbu = gbu_ref[s]
        gb = gbu[:, 0:GROUP_W]
        u = gbu[:, GROUP_W:]
        st = st_ref[s]
        ext = [st[0:1], st[1:2]] + [u[t:t + 1] for t in range(t_len)]
        for t in range(t_len):
            conv = w[0:1] * ext[t] + w[1:2] * ext[t + 1] + w[2:3] * ext[t + 2]
            ya_ref[s, t:t + 1, :] = gb[t:t + 1] * conv
        for r in range(A_KERNEL - 1):
            ns_ref[s, r:r + 1, :] = ext[t_len + r]


def _mix_sample(q5, kn, vn, gbu3, st, c0, c1, c2, w):
    n, t_len = q5.shape[0], q5.shape[3]
    sb = ATTN_SEQ_BLOCK
    assert t_len >= A_KERNEL - 1 and n % sb == 0
    for c, (window, d) in zip((c0, c1, c2), DIL_GROUPS):
        assert c.shape[-1] == NK * d and d & (d - 1) == 0
    qspec = pl.BlockSpec((sb, N_GROUPS, HEADS, t_len, HEAD_DIM), lambda i: (i, 0, 0, 0, 0))
    cspec = lambda c: pl.BlockSpec((sb,) + c.shape[1:], lambda i: (i, 0, 0, 0, 0))
    return pl.pallas_call(
        _mix_sample_kernel,
        grid=(n // sb,),
        in_specs=[
            qspec, qspec, qspec,
            pl.BlockSpec((sb, t_len, 2 * GROUP_W), lambda i: (i, 0, 0)),
            pl.BlockSpec((sb, A_KERNEL - 1, GROUP_W), lambda i: (i, 0, 0)),
            cspec(c0), cspec(c1), cspec(c2),
            pl.BlockSpec((A_KERNEL, GROUP_W), lambda i: (0, 0)),
        ],
        out_specs=[
            pl.BlockSpec((sb, HEADS, t_len, HEAD_DIM), lambda i: (i, 0, 0, 0)),
            pl.BlockSpec((sb, t_len, GROUP_W), lambda i: (i, 0, 0)),
            pl.BlockSpec((sb, A_KERNEL - 1, GROUP_W), lambda i: (i, 0, 0)),
        ],
        out_shape=[
            jax.ShapeDtypeStruct((n, HEADS, t_len, HEAD_DIM), F32),
            jax.ShapeDtypeStruct((n, t_len, GROUP_W), F32),
            jax.ShapeDtypeStruct((n, A_KERNEL - 1, GROUP_W), F32),
        ],
        compiler_params=_params("arbitrary"),
        name="mix_sample",
    )(q5, kn, vn, gbu3, st, c0, c1, c2, w)


def _route(h2, wr_ref, br_ref, xs_ref, rinfo_ref, gtab_ref):
    tm = h2.shape[0]
    h_hi = h2.astype(BF16)
    h_lo = (h2 - h_hi.astype(F32)).astype(BF16)
    nt_dims = (((1,), (1,)), ((), ()))
    w_hi, w_lo = wr_ref[0], wr_ref[1]
    lt = (lax.dot_general(w_hi, h_hi, nt_dims, preferred_element_type=F32)
          + lax.dot_general(w_hi, h_lo, nt_dims, preferred_element_type=F32)
          + lax.dot_general(w_lo, h_hi, nt_dims, preferred_element_type=F32)) + br_ref[...]
    rg = lax.broadcasted_iota(jnp.int32, (SUBLANES, tm), 0)
    gl = jnp.where(rg < N_EXPERT_GROUPS, lt[0:SUBLANES], NEG_INF)
    ge = jnp.exp(gl - jnp.max(gl, axis=0, keepdims=True))
    gp = ge / jnp.sum(ge, axis=0, keepdims=True)
    g_val = jnp.max(gp, axis=0, keepdims=True)
    g_idx = jnp.min(jnp.where(gp == g_val, rg, SUBLANES), axis=0, keepdims=True)
    re = lax.broadcasted_iota(jnp.int32, (N_EXPERTS, tm), 0)
    sel = (re // EXPERTS_PER_GROUP) == g_idx
    el = jnp.where(sel, lt[ROUTER_EXPERT_ROW0:ROUTER_EXPERT_ROW0 + N_EXPERTS], NEG_INF)
    ee = jnp.exp(el - jnp.max(el, axis=0, keepdims=True))
    ep = jnp.where(sel, ee / jnp.sum(ee, axis=0, keepdims=True), -1.0)
    v1 = jnp.max(ep, axis=0, keepdims=True)
    i1 = jnp.min(jnp.where(ep == v1, re, N_EXPERTS), axis=0, keepdims=True)
    ep2 = jnp.where(re == i1, -1.0, ep)
    v2 = jnp.max(ep2, axis=0, keepdims=True)
    i2 = jnp.min(jnp.where(ep2 == v2, re, N_EXPERTS), axis=0, keepdims=True)
    scale = g_val / (v1 + v2)

    oh = [(re == i1).astype(F32), (re == i2).astype(F32)]
    cmat = (oh[0] + oh[1]).astype(BF16)
    earlier = (lax.broadcasted_iota(jnp.int32, (tm, tm), 0)
               < lax.broadcasted_iota(jnp.int32, (tm, tm), 1)).astype(BF16)
    before = _dot(cmat, earlier)
    cnt_row = lax.dot_general(jnp.ones((SUBLANES, tm), BF16), cmat, nt_dims, preferred_element_type=F32)
    grp_row = ((cnt_row.astype(jnp.int32) + (SUBLANES - 1)) // SUBLANES).astype(F32)
    lower = (lax.broadcasted_iota(jnp.int32, (N_EXPERTS, N_EXPERTS), 0)
             < lax.broadcasted_iota(jnp.int32, (N_EXPERTS, N_EXPERTS), 1)).astype(BF16)
    goff_row = _dot(grp_row.astype(BF16), lower)
    slot_rows = lax.broadcasted_iota(jnp.int32, (TILE_SLOTS, tm), 0)
    place = None
    pos = []
    for k in range(2):
        rank = jnp.sum(oh[k] * before, axis=0, keepdims=True)
        seg = _dot(goff_row.astype(BF16), oh[k].astype(BF16))[0:1]
        pos.append(seg * SUBLANES + rank)
        hit = slot_rows == pos[k].astype(jnp.int32)
        place = hit if place is None else place | hit
    xs_ref[...] = _dot(jnp.where(place, 1.0, 0.0).astype(BF16), h_hi)
    rl = lax.broadcasted_iota(jnp.int32, (LANES, tm), 0)
    info_t = jnp.where(rl == 0, v1 * scale, jnp.where(rl == 1, v2 * scale,
                       jnp.where(rl == 2, pos[0], jnp.where(rl == 3, pos[1], 0.0))))
    rinfo_ref[...] = info_t.T
    spread = (lax.broadcasted_iota(jnp.int32, (N_EXPERTS, LANES), 0)
              == lax.broadcasted_iota(jnp.int32, (N_EXPERTS, LANES), 1)).astype(BF16)
    gtab_ref[...] = _dot(grp_row.astype(BF16), spread).astype(jnp.int32)


ROUTER_EXPERT_ROW0 = SUBLANES
ROUTER_ROWS = 48
TILE_GROUPS = -(-(2 * ROW_TILE + (SUBLANES - 1) * N_EXPERTS) // SUBLANES)
TILE_SLOTS = TILE_GROUPS * SUBLANES


def _router_specs(tm, d):
    ins = [pl.BlockSpec((1, d), lambda i: (0, 0)),
           pl.BlockSpec((2, ROUTER_ROWS, d), lambda i: (0, 0, 0)),
           pl.BlockSpec((ROUTER_ROWS, 1), lambda i: (0, 0))]
    outs = [pl.BlockSpec((TILE_SLOTS, d), lambda i: (i, 0)),
            pl.BlockSpec((tm, LANES), lambda i: (i, 0)),
            pl.BlockSpec((None, SUBLANES, LANES), lambda i: (i, 0, 0))]
    return ins, outs


def _router_shapes(m, d):
    nb = m // ROW_TILE
    return [jax.ShapeDtypeStruct((nb * TILE_SLOTS, d), F32), jax.ShapeDtypeStruct((m, LANES), F32),
            jax.ShapeDtypeStruct((nb, SUBLANES, LANES), jnp.int32)]


def _outproj_kernel(n_prompt_tiles, tiles_per_seq, xp_ref, xs_ref, gbu_ref, halo_ref, cw_ref, yb_ref, ys_ref,
                    wo_ref, g_ref, wr_ref, br_ref, x1_ref, xs_out_ref, rinfo_ref, gtab_ref, conv_scr):
    i = pl.program_id(0)
    tm = xp_ref.shape[0]
    is_p = i < n_prompt_tiles
    x = jnp.where(is_p, xp_ref[...], xs_ref[...])
    u = gbu_ref[:, GROUP_W:]
    first = (i % tiles_per_seq) == 0
    conv_scr[0:SUBLANES, :] = jnp.where(first, 0.0, halo_ref[:, GROUP_W:])
    conv_scr[SUBLANES:, :] = u
    cw = cw_ref[...]
    ya = gbu_ref[:, 0:GROUP_W] * (cw[0:1, :] * conv_scr[pl.ds(SUBLANES - 2, tm), :]
                                  + cw[1:2, :] * conv_scr[pl.ds(SUBLANES - 1, tm), :] + cw[2:3, :] * u)
    ymix = jnp.where(is_p, jnp.concatenate([ya, _load_chunks(yb_ref)], axis=-1), ys_ref[...])
    x1 = x + _dot(ymix.astype(BF16), wo_ref[...])
    x1_ref[...] = x1
    _route(_rms(x1, g_ref[...]), wr_ref, br_ref, xs_out_ref, rinfo_ref, gtab_ref)


def _outproj(xp, xs, gbu, conv_w, yb_p, ymix_s, wo, g, wr, br, seq):
    mp, d = xp.shape
    ms = xs.shape[0]
    tm = ROW_TILE
    npt, nst = mp // tm, ms // tm
    m = mp + ms
    r_in, r_out = _router_specs(tm, d)
    pmap = lambda i: (jnp.minimum(i, npt - 1), 0)
    smap = lambda i: (jnp.maximum(i - npt, 0), 0)
    halo_map = lambda i: (jnp.maximum(jnp.minimum(i, npt - 1) * (tm // SUBLANES) - 1, 0), 0)
    return pl.pallas_call(
        functools.partial(_outproj_kernel, npt, seq // tm),
        grid=(npt + nst,),
        in_specs=[
            pl.BlockSpec((tm, d), pmap),
            pl.BlockSpec((tm, d), smap),
            pl.BlockSpec((tm, 2 * GROUP_W), pmap),
            pl.BlockSpec((SUBLANES, 2 * GROUP_W), halo_map),
            pl.BlockSpec((A_KERNEL, GROUP_W), lambda i: (0, 0)),
            pl.BlockSpec((GROUP_W // LANES, tm, LANES), lambda i: (0, jnp.minimum(i, npt - 1), 0)),
            pl.BlockSpec((tm, 2 * GROUP_W), smap),
            pl.BlockSpec((2 * GROUP_W, d), lambda i: (0, 0)),
        ] + r_in,
        out_specs=[pl.BlockSpec((tm, d), lambda i: (i, 0))] + r_out,
        out_shape=[jax.ShapeDtypeStruct((m, d), F32)] + _router_shapes(m, d),
        scratch_shapes=[pltpu.VMEM((tm + SUBLANES, GROUP_W), F32)],
        compiler_params=_params("arbitrary"),
        name="outproj_router",
    )(xp, xs, gbu, gbu, conv_w, yb_p, ymix_s, wo, g, wr, br)


def _experts_kernel(te_ref, src_ref, dst_ref, nt_ref, h_hbm, wg_ref, wu_ref, wd_ref, y_hbm, xbuf, ybuf, gsem, ssem,
                    *, trash_row):
    i = pl.program_id(0)
    nt = nt_ref[0]
    ng = CHUNK_GROUPS
    slot = i % 2
    other = 1 - slot

    def group_rows(r):
        return pl.ds(r if isinstance(r, int) else pl.multiple_of(r, SUBLANES), SUBLANES)

    def gather_copy(p, sl, j):
        return pltpu.make_async_copy(h_hbm.at[group_rows(src_ref[p]), :],
                                     xbuf.at[sl, group_rows(j * SUBLANES), :], gsem.at[sl])

    def scatter_copy(p, sl, j):
        return pltpu.make_async_copy(ybuf.at[sl, group_rows(j * SUBLANES), :],
                                     y_hbm.at[group_rows(dst_ref[p]), :], ssem.at[sl])

    def wait_gather(sl):
        pltpu.make_async_copy(h_hbm.at[pl.ds(0, ng * SUBLANES), :], xbuf.at[sl], gsem.at[sl]).wait()

    def wait_scatter(sl):
        pltpu.make_async_copy(ybuf.at[sl], y_hbm.at[pl.ds(0, ng * SUBLANES), :], ssem.at[sl]).wait()

    @pl.when(i == 0)
    def _():
        ybuf[...] = jnp.zeros(ybuf.shape, F32)
        init = pltpu.make_async_copy(ybuf.at[0], y_hbm.at[pl.ds(trash_row, ng * SUBLANES), :], ssem.at[0])
        init.start()
        init.wait()

        def tail_copy(b):
            first = pl.multiple_of(b * TILE_SLOTS + 2 * ROW_TILE, SUBLANES)
            return pltpu.make_async_copy(ybuf.at[0, pl.ds(0, TILE_SLOTS - 2 * ROW_TILE), :],
                                         y_hbm.at[pl.ds(first, TILE_SLOTS - 2 * ROW_TILE), :], ssem.at[0])

        def start_body(b, c):
            tail_copy(b).start()
            return c

        def wait_body(b, c):
            tail_copy(b).wait()
            return c
        lax.fori_loop(0, trash_row // TILE_SLOTS, start_body, 0)
        lax.fori_loop(0, trash_row // TILE_SLOTS, wait_body, 0)
        for j in range(ng):
            gather_copy(j, 0, j).start()

    @pl.when(i < nt)
    def _():
        wait_gather(slot)

        @pl.when(i >= 1)
        def _():
            wait_scatter(slot)

        base = i * ng
        for j in range(ng):
            gather_copy(base + ng + j, other, j).start(priority=j % 2)
            scatter_copy(base + j, other, j).start(priority=(j + 1) % 2)
        x = xbuf[slot].astype(BF16)
        gate = _dot(x, wg_ref[...].astype(BF16))
        up = _dot(x, wu_ref[...].astype(BF16))
        act = (gate * jax.nn.sigmoid(gate) * up).astype(BF16)
        ybuf[slot] = _dot(act, wd_ref[...].astype(BF16))

    @pl.when(i == nt)
    def _():
        wait_gather(slot)
        wait_scatter(slot)
        for j in range(ng):
            scatter_copy(i * ng + j, other, j).start()
        wait_scatter(other)


def _experts(xs, tile_expert, src, dst, n_tiles, wg, wu, wd, layer):
    ng = CHUNK_GROUPS
    nt_max = tile_expert.shape[0]
    d, f = wg.shape[2], wg.shape[3]
    n_rows = xs.shape[0] + ng * SUBLANES
    wmap = lambda i, te_ref, src_ref, dst_ref, nt_ref: (layer, te_ref[i], 0, 0)
    grid_spec = pltpu.PrefetchScalarGridSpec(
        num_scalar_prefetch=4,
        grid=(nt_max,),
        in_specs=[
            pl.BlockSpec(memory_space=pl.ANY),
            pl.BlockSpec((None, None, d, f), wmap),
            pl.BlockSpec((None, None, d, f), wmap),
            pl.BlockSpec((None, None, f, d), wmap),
        ],
        out_specs=pl.BlockSpec(memory_space=pl.ANY),
        scratch_shapes=[
            pltpu.VMEM((2, ng * SUBLANES, d), F32),
            pltpu.VMEM((2, ng * SUBLANES, d), F32),
            pltpu.SemaphoreType.DMA((2,)),
            pltpu.SemaphoreType.DMA((2,)),
        ],
    )
    return pl.pallas_call(
        functools.partial(_experts_kernel, trash_row=xs.shape[0]),
        grid_spec=grid_spec,
        out_shape=jax.ShapeDtypeStruct((n_rows, d), F32),
        compiler_params=_params("arbitrary"),
        name="experts",
    )(tile_expert, src, dst, n_tiles, xs, wg, wu, wd)


def _plan_chunks(gtab):
    ng = CHUNK_GROUPS
    grp = gtab[:, 0, :N_EXPERTS]
    nb = grp.shape[0]
    nt_max = -(-nb * TILE_GROUPS // ng) + N_EXPERTS + 1
    total = jnp.sum(grp, axis=0)
    chunks = (total + ng - 1) // ng
    chunk_end = jnp.cumsum(chunks)
    chunk_start = chunk_end - chunks
    n_chunks = chunk_end[-1]
    seg_off = jnp.cumsum(grp, axis=1) - grp
    seg_end = jnp.cumsum(grp, axis=0)
    seg_start = seg_end - grp
    cidx = jnp.arange(nt_max, dtype=jnp.int32)
    chunk_expert = jnp.sum((cidx[:, None] >= chunk_end[None, :]).astype(jnp.int32), axis=1)
    last_expert = jnp.max(jnp.where(total > 0, jnp.arange(N_EXPERTS), 0))
    chunk_expert = jnp.where(cidx < n_chunks, jnp.minimum(chunk_expert, N_EXPERTS - 1), last_expert).astype(jnp.int32)
    is_e = chunk_expert[:, None] == jnp.arange(N_EXPERTS, dtype=jnp.int32)[None, :]
    pick = lambda per_expert: jnp.sum(jnp.where(is_e, per_expert[None, :], 0), axis=1)
    pick_rows = lambda table: jnp.sum(jnp.where(is_e[:, :, None], table.T[None, :, :], 0), axis=1)
    within = (cidx - pick(chunk_start))[:, None] * ng + jnp.arange(ng, dtype=jnp.int32)[None, :]
    valid = (within < pick(total)[:, None]) & (cidx < n_chunks)[:, None]
    ends = pick_rows(seg_end)
    base = pick_rows(jnp.arange(nb, dtype=jnp.int32)[:, None] * TILE_GROUPS + seg_off - seg_start)
    step = jnp.concatenate([base[:, :1], base[:, 1:] - base[:, :-1]], axis=1)
    passed = jnp.concatenate([jnp.ones((nt_max, ng, 1), bool), within[:, :, None] >= ends[:, None, :-1]], axis=2)
    group = within + jnp.sum(jnp.where(passed, step[:, None, :], 0), axis=2)
    pad_dst = (nb * TILE_GROUPS + jnp.arange(ng, dtype=jnp.int32)) * SUBLANES
    src = jnp.where(valid, group * SUBLANES, 0).astype(jnp.int32).reshape(nt_max * ng)
    dst = jnp.where(valid, group * SUBLANES, pad_dst[None, :]).astype(jnp.int32).reshape(nt_max * ng)
    dst = jnp.concatenate([pad_dst, dst])
    return chunk_expert, src, dst, n_chunks.astype(jnp.int32).reshape(1)


def _combine(x_ref, y_ref, rinfo_ref):
    tm = x_ref.shape[0]
    n_slots = y_ref.shape[0]
    r = rinfo_ref[...]
    y = y_ref[...].astype(BF16)
    col = lax.broadcasted_iota(jnp.int32, (tm, n_slots), 1)
    out = x_ref[...]
    for k in range(2):
        pick = jnp.where(col == r[:, 2 + k:3 + k].astype(jnp.int32), 1.0, 0.0).astype(BF16)
        out = out + r[:, k:k + 1] * _dot(pick, y)
    return out


def _combine_specs(tm, d):
    return [
        pl.BlockSpec((tm, d), lambda i: (i, 0)),
        pl.BlockSpec((TILE_SLOTS, d), lambda i: (i, 0)),
        pl.BlockSpec((tm, LANES), lambda i: (i, 0)),
    ]


def _combine_pw1_kernel(x_ref, y_ref, rinfo_ref, g_ref, w_ref, b_ref, x2_ref, u_ref):
    x2 = _combine(x_ref, y_ref, rinfo_ref)
    x2_ref[...] = x2
    d = x2.shape[1]
    z = _dot(_rms(x2, g_ref[...]).astype(BF16), w_ref[...]) + b_ref[...]
    u_ref[...] = z[:, :d] * jax.nn.sigmoid(z[:, d:])


def _combine_pw1(x1, y2, rinfo, g, w, b):
    m, d = x1.shape
    tm = ROW_TILE
    return pl.pallas_call(
        _combine_pw1_kernel,
        grid=(m // tm,),
        in_specs=_combine_specs(tm, d) + [
            pl.BlockSpec((1, d), lambda i: (0, 0)),
            pl.BlockSpec((d, 2 * d), lambda i: (0, 0)),
            pl.BlockSpec((1, 2 * d), lambda i: (0, 0)),
        ],
        out_specs=[pl.BlockSpec((tm, d), lambda i: (i, 0)), pl.BlockSpec((tm, d), lambda i: (i, 0))],
        out_shape=[jax.ShapeDtypeStruct((m, d), F32), jax.ShapeDtypeStruct((m, d), F32)],
        compiler_params=_params("arbitrary"),
        name="combine_pw1",
    )(x1, y2, rinfo, g, w, b)


def _combine_final_kernel(n_prompt_tiles, x_ref, y_ref, rinfo_ref, g_ref, yp_ref, ys_ref):
    i = pl.program_id(0)
    y = _rms(_combine(x_ref, y_ref, rinfo_ref), g_ref[...])

    @pl.when(i < n_prompt_tiles)
    def _():
        yp_ref[...] = y

    @pl.when(i >= n_prompt_tiles)
    def _():
        ys_ref[...] = y


def _combine_final(x3, y2, rinfo, g, mp):
    m, d = x3.shape
    tm = ROW_TILE
    npt = mp // tm
    return pl.pallas_call(
        functools.partial(_combine_final_kernel, npt),
        grid=(m // tm,),
        in_specs=_combine_specs(tm, d) + [pl.BlockSpec((1, d), lambda i: (0, 0))],
        out_specs=[pl.BlockSpec((tm, d), lambda i: (jnp.minimum(i, npt - 1), 0)),
                   pl.BlockSpec((tm, d), lambda i: (jnp.maximum(i - npt, 0), 0))],
        out_shape=[jax.ShapeDtypeStruct((mp, d), F32), jax.ShapeDtypeStruct((m - mp, d), F32)],
        compiler_params=_params("arbitrary"),
        name="combine_final",
    )(x3, y2, rinfo, g)


def _dwconv_prompt_kernel(tiles_per_seq, u_ref, halo_ref, w_ref, b_ref, c_ref, scr):
    i = pl.program_id(0)
    tm = u_ref.shape[0]
    first = (i % tiles_per_seq) == 0
    scr[0, 0:CONV_HALO, :] = jnp.where(first, 0.0, halo_ref[...])
    scr[0, CONV_HALO:, :] = u_ref[...]
    n_shift = tm + CONV_HALO - SUBLANES
    for s in range(1, SUBLANES):
        scr[s, 0:n_shift, :] = scr[0, pl.ds(s, n_shift), :]
    acc = jnp.broadcast_to(b_ref[...], c_ref.shape)
    for k in range(C_KERNEL):
        off = CONV_HALO - (C_KERNEL - 1) + k
        acc = acc + w_ref[k:k + 1, :] * scr[off % SUBLANES, pl.ds(off - off % SUBLANES, tm), :]
    c_ref[...] = acc


def _dwconv_prompt(u, w, b, mp, seq):
    tm = ROW_TILE
    d = u.shape[1]
    return pl.pallas_call(
        functools.partial(_dwconv_prompt_kernel, seq // tm),
        grid=(mp // tm,),
        in_specs=[
            pl.BlockSpec((tm, d), lambda i: (i, 0)),
            pl.BlockSpec((CONV_HALO, d), lambda i: (jnp.maximum(i * (tm // CONV_HALO) - 1, 0), 0)),
            pl.BlockSpec((C_KERNEL, d), lambda i: (0, 0)),
            pl.BlockSpec((1, d), lambda i: (0, 0)),
        ],
        out_specs=pl.BlockSpec((tm, d), lambda i: (i, 0)),
        out_shape=jax.ShapeDtypeStruct((mp, d), F32),
        scratch_shapes=[pltpu.VMEM((SUBLANES, tm + CONV_HALO, d), F32)],
        compiler_params=_params("arbitrary"),
        name="dwconv_prompt",
    )(u, u, w, b)


def _dwconv_sample_kernel(st_ref, u_ref, w_ref, b_ref, c_ref, ns_ref):
    n_state, t_len = st_ref.shape[0], u_ref.shape[0]
    rows = [st_ref[r] for r in range(n_state)] + [u_ref[t] for t in range(t_len)]
    for t in range(t_len):
        acc = jnp.broadcast_to(b_ref[...], rows[0].shape)
        for k in range(C_KERNEL):
            acc = acc + w_ref[k:k + 1, :] * rows[t + k]
        c_ref[t] = acc
    for r in range(n_state):
        ns_ref[r] = rows[r + t_len]


def _dwconv_sample(st, u3, w, b):
    n_state, n, d = st.shape
    t_len = u3.shape[0]
    sb = SEQ_BLOCK
    assert n_state == C_KERNEL - 1 and n % sb == 0
    return pl.pallas_call(
        _dwconv_sample_kernel,
        grid=(n // sb,),
        in_specs=[
            pl.BlockSpec((n_state, sb, d), lambda i: (0, i, 0)),
            pl.BlockSpec((t_len, sb, d), lambda i: (0, i, 0)),
            pl.BlockSpec((C_KERNEL, d), lambda i: (0, 0)),
            pl.BlockSpec((1, d), lambda i: (0, 0)),
        ],
        out_specs=[
            pl.BlockSpec((t_len, sb, d), lambda i: (0, i, 0)),
            pl.BlockSpec((n_state, sb, d), lambda i: (0, i, 0)),
        ],
        out_shape=[jax.ShapeDtypeStruct((t_len, n, d), F32), jax.ShapeDtypeStruct((n_state, n, d), F32)],
        compiler_params=_params("arbitrary"),
        name="dwconv_sample",
    )(st, u3, w, b)


def _conf_tail_kernel(n_prompt_tiles, x_ref, cp_ref, cs_ref, lg_ref, lb_ref, w_ref, b_ref, g_ref, wr_ref, br_ref,
                      x3_ref, xs_out_ref, rinfo_ref, gtab_ref):
    i = pl.program_id(0)
    c = jnp.where(i < n_prompt_tiles, cp_ref[...], cs_ref[...])
    mu = jnp.mean(c, axis=-1, keepdims=True)
    cc = c - mu
    var = jnp.mean(cc * cc, axis=-1, keepdims=True)
    y = cc * lax.rsqrt(var + LN_EPS) * lg_ref[...] + lb_ref[...]
    y = y * jax.nn.sigmoid(y)
    x3 = x_ref[...] + _dot(y.astype(BF16), w_ref[...]) + b_ref[...]
    x3_ref[...] = x3
    _route(_rms(x3, g_ref[...]), wr_ref, br_ref, xs_out_ref, rinfo_ref, gtab_ref)


def _conf_tail(x2, c_p, c_s, ln_g, ln_b, w, b, g, wr, br):
    m, d = x2.shape
    tm = ROW_TILE
    npt = c_p.shape[0] // tm
    r_in, r_out = _router_specs(tm, d)
    vec = pl.BlockSpec((1, d), lambda i: (0, 0))
    return pl.pallas_call(
        functools.partial(_conf_tail_kernel, npt),
        grid=(m // tm,),
        in_specs=[
            pl.BlockSpec((tm, d), lambda i: (i, 0)),
            pl.BlockSpec((tm, d), lambda i: (jnp.minimum(i, npt - 1), 0)),
            pl.BlockSpec((tm, d), lambda i: (jnp.maximum(i - npt, 0), 0)),
            vec, vec,
            pl.BlockSpec((d, d), lambda i: (0, 0)),
            vec,
        ] + r_in,
        out_specs=[pl.BlockSpec((tm, d), lambda i: (i, 0))] + r_out,
        out_shape=[jax.ShapeDtypeStruct((m, d), F32)] + _router_shapes(m, d),
        compiler_params=_params("arbitrary"),
        name="conf_tail_router",
    )(x2, c_p, c_s, ln_g, ln_b, w, b, g, wr, br)


def _router_weights(w_rg, b_rg, w_re, b_re):
    d = w_rg.shape[0]
    gpad = ROUTER_EXPERT_ROW0 - N_EXPERT_GROUPS
    epad = ROUTER_ROWS - ROUTER_EXPERT_ROW0 - N_EXPERTS
    wt = jnp.concatenate([w_rg.T, jnp.zeros((gpad, d), F32), w_re.T, jnp.zeros((epad, d), F32)], axis=0)
    w_hi = wt.astype(BF16)
    w_lo = (wt - w_hi.astype(F32)).astype(BF16)
    br = jnp.concatenate([b_rg, jnp.zeros((gpad,), F32), b_re, jnp.zeros((epad,), F32)])[:, None]
    return jnp.stack([w_hi, w_lo]), br


def _moe(xs, gtab, wg, wu, wd, layer):
    chunk_expert, src, dst, n_chunks = _plan_chunks(gtab)
    return _experts(xs, chunk_expert, src, dst, n_chunks, wg, wu, wd, layer)


def kernel(x_prompt, x_sample, state_shortconv, cache_kv_w128, cache_kv_w512, cache_kv_w2048, state_conformer,
           g_mix, g_ffn, g_final, w_in, conv_a_w, w_out, w_pw1, b_pw1, dw_w, dw_b, ln_g, ln_b, w_pw2, b_pw2,
           w_router_group, b_router_group, w_router_expert, b_router_expert, w_gate, w_up, w_down):
    batch, seq, d = x_prompt.shape
    n_dec, t_dec, _ = x_sample.shape
    mp, ms = batch * seq, n_dec * t_dec
    assert g_mix.shape[0] == 2 and mp % ROW_TILE == 0 and ms % ROW_TILE == 0 and seq % ROW_TILE == 0
    xp = x_prompt.reshape(mp, d)
    xs = x_sample.reshape(ms, d)
    slab = (HEADS, HEAD_DIM)

    aw, qw = 3 * GROUP_W, N_GROUPS * GROUP_W
    w0 = w_in[0]
    kcols = w0[:, aw + qw:aw + 2 * qw].reshape(d, N_GROUPS, GROUP_W)
    vcols = w0[:, aw + 2 * qw:].reshape(d, N_GROUPS, GROUP_W)
    w_perm = jnp.concatenate([w0[:, :aw + qw], jnp.stack([kcols, vcols], axis=2).reshape(d, 2 * qw)], axis=1)
    q, kv, gbu = _inproj(xp, xs, g_mix[0][None, :], w_perm.astype(BF16))

    yb_p = _attn_prompt(q, kv, batch, seq)

    hpc = LANES // HEAD_DIM
    q_s = jnp.transpose(q[:, mp:].reshape(N_GROUPS, HEADS // hpc, n_dec, t_dec, hpc, HEAD_DIM),
                        (2, 0, 1, 4, 3, 5)).reshape(n_dec, N_GROUPS, HEADS, t_dec, HEAD_DIM)
    kv_s = jnp.transpose(kv[:, mp:].reshape(N_GROUPS, 2, HEADS // hpc, n_dec, t_dec, hpc, HEAD_DIM),
                         (1, 3, 0, 2, 5, 4, 6)).reshape(2, n_dec, N_GROUPS, HEADS, t_dec, HEAD_DIM)
    gbu_s = gbu[mp:].reshape(n_dec, t_dec, 2 * GROUP_W)
    c0, c1, c2 = (jnp.transpose(c[0], (0, 2, 3, 4, 1)) for c in (cache_kv_w128, cache_kv_w512, cache_kv_w2048))
    yb_s, ya_s, s_sc = _mix_sample(q_s, kv_s[0], kv_s[1], gbu_s, state_shortconv[0], c0, c1, c2, conv_a_w[0])
    ymix_s = jnp.concatenate([ya_s.reshape(ms, GROUP_W),
                              jnp.transpose(yb_s, (0, 2, 1, 3)).reshape(ms, GROUP_W)], axis=-1)

    wr0, br0 = _router_weights(w_router_group[0], b_router_group[0], w_router_expert[0], b_router_expert[0])
    x1, xsort0, rinfo0, gtab0 = _outproj(xp, xs, gbu, conv_a_w[0], yb_p, ymix_s, w_out[0].astype(BF16),
                                         g_ffn[0][None, :], wr0, br0, seq)
    y2 = _moe(xsort0, gtab0, w_gate, w_up, w_down, 0)

    x2, u = _combine_pw1(x1, y2, rinfo0, g_mix[1][None, :], w_pw1[0].astype(BF16), b_pw1[0][None, :])
    c_p = _dwconv_prompt(u, dw_w[0], dw_b[0][None, :], mp, seq)
    c_s3, s_cf = _dwconv_sample(jnp.transpose(state_conformer[0], (1, 0, 2)),
                                jnp.transpose(u[mp:].reshape(n_dec, t_dec, d), (1, 0, 2)), dw_w[0], dw_b[0][None, :])
    c_s = jnp.transpose(c_s3, (1, 0, 2)).reshape(ms, d)
    wr1, br1 = _router_weights(w_router_group[1], b_router_group[1], w_router_expert[1], b_router_expert[1])
    x3, xsort1, rinfo1, gtab1 = _conf_tail(x2, c_p, c_s, ln_g[0][None, :], ln_b[0][None, :],
                                           w_pw2[0].astype(BF16), b_pw2[0][None, :], g_ffn[1][None, :], wr1, br1)
    y2b = _moe(xsort1, gtab1, w_gate, w_up, w_down, 1)
    y_p, y_s = _combine_final(x3, y2b, rinfo1, g_final[None, :], mp)

    cpg = 2 * GROUP_W // LANES

    def tail_rows(arr, n, axis):
        return jnp.stack([lax.slice_in_dim(arr, (b + 1) * seq - n, (b + 1) * seq, axis=axis) for b in range(batch)])

    p_kv = []
    for gi, (w, _) in enumerate(DIL_GROUPS):
        rows = tail_rows(kv[gi * cpg:(gi + 1) * cpg], min(w, seq), 1)
        p_kv.append(jnp.transpose(rows, (0, 2, 1, 3)).reshape(1, batch, min(w, seq), 2, *slab))
    s_kv = [jnp.transpose(kv_s[:, :, gi], (1, 3, 0, 2, 4))[None] for gi in range(N_GROUPS)]
    p_sc = tail_rows(gbu, A_KERNEL - 1, 0)[:, :, GROUP_W:][None]
    p_cf = tail_rows(u, C_KERNEL - 1, 0)[None]
    return (y_p.reshape(batch, seq, d), y_s.reshape(n_dec, t_dec, d), p_sc, p_kv[0], p_kv[1], p_kv[2], p_cf,
            s_sc[None], s_kv[0], s_kv[1], s_kv[2], jnp.transpose(s_cf, (1, 0, 2))[None])
```

```python
import functools

import jax
import jax.numpy as jnp
from jax import lax
from jax.experimental import pallas as pl
from jax.experimental.pallas import tpu as pltpu

F32 = jnp.float32
BF16 = jnp.bfloat16

DIL_GROUPS = ((128, 1), (512, 4), (2048, 16))
N_GROUPS = len(DIL_GROUPS)
HEADS = 4
HEAD_DIM = 64
GROUP_W = HEADS * HEAD_DIM
NK = DIL_GROUPS[0][0] // DIL_GROUPS[0][1]
A_KERNEL = 3
C_KERNEL = 31
N_EXPERT_GROUPS = 4
EXPERTS_PER_GROUP = 8
N_EXPERTS = N_EXPERT_GROUPS * EXPERTS_PER_GROUP
RMS_EPS = 1e-6
LN_EPS = 1e-5
NEG_INF = -1e30

SUBLANES = 8
LANES = 128
VMEM_LIMIT = 48 * 1024 * 1024
ATTN_VMEM_LIMIT = 60 * 1024 * 1024
ATTN_BLOCKS_PER_BODY = 4

ROW_TILE = 256
CHUNK_GROUPS = 64
CONV_HALO = 32
SEQ_BLOCK = 8
ATTN_SEQ_BLOCK = 2


def _params(*sem, vmem_limit=VMEM_LIMIT):
    return pltpu.CompilerParams(dimension_semantics=sem, vmem_limit_bytes=vmem_limit)


def _rms(x, g):
    return x * lax.rsqrt(jnp.mean(x * x, axis=-1, keepdims=True) + RMS_EPS) * g


def _dot(a, b):
    return jnp.dot(a, b, preferred_element_type=F32)


def _rows_to_lanes(ref, n_rows, base=None):
    parts = []
    for s in range(SUBLANES):
        idx = pl.ds(s, n_rows, stride=SUBLANES)
        parts.append(ref[idx, :] if base is None else ref[base, idx, :])
    return jnp.concatenate(parts, axis=-1)


def _lanes_to_rows(ref, val, base=None):
    n_rows = val.shape[0]
    for s in range(SUBLANES):
        idx = pl.ds(s, n_rows, stride=SUBLANES)
        piece = val[:, s * LANES:(s + 1) * LANES]
        if base is None:
            ref[idx, :] = piece
        else:
            ref[base, idx, :] = piece


def _inproj_kernel(n_prompt_tiles, xp_ref, xs_ref, g_ref, w_ref, q_ref, kv_ref, gbu_ref):
    i = pl.program_id(0)
    x = jnp.where(i < n_prompt_tiles, xp_ref[...], xs_ref[...])
    h = _rms(x, g_ref[...]).astype(BF16)
    aw = 3 * GROUP_W
    a = _dot(h, w_ref[:, 0:aw])
    gbu_ref[:, 0:GROUP_W] = a[:, 0:GROUP_W]
    gbu_ref[:, GROUP_W:2 * GROUP_W] = a[:, GROUP_W:2 * GROUP_W] * a[:, 2 * GROUP_W:aw]
    qw = N_GROUPS * GROUP_W
    _store_chunks(q_ref, _dot(h, w_ref[:, aw:aw + qw]) * (HEAD_DIM ** -0.5))
    _store_chunks(kv_ref, _dot(h, w_ref[:, aw + qw:]))


def _store_chunks(ref, val):
    for c in range(val.shape[1] // LANES):
        ref[c] = val[:, c * LANES:(c + 1) * LANES]


def _load_chunks(ref, rows=None):
    n = ref.shape[0]
    return jnp.concatenate([ref[c] if rows is None else ref[c, rows, :] for c in range(n)], axis=-1)


def _inproj(xp, xs, g, w):
    mp, d = xp.shape
    ms = xs.shape[0]
    tm = ROW_TILE
    npt, nst = mp // tm, ms // tm
    m = mp + ms
    ncols = w.shape[1]
    qw = N_GROUPS * GROUP_W
    return pl.pallas_call(
        functools.partial(_inproj_kernel, npt),
        grid=(npt + nst,),
        in_specs=[
            pl.BlockSpec((tm, d), lambda i: (jnp.minimum(i, npt - 1), 0)),
            pl.BlockSpec((tm, d), lambda i: (jnp.maximum(i - npt, 0), 0)),
            pl.BlockSpec((1, d), lambda i: (0, 0)),
            pl.BlockSpec((d, ncols), lambda i: (0, 0)),
        ],
        out_specs=[
            pl.BlockSpec((qw // LANES, tm, LANES), lambda i: (0, i, 0)),
            pl.BlockSpec((2 * qw // LANES, tm, LANES), lambda i: (0, i, 0)),
            pl.BlockSpec((tm, 2 * GROUP_W), lambda i: (i, 0)),
        ],
        out_shape=[
            jax.ShapeDtypeStruct((qw // LANES, m, LANES), F32),
            jax.ShapeDtypeStruct((2 * qw // LANES, m, LANES), F32),
            jax.ShapeDtypeStruct((m, 2 * GROUP_W), F32),
        ],
        compiler_params=_params("arbitrary"),
        name="inproj",
    )(xp, xs, g, w)


def _attn_prompt_kernel(q_ref, kv_ref, o_ref, m_scr, l_scr, acc_scr):
    g = pl.program_id(1)
    seq = q_ref.shape[1]

    @pl.when(g == 0)
    def _():
        m_scr[...] = jnp.full(m_scr.shape, NEG_INF, F32)
        l_scr[...] = jnp.zeros(l_scr.shape, F32)
        acc_scr[...] = jnp.zeros(acc_scr.shape, F32)

    qi = lax.broadcasted_iota(jnp.int32, (NK, 2 * NK), 0)
    kc = lax.broadcasted_iota(jnp.int32, (NK, 2 * NK), 1)
    band = (kc >= qi) & (kc <= qi + NK)
    lane = lax.broadcasted_iota(jnp.int32, (NK, LANES), 1)
    lane_kv = lax.broadcasted_iota(jnp.int32, (2 * NK, LANES), 1)
    ones_kv = jnp.ones((2 * NK, LANES), BF16)
    n_chunks = GROUP_W // LANES
    hpc = LANES // HEAD_DIM

    def block(blk, d, last):
        span = NK * d
        base = pl.multiple_of(blk * span, span)
        pbase = pl.multiple_of(jnp.maximum(blk - 1, 0) * span, span)
        mask = band & ((blk > 0) | (kc >= NK))
        for r in range(d):
            rows = pl.ds(base + r, NK, stride=d) if d > 1 else pl.ds(base, NK)
            prows = pl.ds(pbase + r, NK, stride=d) if d > 1 else pl.ds(pbase, NK)
            m_old = m_scr[rows, :]
            l_old = l_scr[rows, :]
            m_new, l_new = m_old, l_old
            for c in range(GROUP_W // LANES):
                qc = q_ref[c, rows, :].astype(BF16)
                kc_ = jnp.concatenate([kv_ref[c, prows, :], kv_ref[c, rows, :]], axis=0).astype(BF16)
                vc_ = jnp.concatenate([kv_ref[n_chunks + c, prows, :], kv_ref[n_chunks + c, rows, :]],
                                      axis=0).astype(BF16)
                acc = acc_scr[c, rows, :]
                scale, denom, contrib = None, None, None
                for hh in range(hpc):
                    h = c * hpc + hh
                    sel_q = (lane >= hh * HEAD_DIM) & (lane < (hh + 1) * HEAD_DIM)
                    sel_v = (lane_kv >= hh * HEAD_DIM) & (lane_kv < (hh + 1) * HEAD_DIM)
                    s = lax.dot_general(jnp.where(sel_q, qc, 0), kc_, (((1,), (1,)), ((), ())),
                                        preferred_element_type=F32)
                    s = jnp.where(mask, s, NEG_INF)
                    mo = m_old[:, h:h + 1]
                    mn = jnp.maximum(mo, jnp.max(s, axis=-1, keepdims=True))
                    p = jnp.exp(s - mn)
                    alpha = jnp.exp(mo - mn)
                    pv = _dot(p.astype(BF16), jnp.concatenate([jnp.where(sel_v, vc_, 0), ones_kv], axis=1))
                    ln = alpha * l_old[:, h:h + 1] + pv[:, LANES:]
                    contrib = pv[:, :LANES] if contrib is None else contrib + pv[:, :LANES]
                    scale = alpha if scale is None else jnp.where(sel_q, alpha, scale)
                    denom = ln if denom is None else jnp.where(sel_q, ln, denom)
                    m_new = jnp.where(lane == h, mn, m_new)
                    l_new = jnp.where(lane == h, ln, l_new)
                acc = scale * acc + contrib
                if last:
                    o_ref[c, rows, :] = acc / denom
                else:
                    acc_scr[c, rows, :] = acc
            if not last:
                m_scr[rows, :] = m_new
                l_scr[rows, :] = l_new

    for gi, (window, d) in enumerate(DIL_GROUPS):
        @pl.when(g == gi)
        def _(d=d, last=(gi == N_GROUPS - 1)):
            def body(blk, carry):
                block(blk, d, last)
                return carry
            n_blocks = seq // (NK * d)
            lax.fori_loop(0, n_blocks, body, 0, unroll=max(1, min(n_blocks, ATTN_BLOCKS_PER_BODY // d)))


def _attn_prompt(q, kv, batch, seq):
    assert seq % (NK * max(d for _, d in DIL_GROUPS)) == 0
    return pl.pallas_call(
        _attn_prompt_kernel,
        grid=(batch, N_GROUPS),
        in_specs=[
            pl.BlockSpec((GROUP_W // LANES, seq, LANES), lambda b, g: (g, b, 0)),
            pl.BlockSpec((2 * GROUP_W // LANES, seq, LANES), lambda b, g: (g, b, 0)),
        ],
        out_specs=pl.BlockSpec((GROUP_W // LANES, seq, LANES), lambda b, g: (0, b, 0)),
        out_shape=jax.ShapeDtypeStruct((GROUP_W // LANES, batch * seq, LANES), F32),
        scratch_shapes=[
            pltpu.VMEM((seq, LANES), F32),
            pltpu.VMEM((seq, LANES), F32),
            pltpu.VMEM((GROUP_W // LANES, seq, LANES), F32),
        ],
        compiler_params=_params("arbitrary", "arbitrary", vmem_limit=ATTN_VMEM_LIMIT),
        name="attn_prompt",
    )(q, kv)


def _mix_sample_kernel(q_ref, kn_ref, vn_ref, gbu_ref, st_ref, c0_ref, c1_ref, c2_ref, w_ref,
                       yb_ref, ya_ref, ns_ref):
    sb, t_len = q_ref.shape[0], q_ref.shape[3]
    w = w_ref[...]
    caches = (c0_ref, c1_ref, c2_ref)
    nt_dims = (((1,), (1,)), ((), ()))
    qn = lax.broadcasted_iota(jnp.int32, (t_len, t_len), 0)
    tn = lax.broadcasted_iota(jnp.int32, (t_len, t_len), 1)

    for s in range(sb):
        for h in range(HEADS):
            sc_parts, v_parts = [], []
            for gi, (window, d) in enumerate(DIL_GROUPS):
                c_ref = caches[gi]
                n_pos = c_ref.shape[-1]
                qg = q_ref[s, gi, h].astype(BF16)
                qc = lax.broadcasted_iota(jnp.int32, (t_len, n_pos), 0)
                pc = lax.broadcasted_iota(jnp.int32, (t_len, n_pos), 1)
                sc = _dot(qg, c_ref[s, 0, h].astype(BF16))
                sc_parts.append(jnp.where((((pc - qc) & (d - 1)) == 0) & (pc >= qc), sc, NEG_INF))
                v_parts.append((c_ref[s, 1, h].astype(BF16), True))
                sn = lax.dot_general(qg, kn_ref[s, gi, h].astype(BF16), nt_dims, preferred_element_type=F32)
                sc_parts.append(jnp.where((tn <= qn) & (((qn - tn) & (d - 1)) == 0), sn, NEG_INF))
                v_parts.append((vn_ref[s, gi, h].astype(BF16), False))
            mx = functools.reduce(jnp.maximum, [jnp.max(p, axis=-1, keepdims=True) for p in sc_parts])
            den = jnp.zeros((t_len, 1), F32)
            acc = jnp.zeros((t_len, HEAD_DIM), F32)
            for sc, (v, transposed) in zip(sc_parts, v_parts):
                p = jnp.exp(sc - mx)
                den = den + jnp.sum(p, axis=-1, keepdims=True)
                pb = p.astype(BF16)
                acc = acc + (lax.dot_general(pb, v, nt_dims, preferred_element_type=F32) if transposed
                             else _dot(pb, v))
            yb_ref[s, h] = acc / den
        gbu = gbu_ref[s]
        gb = gbu[:, 0:GROUP_W]
        u = gbu[:, GROUP_W:]
        st = st_ref[s]
        ext = [st[0:1], st[1:2]] + [u[t:t + 1] for t in range(t_len)]
        for t in range(t_len):
            conv = w[0:1] * ext[t] + w[1:2] * ext[t + 1] + w[2:3] * ext[t + 2]
            ya_ref[s, t:t + 1, :] = gb[t:t + 1] * conv
        for r in range(A_KERNEL - 1):
            ns_ref[s, r:r + 1, :] = ext[t_len + r]


def _mix_sample(q5, kn, vn, gbu3, st, c0, c1, c2, w):
    n, t_len = q5.shape[0], q5.shape[3]
    sb = ATTN_SEQ_BLOCK
    assert t_len >= A_KERNEL - 1 and n % sb == 0
    for c, (window, d) in zip((c0, c1, c2), DIL_GROUPS):
        assert c.shape[-1] == NK * d and d & (d - 1) == 0
    qspec = pl.BlockSpec((sb, N_GROUPS, HEADS, t_len, HEAD_DIM), lambda i: (i, 0, 0, 0, 0))
    cspec = lambda c: pl.BlockSpec((sb,) + c.shape[1:], lambda i: (i, 0, 0, 0, 0))
    return pl.pallas_call(
        _mix_sample_kernel,
        grid=(n // sb,),
        in_specs=[
            qspec, qspec, qspec,
            pl.BlockSpec((sb, t_len, 2 * GROUP_W), lambda i: (i, 0, 0)),
            pl.BlockSpec((sb, A_KERNEL - 1, GROUP_W), lambda i: (i, 0, 0)),
            cspec(c0), cspec(c1), cspec(c2),
            pl.BlockSpec((A_KERNEL, GROUP_W), lambda i: (0, 0)),
        ],
        out_specs=[
            pl.BlockSpec((sb, HEADS, t_len, HEAD_DIM), lambda i: (i, 0, 0, 0)),
            pl.BlockSpec((sb, t_len, GROUP_W), lambda i: (i, 0, 0)),
            pl.BlockSpec((sb, A_KERNEL - 1, GROUP_W), lambda i: (i, 0, 0)),
        ],
        out_shape=[
            jax.ShapeDtypeStruct((n, HEADS, t_len, HEAD_DIM), F32),
            jax.ShapeDtypeStruct((n, t_len, GROUP_W), F32),
            jax.ShapeDtypeStruct((n, A_KERNEL - 1, GROUP_W), F32),
        ],
        compiler_params=_params("arbitrary"),
        name="mix_sample",
    )(q5, kn, vn, gbu3, st, c0, c1, c2, w)


def _route(h2, wr_ref, br_ref, xs_ref, rinfo_ref, gtab_ref):
    tm = h2.shape[0]
    h_hi = h2.astype(BF16)
    h_lo = (h2 - h_hi.astype(F32)).astype(BF16)
    nt_dims = (((1,), (1,)), ((), ()))
    w_hi, w_lo = wr_ref[0], wr_ref[1]
    lt = (lax.dot_general(w_hi, h_hi, nt_dims, preferred_element_type=F32)
          + lax.dot_general(w_hi, h_lo, nt_dims, preferred_element_type=F32)
          + lax.dot_general(w_lo, h_hi, nt_dims, preferred_element_type=F32)) + br_ref[...]
    rg = lax.broadcasted_iota(jnp.int32, (SUBLANES, tm), 0)
    gl = jnp.where(rg < N_EXPERT_GROUPS, lt[0:SUBLANES], NEG_INF)
    ge = jnp.exp(gl - jnp.max(gl, axis=0, keepdims=True))
    gp = ge / jnp.sum(ge, axis=0, keepdims=True)
    g_val = jnp.max(gp, axis=0, keepdims=True)
    g_idx = jnp.min(jnp.where(gp == g_val, rg, SUBLANES), axis=0, keepdims=True)
    re = lax.broadcasted_iota(jnp.int32, (N_EXPERTS, tm), 0)
    sel = (re // EXPERTS_PER_GROUP) == g_idx
    el = jnp.where(sel, lt[ROUTER_EXPERT_ROW0:ROUTER_EXPERT_ROW0 + N_EXPERTS], NEG_INF)
    ee = jnp.exp(el - jnp.max(el, axis=0, keepdims=True))
    ep = jnp.where(sel, ee / jnp.sum(ee, axis=0, keepdims=True), -1.0)
    v1 = jnp.max(ep, axis=0, keepdims=True)
    i1 = jnp.min(jnp.where(ep == v1, re, N_EXPERTS), axis=0, keepdims=True)
    ep2 = jnp.where(re == i1, -1.0, ep)
    v2 = jnp.max(ep2, axis=0, keepdims=True)
    i2 = jnp.min(jnp.where(ep2 == v2, re, N_EXPERTS), axis=0, keepdims=True)
    scale = g_val / (v1 + v2)

    oh = [(re == i1).astype(F32), (re == i2).astype(F32)]
    cmat = (oh[0] + oh[1]).astype(BF16)
    earlier = (lax.broadcasted_iota(jnp.int32, (tm, tm), 0)
               < lax.broadcasted_iota(jnp.int32, (tm, tm), 1)).astype(BF16)
    before = _dot(cmat, earlier)
    cnt_row = lax.dot_general(jnp.ones((SUBLANES, tm), BF16), cmat, nt_dims, preferred_element_type=F32)
    grp_row = ((cnt_row.astype(jnp.int32) + (SUBLANES - 1)) // SUBLANES).astype(F32)
    lower = (lax.broadcasted_iota(jnp.int32, (N_EXPERTS, N_EXPERTS), 0)
             < lax.broadcasted_iota(jnp.int32, (N_EXPERTS, N_EXPERTS), 1)).astype(BF16)
    goff_row = _dot(grp_row.astype(BF16), lower)
    slot_rows = lax.broadcasted_iota(jnp.int32, (TILE_SLOTS, tm), 0)
    place = None
    pos = []
    for k in range(2):
        rank = jnp.sum(oh[k] * before, axis=0, keepdims=True)
        seg = _dot(goff_row.astype(BF16), oh[k].astype(BF16))[0:1]
        pos.append(seg * SUBLANES + rank)
        hit = slot_rows == pos[k].astype(jnp.int32)
        place = hit if place is None else place | hit
    xs_ref[...] = _dot(jnp.where(place, 1.0, 0.0).astype(BF16), h_hi)
    rl = lax.broadcasted_iota(jnp.int32, (LANES, tm), 0)
    info_t = jnp.where(rl == 0, v1 * scale, jnp.where(rl == 1, v2 * scale,
                       jnp.where(rl == 2, pos[0], jnp.where(rl == 3, pos[1], 0.0))))
    rinfo_ref[...] = info_t.T
    spread = (lax.broadcasted_iota(jnp.int32, (N_EXPERTS, LANES), 0)
              == lax.broadcasted_iota(jnp.int32, (N_EXPERTS, LANES), 1)).astype(BF16)
    gtab_ref[...] = _dot(grp_row.astype(BF16), spread).astype(jnp.int32)


ROUTER_EXPERT_ROW0 = SUBLANES
ROUTER_ROWS = 48
TILE_GROUPS = -(-(2 * ROW_TILE + (SUBLANES - 1) * N_EXPERTS) // SUBLANES)
TILE_SLOTS = TILE_GROUPS * SUBLANES


def _router_specs(tm, d):
    ins = [pl.BlockSpec((1, d), lambda i: (0, 0)),
           pl.BlockSpec((2, ROUTER_ROWS, d), lambda i: (0, 0, 0)),
           pl.BlockSpec((ROUTER_ROWS, 1), lambda i: (0, 0))]
    outs = [pl.BlockSpec((TILE_SLOTS, d), lambda i: (i, 0)),
            pl.BlockSpec((tm, LANES), lambda i: (i, 0)),
            pl.BlockSpec((None, SUBLANES, LANES), lambda i: (i, 0, 0))]
    return ins, outs


def _router_shapes(m, d):
    nb = m // ROW_TILE
    return [jax.ShapeDtypeStruct((nb * TILE_SLOTS, d), F32), jax.ShapeDtypeStruct((m, LANES), F32),
            jax.ShapeDtypeStruct((nb, SUBLANES, LANES), jnp.int32)]


def _outproj_kernel(n_prompt_tiles, tiles_per_seq, xp_ref, xs_ref, gbu_ref, halo_ref, cw_ref, yb_ref, ys_ref,
                    wo_ref, g_ref, wr_ref, br_ref, x1_ref, xs_out_ref, rinfo_ref, gtab_ref, conv_scr):
    i = pl.program_id(0)
    tm = xp_ref.shape[0]
    is_p = i < n_prompt_tiles
    x = jnp.where(is_p, xp_ref[...], xs_ref[...])
    u = gbu_ref[:, GROUP_W:]
    first = (i % tiles_per_seq) == 0
    conv_scr[0:SUBLANES, :] = jnp.where(first, 0.0, halo_ref[:, GROUP_W:])
    conv_scr[SUBLANES:, :] = u
    cw = cw_ref[...]
    ya = gbu_ref[:, 0:GROUP_W] * (cw[0:1, :] * conv_scr[pl.ds(SUBLANES - 2, tm), :]
                                  + cw[1:2, :] * conv_scr[pl.ds(SUBLANES - 1, tm), :] + cw[2:3, :] * u)
    ymix = jnp.where(is_p, jnp.concatenate([ya, _load_chunks(yb_ref)], axis=-1), ys_ref[...])
    x1 = x + _dot(ymix.astype(BF16), wo_ref[...])
    x1_ref[...] = x1
    _route(_rms(x1, g_ref[...]), wr_ref, br_ref, xs_out_ref, rinfo_ref, gtab_ref)


def _outproj(xp, xs, gbu, conv_w, yb_p, ymix_s, wo, g, wr, br, seq):
    mp, d = xp.shape
    ms = xs.shape[0]
    tm = ROW_TILE
    npt, nst = mp // tm, ms // tm
    m = mp + ms
    r_in, r_out = _router_specs(tm, d)
    pmap = lambda i: (jnp.minimum(i, npt - 1), 0)
    smap = lambda i: (jnp.maximum(i - npt, 0), 0)
    halo_map = lambda i: (jnp.maximum(jnp.minimum(i, npt - 1) * (tm // SUBLANES) - 1, 0), 0)
    return pl.pallas_call(
        functools.partial(_outproj_kernel, npt, seq // tm),
        grid=(npt + nst,),
        in_specs=[
            pl.BlockSpec((tm, d), pmap),
            pl.BlockSpec((tm, d), smap),
            pl.BlockSpec((tm, 2 * GROUP_W), pmap),
            pl.BlockSpec((SUBLANES, 2 * GROUP_W), halo_map),
            pl.BlockSpec((A_KERNEL, GROUP_W), lambda i: (0, 0)),
            pl.BlockSpec((GROUP_W // LANES, tm, LANES), lambda i: (0, jnp.minimum(i, npt - 1), 0)),
            pl.BlockSpec((tm, 2 * GROUP_W), smap),
            pl.BlockSpec((2 * GROUP_W, d), lambda i: (0, 0)),
        ] + r_in,
        out_specs=[pl.BlockSpec((tm, d), lambda i: (i, 0))] + r_out,
        out_shape=[jax.ShapeDtypeStruct((m, d), F32)] + _router_shapes(m, d),
        scratch_shapes=[pltpu.VMEM((tm + SUBLANES, GROUP_W), F32)],
        compiler_params=_params("arbitrary"),
        name="outproj_router",
    )(xp, xs, gbu, gbu, conv_w, yb_p, ymix_s, wo, g, wr, br)


def _experts_kernel(te_ref, src_ref, dst_ref, nt_ref, h_hbm, wg_ref, wu_ref, wd_ref, y_hbm, xbuf, ybuf, gsem, ssem,
                    *, trash_row):
    i = pl.program_id(0)
    nt = nt_ref[0]
    ng = CHUNK_GROUPS
    slot = i % 2
    other = 1 - slot

    def group_rows(r):
        return pl.ds(r if isinstance(r, int) else pl.multiple_of(r, SUBLANES), SUBLANES)

    def gather_copy(p, sl, j):
        return pltpu.make_async_copy(h_hbm.at[group_rows(src_ref[p]), :],
                                     xbuf.at[sl, group_rows(j * SUBLANES), :], gsem.at[sl])

    def scatter_copy(p, sl, j):
        return pltpu.make_async_copy(ybuf.at[sl, group_rows(j * SUBLANES), :],
                                     y_hbm.at[group_rows(dst_ref[p]), :], ssem.at[sl])

    def wait_gather(sl):
        pltpu.make_async_copy(h_hbm.at[pl.ds(0, ng * SUBLANES), :], xbuf.at[sl], gsem.at[sl]).wait()

    def wait_scatter(sl):
        pltpu.make_async_copy(ybuf.at[sl], y_hbm.at[pl.ds(0, ng * SUBLANES), :], ssem.at[sl]).wait()

    @pl.when(i == 0)
    def _():
        ybuf[...] = jnp.zeros(ybuf.shape, F32)
        init = pltpu.make_async_copy(ybuf.at[0], y_hbm.at[pl.ds(trash_row, ng * SUBLANES), :], ssem.at[0])
        init.start()
        init.wait()

        def tail_copy(b):
            first = pl.multiple_of(b * TILE_SLOTS + 2 * ROW_TILE, SUBLANES)
            return pltpu.make_async_copy(ybuf.at[0, pl.ds(0, TILE_SLOTS - 2 * ROW_TILE), :],
                                         y_hbm.at[pl.ds(first, TILE_SLOTS - 2 * ROW_TILE), :], ssem.at[0])

        def start_body(b, c):
            tail_copy(b).start()
            return c

        def wait_body(b, c):
            tail_copy(b).wait()
            return c
        lax.fori_loop(0, trash_row // TILE_SLOTS, start_body, 0)
        lax.fori_loop(0, trash_row // TILE_SLOTS, wait_body, 0)
        for j in range(ng):
            gather_copy(j, 0, j).start()

    @pl.when(i < nt)
    def _():
        wait_gather(slot)

        @pl.when(i >= 1)
        def _():
            wait_scatter(slot)

        base = i * ng
        for j in range(ng):
            gather_copy(base + ng + j, other, j).start(priority=j % 2)
            scatter_copy(base + j, other, j).start(priority=(j + 1) % 2)
        x = xbuf[slot].astype(BF16)
        gate = _dot(x, wg_ref[...].astype(BF16))
        up = _dot(x, wu_ref[...].astype(BF16))
        act = (gate * jax.nn.sigmoid(gate) * up).astype(BF16)
        ybuf[slot] = _dot(act, wd_ref[...].astype(BF16))

    @pl.when(i == nt)
    def _():
        wait_gather(slot)
        wait_scatter(slot)
        for j in range(ng):
            scatter_copy(i * ng + j, other, j).start()
        wait_scatter(other)


def _experts(xs, tile_expert, src, dst, n_tiles, wg, wu, wd, layer):
    ng = CHUNK_GROUPS
    nt_max = tile_expert.shape[0]
    d, f = wg.shape[2], wg.shape[3]
    n_rows = xs.shape[0] + ng * SUBLANES
    wmap = lambda i, te_ref, src_ref, dst_ref, nt_ref: (layer, te_ref[i], 0, 0)
    grid_spec = pltpu.PrefetchScalarGridSpec(
        num_scalar_prefetch=4,
        grid=(nt_max,),
        in_specs=[
            pl.BlockSpec(memory_space=pl.ANY),
            pl.BlockSpec((None, None, d, f), wmap),
            pl.BlockSpec((None, None, d, f), wmap),
            pl.BlockSpec((None, None, f, d), wmap),
        ],
        out_specs=pl.BlockSpec(memory_space=pl.ANY),
        scratch_shapes=[
            pltpu.VMEM((2, ng * SUBLANES, d), F32),
            pltpu.VMEM((2, ng * SUBLANES, d), F32),
            pltpu.SemaphoreType.DMA((2,)),
            pltpu.SemaphoreType.DMA((2,)),
        ],
    )
    return pl.pallas_call(
        functools.partial(_experts_kernel, trash_row=xs.shape[0]),
        grid_spec=grid_spec,
        out_shape=jax.ShapeDtypeStruct((n_rows, d), F32),
        compiler_params=_params("arbitrary"),
        name="experts",
    )(tile_expert, src, dst, n_tiles, xs, wg, wu, wd)


def _plan_chunks(gtab):
    ng = CHUNK_GROUPS
    grp = gtab[:, 0, :N_EXPERTS]
    nb = grp.shape[0]
    nt_max = -(-nb * TILE_GROUPS // ng) + N_EXPERTS + 1
    total = jnp.sum(grp, axis=0)
    chunks = (total + ng - 1) // ng
    chunk_end = jnp.cumsum(chunks)
    chunk_start = chunk_end - chunks
    n_chunks = chunk_end[-1]
    seg_off = jnp.cumsum(grp, axis=1) - grp
    seg_end = jnp.cumsum(grp, axis=0)
    seg_start = seg_end - grp
    cidx = jnp.arange(nt_max, dtype=jnp.int32)
    chunk_expert = jnp.sum((cidx[:, None] >= chunk_end[None, :]).astype(jnp.int32), axis=1)
    last_expert = jnp.max(jnp.where(total > 0, jnp.arange(N_EXPERTS), 0))
    chunk_expert = jnp.where(cidx < n_chunks, jnp.minimum(chunk_expert, N_EXPERTS - 1), last_expert).astype(jnp.int32)
    is_e = chunk_expert[:, None] == jnp.arange(N_EXPERTS, dtype=jnp.int32)[None, :]
    pick = lambda per_expert: jnp.sum(jnp.where(is_e, per_expert[None, :], 0), axis=1)
    pick_rows = lambda table: jnp.sum(jnp.where(is_e[:, :, None], table.T[None, :, :], 0), axis=1)
    within = (cidx - pick(chunk_start))[:, None] * ng + jnp.arange(ng, dtype=jnp.int32)[None, :]
    valid = (within < pick(total)[:, None]) & (cidx < n_chunks)[:, None]
    ends = pick_rows(seg_end)
    base = pick_rows(jnp.arange(nb, dtype=jnp.int32)[:, None] * TILE_GROUPS + seg_off - seg_start)
    step = jnp.concatenate([base[:, :1], base[:, 1:] - base[:, :-1]], axis=1)
    passed = jnp.concatenate([jnp.ones((nt_max, ng, 1), bool), within[:, :, None] >= ends[:, None, :-1]], axis=2)
    group = within + jnp.sum(jnp.where(passed, step[:, None, :], 0), axis=2)
    pad_dst = (nb * TILE_GROUPS + jnp.arange(ng, dtype=jnp.int32)) * SUBLANES
    src = jnp.where(valid, group * SUBLANES, 0).astype(jnp.int32).reshape(nt_max * ng)
    dst = jnp.where(valid, group * SUBLANES, pad_dst[None, :]).astype(jnp.int32).reshape(nt_max * ng)
    dst = jnp.concatenate([pad_dst, dst])
    return chunk_expert, src, dst, n_chunks.astype(jnp.int32).reshape(1)


def _combine(x_ref, y_ref, rinfo_ref):
    tm = x_ref.shape[0]
    n_slots = y_ref.shape[0]
    r = rinfo_ref[...]
    col = lax.broadcasted_iota(jnp.int32, (tm, n_slots), 1)
    pick = (jnp.where(col == r[:, 2:3].astype(jnp.int32), r[:, 0:1], 0.0)
            + jnp.where(col == r[:, 3:4].astype(jnp.int32), r[:, 1:2], 0.0))
    return x_ref[...] + _dot(pick.astype(BF16), y_ref[...].astype(BF16))


def _combine_specs(tm, d):
    return [
        pl.BlockSpec((tm, d), lambda i: (i, 0)),
        pl.BlockSpec((TILE_SLOTS, d), lambda i: (i, 0)),
        pl.BlockSpec((tm, LANES), lambda i: (i, 0)),
    ]


def _combine_pw1_kernel(x_ref, y_ref, rinfo_ref, g_ref, w_ref, b_ref, x2_ref, u_ref):
    x2 = _combine(x_ref, y_ref, rinfo_ref)
    x2_ref[...] = x2
    d = x2.shape[1]
    z = _dot(_rms(x2, g_ref[...]).astype(BF16), w_ref[...]) + b_ref[...]
    u_ref[...] = z[:, :d] * jax.nn.sigmoid(z[:, d:])


def _combine_pw1(x1, y2, rinfo, g, w, b):
    m, d = x1.shape
    tm = ROW_TILE
    return pl.pallas_call(
        _combine_pw1_kernel,
        grid=(m // tm,),
        in_specs=_combine_specs(tm, d) + [
            pl.BlockSpec((1, d), lambda i: (0, 0)),
            pl.BlockSpec((d, 2 * d), lambda i: (0, 0)),
            pl.BlockSpec((1, 2 * d), lambda i: (0, 0)),
        ],
        out_specs=[pl.BlockSpec((tm, d), lambda i: (i, 0)), pl.BlockSpec((tm, d), lambda i: (i, 0))],
        out_shape=[jax.ShapeDtypeStruct((m, d), F32), jax.ShapeDtypeStruct((m, d), F32)],
        compiler_params=_params("arbitrary"),
        name="combine_pw1",
    )(x1, y2, rinfo, g, w, b)


def _combine_final_kernel(n_prompt_tiles, x_ref, y_ref, rinfo_ref, g_ref, yp_ref, ys_ref):
    i = pl.program_id(0)
    y = _rms(_combine(x_ref, y_ref, rinfo_ref), g_ref[...])

    @pl.when(i < n_prompt_tiles)
    def _():
        yp_ref[...] = y

    @pl.when(i >= n_prompt_tiles)
    def _():
        ys_ref[...] = y


def _combine_final(x3, y2, rinfo, g, mp):
    m, d = x3.shape
    tm = ROW_TILE
    npt = mp // tm
    return pl.pallas_call(
        functools.partial(_combine_final_kernel, npt),
        grid=(m // tm,),
        in_specs=_combine_specs(tm, d) + [pl.BlockSpec((1, d), lambda i: (0, 0))],
        out_specs=[pl.BlockSpec((tm, d), lambda i: (jnp.minimum(i, npt - 1), 0)),
                   pl.BlockSpec((tm, d), lambda i: (jnp.maximum(i - npt, 0), 0))],
        out_shape=[jax.ShapeDtypeStruct((mp, d), F32), jax.ShapeDtypeStruct((m - mp, d), F32)],
        compiler_params=_params("arbitrary"),
        name="combine_final",
    )(x3, y2, rinfo, g)


def _dwconv_prompt_kernel(tiles_per_seq, u_ref, halo_ref, w_ref, b_ref, c_ref, scr):
    i = pl.program_id(0)
    tm = u_ref.shape[0]
    first = (i % tiles_per_seq) == 0
    scr[0, 0:CONV_HALO, :] = jnp.where(first, 0.0, halo_ref[...])
    scr[0, CONV_HALO:, :] = u_ref[...]
    n_shift = tm + CONV_HALO - SUBLANES
    for s in range(1, SUBLANES):
        scr[s, 0:n_shift, :] = scr[0, pl.ds(s, n_shift), :]
    acc = jnp.broadcast_to(b_ref[...], c_ref.shape)
    for k in range(C_KERNEL):
        off = CONV_HALO - (C_KERNEL - 1) + k
        acc = acc + w_ref[k:k + 1, :] * scr[off % SUBLANES, pl.ds(off - off % SUBLANES, tm), :]
    c_ref[...] = acc


def _dwconv_prompt(u, w, b, mp, seq):
    tm = ROW_TILE
    d = u.shape[1]
    return pl.pallas_call(
        functools.partial(_dwconv_prompt_kernel, seq // tm),
        grid=(mp // tm,),
        in_specs=[
            pl.BlockSpec((tm, d), lambda i: (i, 0)),
            pl.BlockSpec((CONV_HALO, d), lambda i: (jnp.maximum(i * (tm // CONV_HALO) - 1, 0), 0)),
            pl.BlockSpec((C_KERNEL, d), lambda i: (0, 0)),
            pl.BlockSpec((1, d), lambda i: (0, 0)),
        ],
        out_specs=pl.BlockSpec((tm, d), lambda i: (i, 0)),
        out_shape=jax.ShapeDtypeStruct((mp, d), F32),
        scratch_shapes=[pltpu.VMEM((SUBLANES, tm + CONV_HALO, d), F32)],
        compiler_params=_params("arbitrary"),
        name="dwconv_prompt",
    )(u, u, w, b)


def _dwconv_sample_kernel(st_ref, u_ref, w_ref, b_ref, c_ref, ns_ref):
    n_state, t_len = st_ref.shape[0], u_ref.shape[0]
    rows = [st_ref[r] for r in range(n_state)] + [u_ref[t] for t in range(t_len)]
    for t in range(t_len):
        acc = jnp.broadcast_to(b_ref[...], rows[0].shape)
        for k in range(C_KERNEL):
            acc = acc + w_ref[k:k + 1, :] * rows[t + k]
        c_ref[t] = acc
    for r in range(n_state):
        ns_ref[r] = rows[r + t_len]


def _dwconv_sample(st, u3, w, b):
    n_state, n, d = st.shape
    t_len = u3.shape[0]
    sb = SEQ_BLOCK
    assert n_state == C_KERNEL - 1 and n % sb == 0
    return pl.pallas_call(
        _dwconv_sample_kernel,
        grid=(n // sb,),
        in_specs=[
            pl.BlockSpec((n_state, sb, d), lambda i: (0, i, 0)),
            pl.BlockSpec((t_len, sb, d), lambda i: (0, i, 0)),
            pl.BlockSpec((C_KERNEL, d), lambda i: (0, 0)),
            pl.BlockSpec((1, d), lambda i: (0, 0)),
        ],
        out_specs=[
            pl.BlockSpec((t_len, sb, d), lambda i: (0, i, 0)),
            pl.BlockSpec((n_state, sb, d), lambda i: (0, i, 0)),
        ],
        out_shape=[jax.ShapeDtypeStruct((t_len, n, d), F32), jax.ShapeDtypeStruct((n_state, n, d), F32)],
        compiler_params=_params("arbitrary"),
        name="dwconv_sample",
    )(st, u3, w, b)


def _conf_tail_kernel(n_prompt_tiles, x_ref, cp_ref, cs_ref, lg_ref, lb_ref, w_ref, b_ref, g_ref, wr_ref, br_ref,
                      x3_ref, xs_out_ref, rinfo_ref, gtab_ref):
    i = pl.program_id(0)
    c = jnp.where(i < n_prompt_tiles, cp_ref[...], cs_ref[...])
    mu = jnp.mean(c, axis=-1, keepdims=True)
    cc = c - mu
    var = jnp.mean(cc * cc, axis=-1, keepdims=True)
    y = cc * lax.rsqrt(var + LN_EPS) * lg_ref[...] + lb_ref[...]
    y = y * jax.nn.sigmoid(y)
    x3 = x_ref[...] + _dot(y.astype(BF16), w_ref[...]) + b_ref[...]
    x3_ref[...] = x3
    _route(_rms(x3, g_ref[...]), wr_ref, br_ref, xs_out_ref, rinfo_ref, gtab_ref)


def _conf_tail(x2, c_p, c_s, ln_g, ln_b, w, b, g, wr, br):
    m, d = x2.shape
    tm = ROW_TILE
    npt = c_p.shape[0] // tm
    r_in, r_out = _router_specs(tm, d)
    vec = pl.BlockSpec((1, d), lambda i: (0, 0))
    return pl.pallas_call(
        functools.partial(_conf_tail_kernel, npt),
        grid=(m // tm,),
        in_specs=[
            pl.BlockSpec((tm, d), lambda i: (i, 0)),
            pl.BlockSpec((tm, d), lambda i: (jnp.minimum(i, npt - 1), 0)),
            pl.BlockSpec((tm, d), lambda i: (jnp.maximum(i - npt, 0), 0)),
            vec, vec,
            pl.BlockSpec((d, d), lambda i: (0, 0)),
            vec,
        ] + r_in,
        out_specs=[pl.BlockSpec((tm, d), lambda i: (i, 0))] + r_out,
        out_shape=[jax.ShapeDtypeStruct((m, d), F32)] + _router_shapes(m, d),
        compiler_params=_params("arbitrary"),
        name="conf_tail_router",
    )(x2, c_p, c_s, ln_g, ln_b, w, b, g, wr, br)


def _router_weights(w_rg, b_rg, w_re, b_re):
    d = w_rg.shape[0]
    gpad = ROUTER_EXPERT_ROW0 - N_EXPERT_GROUPS
    epad = ROUTER_ROWS - ROUTER_EXPERT_ROW0 - N_EXPERTS
    wt = jnp.concatenate([w_rg.T, jnp.zeros((gpad, d), F32), w_re.T, jnp.zeros((epad, d), F32)], axis=0)
    w_hi = wt.astype(BF16)
    w_lo = (wt - w_hi.astype(F32)).astype(BF16)
    br = jnp.concatenate([b_rg, jnp.zeros((gpad,), F32), b_re, jnp.zeros((epad,), F32)])[:, None]
    return jnp.stack([w_hi, w_lo]), br


def _moe(xs, gtab, wg, wu, wd, layer):
    chunk_expert, src, dst, n_chunks = _plan_chunks(gtab)
    return _experts(xs, chunk_expert, src, dst, n_chunks, wg, wu, wd, layer)


def kernel(x_prompt, x_sample, state_shortconv, cache_kv_w128, cache_kv_w512, cache_kv_w2048, state_conformer,
           g_mix, g_ffn, g_final, w_in, conv_a_w, w_out, w_pw1, b_pw1, dw_w, dw_b, ln_g, ln_b, w_pw2, b_pw2,
           w_router_group, b_router_group, w_router_expert, b_router_expert, w_gate, w_up, w_down):
    batch, seq, d = x_prompt.shape
    n_dec, t_dec, _ = x_sample.shape
    mp, ms = batch * seq, n_dec * t_dec
    assert g_mix.shape[0] == 2 and mp % ROW_TILE == 0 and ms % ROW_TILE == 0 and seq % ROW_TILE == 0
    xp = x_prompt.reshape(mp, d)
    xs = x_sample.reshape(ms, d)
    slab = (HEADS, HEAD_DIM)

    aw, qw = 3 * GROUP_W, N_GROUPS * GROUP_W
    w0 = w_in[0]
    kcols = w0[:, aw + qw:aw + 2 * qw].reshape(d, N_GROUPS, GROUP_W)
    vcols = w0[:, aw + 2 * qw:].reshape(d, N_GROUPS, GROUP_W)
    w_perm = jnp.concatenate([w0[:, :aw + qw], jnp.stack([kcols, vcols], axis=2).reshape(d, 2 * qw)], axis=1)
    q, kv, gbu = _inproj(xp, xs, g_mix[0][None, :], w_perm.astype(BF16))

    yb_p = _attn_prompt(q, kv, batch, seq)

    hpc = LANES // HEAD_DIM
    q_s = jnp.transpose(q[:, mp:].reshape(N_GROUPS, HEADS // hpc, n_dec, t_dec, hpc, HEAD_DIM),
                        (2, 0, 1, 4, 3, 5)).reshape(n_dec, N_GROUPS, HEADS, t_dec, HEAD_DIM)
    kv_s = jnp.transpose(kv[:, mp:].reshape(N_GROUPS, 2, HEADS // hpc, n_dec, t_dec, hpc, HEAD_DIM),
                         (1, 3, 0, 2, 5, 4, 6)).reshape(2, n_dec, N_GROUPS, HEADS, t_dec, HEAD_DIM)
    gbu_s = gbu[mp:].reshape(n_dec, t_dec, 2 * GROUP_W)
    c0, c1, c2 = (jnp.transpose(c[0], (0, 2, 3, 4, 1)) for c in (cache_kv_w128, cache_kv_w512, cache_kv_w2048))
    yb_s, ya_s, s_sc = _mix_sample(q_s, kv_s[0], kv_s[1], gbu_s, state_shortconv[0], c0, c1, c2, conv_a_w[0])
    ymix_s = jnp.concatenate([ya_s.reshape(ms, GROUP_W),
                              jnp.transpose(yb_s, (0, 2, 1, 3)).reshape(ms, GROUP_W)], axis=-1)

    wr0, br0 = _router_weights(w_router_group[0], b_router_group[0], w_router_expert[0], b_router_expert[0])
    x1, xsort0, rinfo0, gtab0 = _outproj(xp, xs, gbu, conv_a_w[0], yb_p, ymix_s, w_out[0].astype(BF16),
                                         g_ffn[0][None, :], wr0, br0, seq)
    y2 = _moe(xsort0, gtab0, w_gate, w_up, w_down, 0)

    x2, u = _combine_pw1(x1, y2, rinfo0, g_mix[1][None, :], w_pw1[0].astype(BF16), b_pw1[0][None, :])
    c_p = _dwconv_prompt(u, dw_w[0], dw_b[0][None, :], mp, seq)
    c_s3, s_cf = _dwconv_sample(jnp.transpose(state_conformer[0], (1, 0, 2)),
                                jnp.transpose(u[mp:].reshape(n_dec, t_dec, d), (1, 0, 2)), dw_w[0], dw_b[0][None, :])
    c_s = jnp.transpose(c_s3, (1, 0, 2)).reshape(ms, d)
    wr1, br1 = _router_weights(w_router_group[1], b_router_group[1], w_router_expert[1], b_router_expert[1])
    x3, xsort1, rinfo1, gtab1 = _conf_tail(x2, c_p, c_s, ln_g[0][None, :], ln_b[0][None, :],
                                           w_pw2[0].astype(BF16), b_pw2[0][None, :], g_ffn[1][None, :], wr1, br1)
    y2b = _moe(xsort1, gtab1, w_gate, w_up, w_down, 1)
    y_p, y_s = _combine_final(x3, y2b, rinfo1, g_final[None, :], mp)

    cpg = 2 * GROUP_W // LANES

    def tail_rows(arr, n, axis):
        return jnp.stack([lax.slice_in_dim(arr, (b + 1) * seq - n, (b + 1) * seq, axis=axis) for b in range(batch)])

    p_kv = []
    for gi, (w, _) in enumerate(DIL_GROUPS):
        rows = tail_rows(kv[gi * cpg:(gi + 1) * cpg], min(w, seq), 1)
        p_kv.append(jnp.transpose(rows, (0, 2, 1, 3)).reshape(1, batch, min(w, seq), 2, *slab))
    s_kv = [jnp.transpose(kv_s[:, :, gi], (1, 3, 0, 2, 4))[None] for gi in range(N_GROUPS)]
    p_sc = tail_rows(gbu, A_KERNEL - 1, 0)[:, :, GROUP_W:][None]
    p_cf = tail_rows(u, C_KERNEL - 1, 0)[None]
    return (y_p.reshape(batch, seq, d), y_s.reshape(n_dec, t_dec, d), p_sc, p_kv[0], p_kv[1], p_kv[2], p_cf,
            s_sc[None], s_kv[0], s_kv[1], s_kv[2], jnp.transpose(s_cf, (1, 0, 2))[None])
```

```python
import functools

import jax
import jax.numpy as jnp
from jax import lax
from jax.experimental import pallas as pl
from jax.experimental.pallas import tpu as pltpu

F32 = jnp.float32
BF16 = jnp.bfloat16

DIL_GROUPS = ((128, 1), (512, 4), (2048, 16))
N_GROUPS = len(DIL_GROUPS)
HEADS = 4
HEAD_DIM = 64
GROUP_W = HEADS * HEAD_DIM
NK = DIL_GROUPS[0][0] // DIL_GROUPS[0][1]
A_KERNEL = 3
C_KERNEL = 31
N_EXPERT_GROUPS = 4
EXPERTS_PER_GROUP = 8
N_EXPERTS = N_EXPERT_GROUPS * EXPERTS_PER_GROUP
RMS_EPS = 1e-6
LN_EPS = 1e-5
NEG_INF = -1e30

SUBLANES = 8
LANES = 128
VMEM_LIMIT = 48 * 1024 * 1024
ATTN_VMEM_LIMIT = 60 * 1024 * 1024
ATTN_BLOCKS_PER_BODY = 4

ROW_TILE = 256
CHUNK_GROUPS = 64
CONV_HALO = 32
SEQ_BLOCK = 8
ATTN_SEQ_BLOCK = 2


def _params(*sem, vmem_limit=VMEM_LIMIT):
    return pltpu.CompilerParams(dimension_semantics=sem, vmem_limit_bytes=vmem_limit)


def _rms(x, g):
    return x * lax.rsqrt(jnp.mean(x * x, axis=-1, keepdims=True) + RMS_EPS) * g


def _dot(a, b):
    return jnp.dot(a, b, preferred_element_type=F32)


def _rows_to_lanes(ref, n_rows, base=None):
    parts = []
    for s in range(SUBLANES):
        idx = pl.ds(s, n_rows, stride=SUBLANES)
        parts.append(ref[idx, :] if base is None else ref[base, idx, :])
    return jnp.concatenate(parts, axis=-1)


def _lanes_to_rows(ref, val, base=None):
    n_rows = val.shape[0]
    for s in range(SUBLANES):
        idx = pl.ds(s, n_rows, stride=SUBLANES)
        piece = val[:, s * LANES:(s + 1) * LANES]
        if base is None:
            ref[idx, :] = piece
        else:
            ref[base, idx, :] = piece


def _inproj_kernel(n_prompt_tiles, xp_ref, xs_ref, g_ref, w_ref, q_ref, kv_ref, gbu_ref):
    i = pl.program_id(0)
    x = jnp.where(i < n_prompt_tiles, xp_ref[...], xs_ref[...])
    h = _rms(x, g_ref[...]).astype(BF16)
    aw = 3 * GROUP_W
    a = _dot(h, w_ref[:, 0:aw])
    gbu_ref[:, 0:GROUP_W] = a[:, 0:GROUP_W]
    gbu_ref[:, GROUP_W:2 * GROUP_W] = a[:, GROUP_W:2 * GROUP_W] * a[:, 2 * GROUP_W:aw]
    qw = N_GROUPS * GROUP_W
    _store_chunks(q_ref, _dot(h, w_ref[:, aw:aw + qw]) * (HEAD_DIM ** -0.5))
    _store_chunks(kv_ref, _dot(h, w_ref[:, aw + qw:]))


def _store_chunks(ref, val):
    for c in range(val.shape[1] // LANES):
        ref[c] = val[:, c * LANES:(c + 1) * LANES]


def _load_chunks(ref, rows=None):
    n = ref.shape[0]
    return jnp.concatenate([ref[c] if rows is None else ref[c, rows, :] for c in range(n)], axis=-1)


def _inproj(xp, xs, g, w):
    mp, d = xp.shape
    ms = xs.shape[0]
    tm = ROW_TILE
    npt, nst = mp // tm, ms // tm
    m = mp + ms
    ncols = w.shape[1]
    qw = N_GROUPS * GROUP_W
    return pl.pallas_call(
        functools.partial(_inproj_kernel, npt),
        grid=(npt + nst,),
        in_specs=[
            pl.BlockSpec((tm, d), lambda i: (jnp.minimum(i, npt - 1), 0)),
            pl.BlockSpec((tm, d), lambda i: (jnp.maximum(i - npt, 0), 0)),
            pl.BlockSpec((1, d), lambda i: (0, 0)),
            pl.BlockSpec((d, ncols), lambda i: (0, 0)),
        ],
        out_specs=[
            pl.BlockSpec((qw // LANES, tm, LANES), lambda i: (0, i, 0)),
            pl.BlockSpec((2 * qw // LANES, tm, LANES), lambda i: (0, i, 0)),
            pl.BlockSpec((tm, 2 * GROUP_W), lambda i: (i, 0)),
        ],
        out_shape=[
            jax.ShapeDtypeStruct((qw // LANES, m, LANES), F32),
            jax.ShapeDtypeStruct((2 * qw // LANES, m, LANES), F32),
            jax.ShapeDtypeStruct((m, 2 * GROUP_W), F32),
        ],
        compiler_params=_params("arbitrary"),
        name="inproj",
    )(xp, xs, g, w)


def _attn_prompt_kernel(q_ref, kv_ref, o_ref, m_scr, l_scr, acc_scr):
    g = pl.program_id(1)
    seq = q_ref.shape[1]

    @pl.when(g == 0)
    def _():
        m_scr[...] = jnp.full(m_scr.shape, NEG_INF, F32)
        l_scr[...] = jnp.zeros(l_scr.shape, F32)
        acc_scr[...] = jnp.zeros(acc_scr.shape, F32)

    qi = lax.broadcasted_iota(jnp.int32, (NK, 2 * NK), 0)
    kc = lax.broadcasted_iota(jnp.int32, (NK, 2 * NK), 1)
    band = (kc >= qi) & (kc <= qi + NK)
    lane = lax.broadcasted_iota(jnp.int32, (NK, LANES), 1)
    lane_kv = lax.broadcasted_iota(jnp.int32, (2 * NK, LANES), 1)
    ones_kv = jnp.ones((2 * NK, LANES), BF16)
    n_chunks = GROUP_W // LANES
    hpc = LANES // HEAD_DIM

    def block(blk, d, last):
        span = NK * d
        base = pl.multiple_of(blk * span, span)
        pbase = pl.multiple_of(jnp.maximum(blk - 1, 0) * span, span)
        mask = band & ((blk > 0) | (kc >= NK))
        for r in range(d):
            rows = pl.ds(base + r, NK, stride=d) if d > 1 else pl.ds(base, NK)
            prows = pl.ds(pbase + r, NK, stride=d) if d > 1 else pl.ds(pbase, NK)
            m_old = m_scr[rows, :]
            l_old = l_scr[rows, :]
            m_new, l_new = m_old, l_old
            for c in range(GROUP_W // LANES):
                qc = q_ref[c, rows, :].astype(BF16)
                kc_ = jnp.concatenate([kv_ref[c, prows, :], kv_ref[c, rows, :]], axis=0).astype(BF16)
                vc_ = jnp.concatenate([kv_ref[n_chunks + c, prows, :], kv_ref[n_chunks + c, rows, :]],
                                      axis=0).astype(BF16)
                acc = acc_scr[c, rows, :]
                scale, denom, contrib = None, None, None
                for hh in range(hpc):
                    h = c * hpc + hh
                    sel_q = (lane >= hh * HEAD_DIM) & (lane < (hh + 1) * HEAD_DIM)
                    sel_v = (lane_kv >= hh * HEAD_DIM) & (lane_kv < (hh + 1) * HEAD_DIM)
                    s = lax.dot_general(jnp.where(sel_q, qc, 0), kc_, (((1,), (1,)), ((), ())),
                                        preferred_element_type=F32)
                    s = jnp.where(mask, s, NEG_INF)
                    mo = m_old[:, h:h + 1]
                    mn = jnp.maximum(mo, jnp.max(s, axis=-1, keepdims=True))
                    p = jnp.exp(s - mn)
                    alpha = jnp.exp(mo - mn)
                    pv = _dot(p.astype(BF16), jnp.concatenate([jnp.where(sel_v, vc_, 0), ones_kv], axis=1))
                    ln = alpha * l_old[:, h:h + 1] + pv[:, LANES:]
                    contrib = pv[:, :LANES] if contrib is None else contrib + pv[:, :LANES]
                    scale = alpha if scale is None else jnp.where(sel_q, alpha, scale)
                    denom = ln if denom is None else jnp.where(sel_q, ln, denom)
                    m_new = jnp.where(lane == h, mn, m_new)
                    l_new = jnp.where(lane == h, ln, l_new)
                acc = scale * acc + contrib
                if last:
                    o_ref[c, rows, :] = acc / denom
                else:
                    acc_scr[c, rows, :] = acc
            if not last:
                m_scr[rows, :] = m_new
                l_scr[rows, :] = l_new

    for gi, (window, d) in enumerate(DIL_GROUPS):
        @pl.when(g == gi)
        def _(d=d, last=(gi == N_GROUPS - 1)):
            def body(blk, carry):
                block(blk, d, last)
                return carry
            n_blocks = seq // (NK * d)
            lax.fori_loop(0, n_blocks, body, 0, unroll=max(1, min(n_blocks, ATTN_BLOCKS_PER_BODY // d)))


def _attn_prompt(q, kv, batch, seq):
    assert seq % (NK * max(d for _, d in DIL_GROUPS)) == 0
    return pl.pallas_call(
        _attn_prompt_kernel,
        grid=(batch, N_GROUPS),
        in_specs=[
            pl.BlockSpec((GROUP_W // LANES, seq, LANES), lambda b, g: (g, b, 0)),
            pl.BlockSpec((2 * GROUP_W // LANES, seq, LANES), lambda b, g: (g, b, 0)),
        ],
        out_specs=pl.BlockSpec((GROUP_W // LANES, seq, LANES), lambda b, g: (0, b, 0)),
        out_shape=jax.ShapeDtypeStruct((GROUP_W // LANES, batch * seq, LANES), F32),
        scratch_shapes=[
            pltpu.VMEM((seq, LANES), F32),
            pltpu.VMEM((seq, LANES), F32),
            pltpu.VMEM((GROUP_W // LANES, seq, LANES), F32),
        ],
        compiler_params=_params("arbitrary", "arbitrary", vmem_limit=ATTN_VMEM_LIMIT),
        name="attn_prompt",
    )(q, kv)


def _mix_sample_kernel(q_ref, kn_ref, vn_ref, gbu_ref, st_ref, c0_ref, c1_ref, c2_ref, w_ref,
                       yb_ref, ya_ref, ns_ref):
    sb, t_len = q_ref.shape[0], q_ref.shape[3]
    w = w_ref[...]
    caches = (c0_ref, c1_ref, c2_ref)
    nt_dims = (((1,), (1,)), ((), ()))
    qn = lax.broadcasted_iota(jnp.int32, (t_len, t_len), 0)
    tn = lax.broadcasted_iota(jnp.int32, (t_len, t_len), 1)

    for s in range(sb):
        for h in range(HEADS):
            sc_parts, v_parts = [], []
            for gi, (window, d) in enumerate(DIL_GROUPS):
                c_ref = caches[gi]
                n_pos = c_ref.shape[-1]
                qg = q_ref[s, gi, h].astype(BF16)
                qc = lax.broadcasted_iota(jnp.int32, (t_len, n_pos), 0)
                pc = lax.broadcasted_iota(jnp.int32, (t_len, n_pos), 1)
                sc = _dot(qg, c_ref[s, 0, h].astype(BF16))
                sc_parts.append(jnp.where((((pc - qc) & (d - 1)) == 0) & (pc >= qc), sc, NEG_INF))
                v_parts.append((c_ref[s, 1, h].astype(BF16), True))
                sn = lax.dot_general(qg, kn_ref[s, gi, h].astype(BF16), nt_dims, preferred_element_type=F32)
                sc_parts.append(jnp.where((tn <= qn) & (((qn - tn) & (d - 1)) == 0), sn, NEG_INF))
                v_parts.append((vn_ref[s, gi, h].astype(BF16), False))
            mx = functools.reduce(jnp.maximum, [jnp.max(p, axis=-1, keepdims=True) for p in sc_parts])
            den = jnp.zeros((t_len, 1), F32)
            acc = jnp.zeros((t_len, HEAD_DIM), F32)
            for sc, (v, transposed) in zip(sc_parts, v_parts):
                p = jnp.exp(sc - mx)
                den = den + jnp.sum(p, axis=-1, keepdims=True)
                pb = p.astype(BF16)
                acc = acc + (lax.dot_general(pb, v, nt_dims, preferred_element_type=F32) if transposed
                             else _dot(pb, v))
            yb_ref[s, h] = acc / den
        gbu = gbu_ref[s]
        gb = gbu[:, 0:GROUP_W]
        u = gbu[:, GROUP_W:]
        st = st_ref[s]
        ext = [st[0:1], st[1:2]] + [u[t:t + 1] for t in range(t_len)]
        for t in range(t_len):
            conv = w[0:1] * ext[t] + w[1:2] * ext[t + 1] + w[2:3] * ext[t + 2]
            ya_ref[s, t:t + 1, :] = gb[t:t + 1] * conv
        for r in range(A_KERNEL - 1):
            ns_ref[s, r:r + 1, :] = ext[t_len + r]


def _mix_sample(q5, kn, vn, gbu3, st, c0, c1, c2, w):
    n, t_len = q5.shape[0], q5.shape[3]
    sb = ATTN_SEQ_BLOCK
    assert t_len >= A_KERNEL - 1 and n % sb == 0
    for c, (window, d) in zip((c0, c1, c2), DIL_GROUPS):
        assert c.shape[-1] == NK * d and d & (d - 1) == 0
    qspec = pl.BlockSpec((sb, N_GROUPS, HEADS, t_len, HEAD_DIM), lambda i: (i, 0, 0, 0, 0))
    cspec = lambda c: pl.BlockSpec((sb,) + c.shape[1:], lambda i: (i, 0, 0, 0, 0))
    return pl.pallas_call(
        _mix_sample_kernel,
        grid=(n // sb,),
        in_specs=[
            qspec, qspec, qspec,
            pl.BlockSpec((sb, t_len, 2 * GROUP_W), lambda i: (i, 0, 0)),
            pl.BlockSpec((sb, A_KERNEL - 1, GROUP_W), lambda i: (i, 0, 0)),
            cspec(c0), cspec(c1), cspec(c2),
            pl.BlockSpec((A_KERNEL, GROUP_W), lambda i: (0, 0)),
        ],
        out_specs=[
            pl.BlockSpec((sb, HEADS, t_len, HEAD_DIM), lambda i: (i, 0, 0, 0)),
            pl.BlockSpec((sb, t_len, GROUP_W), lambda i: (i, 0, 0)),
            pl.BlockSpec((sb, A_KERNEL - 1, GROUP_W), lambda i: (i, 0, 0)),
        ],
        out_shape=[
            jax.ShapeDtypeStruct((n, HEADS, t_len, HEAD_DIM), F32),
            jax.ShapeDtypeStruct((n, t_len, GROUP_W), F32),
            jax.ShapeDtypeStruct((n, A_KERNEL - 1, GROUP_W), F32),
        ],
        compiler_params=_params("arbitrary"),
        name="mix_sample",
    )(q5, kn, vn, gbu3, st, c0, c1, c2, w)


def _route(h2, wr_ref, br_ref, xs_ref, rinfo_ref, gtab_ref):
    tm = h2.shape[0]
    h_hi = h2.astype(BF16)
    h_lo = (h2 - h_hi.astype(F32)).astype(BF16)
    nt_dims = (((1,), (1,)), ((), ()))
    w_hi, w_lo = wr_ref[0], wr_ref[1]
    lt = (lax.dot_general(w_hi, h_hi, nt_dims, preferred_element_type=F32)
          + lax.dot_general(w_hi, h_lo, nt_dims, preferred_element_type=F32)
          + lax.dot_general(w_lo, h_hi, nt_dims, preferred_element_type=F32)) + br_ref[...]
    rg = lax.broadcasted_iota(jnp.int32, (SUBLANES, tm), 0)
    gl = jnp.where(rg < N_EXPERT_GROUPS, lt[0:SUBLANES], NEG_INF)
    ge = jnp.exp(gl - jnp.max(gl, axis=0, keepdims=True))
    gp = ge / jnp.sum(ge, axis=0, keepdims=True)
    g_val = jnp.max(gp, axis=0, keepdims=True)
    g_idx = jnp.min(jnp.where(gp == g_val, rg, SUBLANES), axis=0, keepdims=True)
    re = lax.broadcasted_iota(jnp.int32, (N_EXPERTS, tm), 0)
    sel = (re // EXPERTS_PER_GROUP) == g_idx
    el = jnp.where(sel, lt[ROUTER_EXPERT_ROW0:ROUTER_EXPERT_ROW0 + N_EXPERTS], NEG_INF)
    ee = jnp.exp(el - jnp.max(el, axis=0, keepdims=True))
    ep = jnp.where(sel, ee / jnp.sum(ee, axis=0, keepdims=True), -1.0)
    v1 = jnp.max(ep, axis=0, keepdims=True)
    i1 = jnp.min(jnp.where(ep == v1, re, N_EXPERTS), axis=0, keepdims=True)
    ep2 = jnp.where(re == i1, -1.0, ep)
    v2 = jnp.max(ep2, axis=0, keepdims=True)
    i2 = jnp.min(jnp.where(ep2 == v2, re, N_EXPERTS), axis=0, keepdims=True)
    scale = g_val / (v1 + v2)

    oh = [(re == i1).astype(F32), (re == i2).astype(F32)]
    cmat = (oh[0] + oh[1]).astype(BF16)
    earlier = (lax.broadcasted_iota(jnp.int32, (tm, tm), 0)
               < lax.broadcasted_iota(jnp.int32, (tm, tm), 1)).astype(BF16)
    before = _dot(cmat, earlier)
    cnt_row = lax.dot_general(jnp.ones((SUBLANES, tm), BF16), cmat, nt_dims, preferred_element_type=F32)
    grp_row = ((cnt_row.astype(jnp.int32) + (SUBLANES - 1)) // SUBLANES).astype(F32)
    lower = (lax.broadcasted_iota(jnp.int32, (N_EXPERTS, N_EXPERTS), 0)
             < lax.broadcasted_iota(jnp.int32, (N_EXPERTS, N_EXPERTS), 1)).astype(BF16)
    goff_row = _dot(grp_row.astype(BF16), lower)
    slot_rows = lax.broadcasted_iota(jnp.int32, (TILE_SLOTS, tm), 0)
    place = None
    pos = []
    for k in range(2):
        rank = jnp.sum(oh[k] * before, axis=0, keepdims=True)
        seg = _dot(goff_row.astype(BF16), oh[k].astype(BF16))[0:1]
        pos.append(seg * SUBLANES + rank)
        hit = slot_rows == pos[k].astype(jnp.int32)
        place = hit if place is None else place | hit
    xs_ref[...] = _dot(jnp.where(place, 1.0, 0.0).astype(BF16), h_hi)
    rl = lax.broadcasted_iota(jnp.int32, (LANES, tm), 0)
    info_t = jnp.where(rl == 0, v1 * scale, jnp.where(rl == 1, v2 * scale,
                       jnp.where(rl == 2, pos[0], jnp.where(rl == 3, pos[1], 0.0))))
    rinfo_ref[...] = info_t.T
    spread = (lax.broadcasted_iota(jnp.int32, (N_EXPERTS, LANES), 0)
              == lax.broadcasted_iota(jnp.int32, (N_EXPERTS, LANES), 1)).astype(BF16)
    gtab_ref[...] = _dot(grp_row.astype(BF16), spread).astype(jnp.int32)


ROUTER_EXPERT_ROW0 = SUBLANES
ROUTER_ROWS = 48
TILE_GROUPS = -(-(2 * ROW_TILE + (SUBLANES - 1) * N_EXPERTS) // SUBLANES)
TILE_SLOTS = TILE_GROUPS * SUBLANES


def _router_specs(tm, d):
    ins = [pl.BlockSpec((1, d), lambda i: (0, 0)),
           pl.BlockSpec((2, ROUTER_ROWS, d), lambda i: (0, 0, 0)),
           pl.BlockSpec((ROUTER_ROWS, 1), lambda i: (0, 0))]
    outs = [pl.BlockSpec((TILE_SLOTS, d), lambda i: (i, 0)),
            pl.BlockSpec((tm, LANES), lambda i: (i, 0)),
            pl.BlockSpec((None, SUBLANES, LANES), lambda i: (i, 0, 0))]
    return ins, outs


def _router_shapes(m, d):
    nb = m // ROW_TILE
    return [jax.ShapeDtypeStruct((nb * TILE_SLOTS, d), F32), jax.ShapeDtypeStruct((m, LANES), F32),
            jax.ShapeDtypeStruct((nb, SUBLANES, LANES), jnp.int32)]


def _outproj_kernel(n_prompt_tiles, tiles_per_seq, xp_ref, xs_ref, gbu_ref, halo_ref, cw_ref, yb_ref, ys_ref,
                    wo_ref, g_ref, wr_ref, br_ref, x1_ref, xs_out_ref, rinfo_ref, gtab_ref, conv_scr):
    i = pl.program_id(0)
    tm = xp_ref.shape[0]
    is_p = i < n_prompt_tiles
    x = jnp.where(is_p, xp_ref[...], xs_ref[...])
    u = gbu_ref[:, GROUP_W:]
    first = (i % tiles_per_seq) == 0
    conv_scr[0:SUBLANES, :] = jnp.where(first, 0.0, halo_ref[:, GROUP_W:])
    conv_scr[SUBLANES:, :] = u
    cw = cw_ref[...]
    ya = gbu_ref[:, 0:GROUP_W] * (cw[0:1, :] * conv_scr[pl.ds(SUBLANES - 2, tm), :]
                                  + cw[1:2, :] * conv_scr[pl.ds(SUBLANES - 1, tm), :] + cw[2:3, :] * u)
    ymix = jnp.where(is_p, jnp.concatenate([ya, _load_chunks(yb_ref)], axis=-1), ys_ref[...])
    x1 = x + _dot(ymix.astype(BF16), wo_ref[...])
    x1_ref[...] = x1
    _route(_rms(x1, g_ref[...]), wr_ref, br_ref, xs_out_ref, rinfo_ref, gtab_ref)


def _outproj(xp, xs, gbu, conv_w, yb_p, ymix_s, wo, g, wr, br, seq):
    mp, d = xp.shape
    ms = xs.shape[0]
    tm = ROW_TILE
    npt, nst = mp // tm, ms // tm
    m = mp + ms
    r_in, r_out = _router_specs(tm, d)
    pmap = lambda i: (jnp.minimum(i, npt - 1), 0)
    smap = lambda i: (jnp.maximum(i - npt, 0), 0)
    halo_map = lambda i: (jnp.maximum(jnp.minimum(i, npt - 1) * (tm // SUBLANES) - 1, 0), 0)
    return pl.pallas_call(
        functools.partial(_outproj_kernel, npt, seq // tm),
        grid=(npt + nst,),
        in_specs=[
            pl.BlockSpec((tm, d), pmap),
            pl.BlockSpec((tm, d), smap),
            pl.BlockSpec((tm, 2 * GROUP_W), pmap),
            pl.BlockSpec((SUBLANES, 2 * GROUP_W), halo_map),
            pl.BlockSpec((A_KERNEL, GROUP_W), lambda i: (0, 0)),
            pl.BlockSpec((GROUP_W // LANES, tm, LANES), lambda i: (0, jnp.minimum(i, npt - 1), 0)),
            pl.BlockSpec((tm, 2 * GROUP_W), smap),
            pl.BlockSpec((2 * GROUP_W, d), lambda i: (0, 0)),
        ] + r_in,
        out_specs=[pl.BlockSpec((tm, d), lambda i: (i, 0))] + r_out,
        out_shape=[jax.ShapeDtypeStruct((m, d), F32)] + _router_shapes(m, d),
        scratch_shapes=[pltpu.VMEM((tm + SUBLANES, GROUP_W), F32)],
        compiler_params=_params("arbitrary"),
        name="outproj_router",
    )(xp, xs, gbu, gbu, conv_w, yb_p, ymix_s, wo, g, wr, br)


def _experts_kernel(te_ref, src_ref, dst_ref, nt_ref, h_hbm, wg_ref, wu_ref, wd_ref, y_hbm, xbuf, ybuf, gsem, ssem,
                    *, trash_row):
    i = pl.program_id(0)
    nt = nt_ref[0]
    ng = CHUNK_GROUPS
    slot = i % 2
    other = 1 - slot

    def group_rows(r):
        return pl.ds(r if isinstance(r, int) else pl.multiple_of(r, SUBLANES), SUBLANES)

    def gather_copy(p, sl, j):
        return pltpu.make_async_copy(h_hbm.at[group_rows(src_ref[p]), :],
                                     xbuf.at[sl, group_rows(j * SUBLANES), :], gsem.at[sl])

    def scatter_copy(p, sl, j):
        return pltpu.make_async_copy(ybuf.at[sl, group_rows(j * SUBLANES), :],
                                     y_hbm.at[group_rows(dst_ref[p]), :], ssem.at[sl])

    def wait_gather(sl):
        pltpu.make_async_copy(h_hbm.at[pl.ds(0, ng * SUBLANES), :], xbuf.at[sl], gsem.at[sl]).wait()

    def wait_scatter(sl):
        pltpu.make_async_copy(ybuf.at[sl], y_hbm.at[pl.ds(0, ng * SUBLANES), :], ssem.at[sl]).wait()

    @pl.when(i == 0)
    def _():
        ybuf[...] = jnp.zeros(ybuf.shape, F32)
        init = pltpu.make_async_copy(ybuf.at[0], y_hbm.at[pl.ds(trash_row, ng * SUBLANES), :], ssem.at[0])
        init.start()
        init.wait()

        def tail_copy(b):
            first = pl.multiple_of(b * TILE_SLOTS + 2 * ROW_TILE, SUBLANES)
            return pltpu.make_async_copy(ybuf.at[0, pl.ds(0, TILE_SLOTS - 2 * ROW_TILE), :],
                                         y_hbm.at[pl.ds(first, TILE_SLOTS - 2 * ROW_TILE), :], ssem.at[0])

        def start_body(b, c):
            tail_copy(b).start()
            return c

        def wait_body(b, c):
            tail_copy(b).wait()
            return c
        lax.fori_loop(0, trash_row // TILE_SLOTS, start_body, 0)
        lax.fori_loop(0, trash_row // TILE_SLOTS, wait_body, 0)
        for j in range(ng):
            gather_copy(j, 0, j).start()

    @pl.when(i < nt)
    def _():
        wait_gather(slot)

        @pl.when(i >= 1)
        def _():
            wait_scatter(slot)

        base = i * ng
        for j in range(ng):
            gather_copy(base + ng + j, other, j).start(priority=j % 2)
            scatter_copy(base + j, other, j).start(priority=(j + 1) % 2)
        x = xbuf[slot].astype(BF16)
        gate = _dot(x, wg_ref[...].astype(BF16))
        up = _dot(x, wu_ref[...].astype(BF16))
        act = (gate * jax.nn.sigmoid(gate) * up).astype(BF16)
        ybuf[slot] = _dot(act, wd_ref[...].astype(BF16))

    @pl.when(i == nt)
    def _():
        wait_gather(slot)
        wait_scatter(slot)
        for j in range(ng):
            scatter_copy(i * ng + j, other, j).start()
        wait_scatter(other)


def _experts(xs, tile_expert, src, dst, n_tiles, wg, wu, wd, layer):
    ng = CHUNK_GROUPS
    nt_max = tile_expert.shape[0]
    d, f = wg.shape[2], wg.shape[3]
    n_rows = xs.shape[0] + ng * SUBLANES
    wmap = lambda i, te_ref, src_ref, dst_ref, nt_ref: (layer, te_ref[i], 0, 0)
    grid_spec = pltpu.PrefetchScalarGridSpec(
        num_scalar_prefetch=4,
        grid=(nt_max,),
        in_specs=[
            pl.BlockSpec(memory_space=pl.ANY),
            pl.BlockSpec((None, None, d, f), wmap),
            pl.BlockSpec((None, None, d, f), wmap),
            pl.BlockSpec((None, None, f, d), wmap),
        ],
        out_specs=pl.BlockSpec(memory_space=pl.ANY),
        scratch_shapes=[
            pltpu.VMEM((2, ng * SUBLANES, d), F32),
            pltpu.VMEM((2, ng * SUBLANES, d), F32),
            pltpu.SemaphoreType.DMA((2,)),
            pltpu.SemaphoreType.DMA((2,)),
        ],
    )
    return pl.pallas_call(
        functools.partial(_experts_kernel, trash_row=xs.shape[0]),
        grid_spec=grid_spec,
        out_shape=jax.ShapeDtypeStruct((n_rows, d), F32),
        compiler_params=_params("arbitrary"),
        name="experts",
    )(tile_expert, src, dst, n_tiles, xs, wg, wu, wd)


def _plan_chunks(gtab):
    ng = CHUNK_GROUPS
    grp = gtab[:, 0, :N_EXPERTS]
    nb = grp.shape[0]
    nt_max = -(-nb * TILE_GROUPS // ng) + N_EXPERTS + 1
    total = jnp.sum(grp, axis=0)
    chunks = (total + ng - 1) // ng
    chunk_end = jnp.cumsum(chunks)
    chunk_start = chunk_end - chunks
    n_chunks = chunk_end[-1]
    seg_off = jnp.cumsum(grp, axis=1) - grp
    seg_end = jnp.cumsum(grp, axis=0)
    seg_start = seg_end - grp
    cidx = jnp.arange(nt_max, dtype=jnp.int32)
    chunk_expert = jnp.sum((cidx[:, None] >= chunk_end[None, :]).astype(jnp.int32), axis=1)
    last_expert = jnp.max(jnp.where(total > 0, jnp.arange(N_EXPERTS), 0))
    chunk_expert = jnp.where(cidx < n_chunks, jnp.minimum(chunk_expert, N_EXPERTS - 1), last_expert).astype(jnp.int32)
    is_e = chunk_expert[:, None] == jnp.arange(N_EXPERTS, dtype=jnp.int32)[None, :]
    pick = lambda per_expert: jnp.sum(jnp.where(is_e, per_expert[None, :], 0), axis=1)
    pick_rows = lambda table: jnp.sum(jnp.where(is_e[:, :, None], table.T[None, :, :], 0), axis=1)
    within = (cidx - pick(chunk_start))[:, None] * ng + jnp.arange(ng, dtype=jnp.int32)[None, :]
    valid = (within < pick(total)[:, None]) & (cidx < n_chunks)[:, None]
    ends = pick_rows(seg_end)
    base = pick_rows(jnp.arange(nb, dtype=jnp.int32)[:, None] * TILE_GROUPS + seg_off - seg_start)
    step = jnp.concatenate([base[:, :1], base[:, 1:] - base[:, :-1]], axis=1)
    passed = jnp.concatenate([jnp.ones((nt_max, ng, 1), bool), within[:, :, None] >= ends[:, None, :-1]], axis=2)
    group = within + jnp.sum(jnp.where(passed, step[:, None, :], 0), axis=2)
    pad_dst = (nb * TILE_GROUPS + jnp.arange(ng, dtype=jnp.int32)) * SUBLANES
    src = jnp.where(valid, group * SUBLANES, 0).astype(jnp.int32).reshape(nt_max * ng)
    dst = jnp.where(valid, group * SUBLANES, pad_dst[None, :]).astype(jnp.int32).reshape(nt_max * ng)
    dst = jnp.concatenate([pad_dst, dst])
    return chunk_expert, src, dst, n_chunks.astype(jnp.int32).reshape(1)


def _combine(x_ref, y_ref, rinfo_ref):
    tm = x_ref.shape[0]
    n_slots = y_ref.shape[0]
    r = rinfo_ref[...]
    col = lax.broadcasted_iota(jnp.int32, (tm, n_slots), 1)
    pick = (jnp.where(col == r[:, 2:3].astype(jnp.int32), r[:, 0:1], 0.0)
            + jnp.where(col == r[:, 3:4].astype(jnp.int32), r[:, 1:2], 0.0))
    return x_ref[...] + _dot(pick.astype(BF16), y_ref[...].astype(BF16))


def _combine_specs(tm, d):
    return [
        pl.BlockSpec((tm, d), lambda i: (i, 0)),
        pl.BlockSpec((TILE_SLOTS, d), lambda i: (i, 0)),
        pl.BlockSpec((tm, LANES), lambda i: (i, 0)),
    ]


def _combine_pw1_kernel(x_ref, y_ref, rinfo_ref, g_ref, w_ref, b_ref, x2_ref, u_ref):
    x2 = _combine(x_ref, y_ref, rinfo_ref)
    x2_ref[...] = x2
    d = x2.shape[1]
    z = _dot(_rms(x2, g_ref[...]).astype(BF16), w_ref[...]) + b_ref[...]
    u_ref[...] = z[:, :d] * jax.nn.sigmoid(z[:, d:])


def _combine_pw1(x1, y2, rinfo, g, w, b):
    m, d = x1.shape
    tm = ROW_TILE
    return pl.pallas_call(
        _combine_pw1_kernel,
        grid=(m // tm,),
        in_specs=_combine_specs(tm, d) + [
            pl.BlockSpec((1, d), lambda i: (0, 0)),
            pl.BlockSpec((d, 2 * d), lambda i: (0, 0)),
            pl.BlockSpec((1, 2 * d), lambda i: (0, 0)),
        ],
        out_specs=[pl.BlockSpec((tm, d), lambda i: (i, 0)), pl.BlockSpec((tm, d), lambda i: (i, 0))],
        out_shape=[jax.ShapeDtypeStruct((m, d), F32), jax.ShapeDtypeStruct((m, d), F32)],
        compiler_params=_params("arbitrary"),
        name="combine_pw1",
    )(x1, y2, rinfo, g, w, b)


def _combine_final_kernel(n_prompt_tiles, x_ref, y_ref, rinfo_ref, g_ref, yp_ref, ys_ref):
    i = pl.program_id(0)
    y = _rms(_combine(x_ref, y_ref, rinfo_ref), g_ref[...])

    @pl.when(i < n_prompt_tiles)
    def _():
        yp_ref[...] = y

    @pl.when(i >= n_prompt_tiles)
    def _():
        ys_ref[...] = y


def _combine_final(x3, y2, rinfo, g, mp):
    m, d = x3.shape
    tm = ROW_TILE
    npt = mp // tm
    return pl.pallas_call(
        functools.partial(_combine_final_kernel, npt),
        grid=(m // tm,),
        in_specs=_combine_specs(tm, d) + [pl.BlockSpec((1, d), lambda i: (0, 0))],
        out_specs=[pl.BlockSpec((tm, d), lambda i: (jnp.minimum(i, npt - 1), 0)),
                   pl.BlockSpec((tm, d), lambda i: (jnp.maximum(i - npt, 0), 0))],
        out_shape=[jax.ShapeDtypeStruct((mp, d), F32), jax.ShapeDtypeStruct((m - mp, d), F32)],
        compiler_params=_params("arbitrary"),
        name="combine_final",
    )(x3, y2, rinfo, g)


def _dwconv_prompt_tile(first, u_ref, halo_ref, w_ref, b_ref, scr):
    tm = u_ref.shape[0]
    scr[0, 0:CONV_HALO, :] = jnp.where(first, 0.0, halo_ref[...])
    scr[0, CONV_HALO:, :] = u_ref[...]
    n_shift = tm + CONV_HALO - SUBLANES
    for s in range(1, SUBLANES):
        scr[s, 0:n_shift, :] = scr[0, pl.ds(s, n_shift), :]
    acc = jnp.broadcast_to(b_ref[...], u_ref.shape)
    for k in range(C_KERNEL):
        off = CONV_HALO - (C_KERNEL - 1) + k
        acc = acc + w_ref[k:k + 1, :] * scr[off % SUBLANES, pl.ds(off - off % SUBLANES, tm), :]
    return acc


def _dwconv_sample_kernel(st_ref, u_ref, w_ref, b_ref, c_ref, ns_ref):
    n_state, t_len = st_ref.shape[0], u_ref.shape[0]
    rows = [st_ref[r] for r in range(n_state)] + [u_ref[t] for t in range(t_len)]
    for t in range(t_len):
        acc = jnp.broadcast_to(b_ref[...], rows[0].shape)
        for k in range(C_KERNEL):
            acc = acc + w_ref[k:k + 1, :] * rows[t + k]
        c_ref[t] = acc
    for r in range(n_state):
        ns_ref[r] = rows[r + t_len]


def _dwconv_sample(st, u3, w, b):
    n_state, n, d = st.shape
    t_len = u3.shape[0]
    sb = SEQ_BLOCK
    assert n_state == C_KERNEL - 1 and n % sb == 0
    return pl.pallas_call(
        _dwconv_sample_kernel,
        grid=(n // sb,),
        in_specs=[
            pl.BlockSpec((n_state, sb, d), lambda i: (0, i, 0)),
            pl.BlockSpec((t_len, sb, d), lambda i: (0, i, 0)),
            pl.BlockSpec((C_KERNEL, d), lambda i: (0, 0)),
            pl.BlockSpec((1, d), lambda i: (0, 0)),
        ],
        out_specs=[
            pl.BlockSpec((t_len, sb, d), lambda i: (0, i, 0)),
            pl.BlockSpec((n_state, sb, d), lambda i: (0, i, 0)),
        ],
        out_shape=[jax.ShapeDtypeStruct((t_len, n, d), F32), jax.ShapeDtypeStruct((n_state, n, d), F32)],
        compiler_params=_params("arbitrary"),
        name="dwconv_sample",
    )(st, u3, w, b)


def _conf_tail_kernel(n_prompt_tiles, tiles_per_seq, x_ref, u_ref, halo_ref, dww_ref, dwb_ref, cs_ref,
                      lg_ref, lb_ref, w_ref, b_ref, g_ref, wr_ref, br_ref,
                      x3_ref, xs_out_ref, rinfo_ref, gtab_ref, conv_scr):
    i = pl.program_id(0)
    c_p = _dwconv_prompt_tile((i % tiles_per_seq) == 0, u_ref, halo_ref, dww_ref, dwb_ref, conv_scr)
    c = jnp.where(i < n_prompt_tiles, c_p, cs_ref[...])
    mu = jnp.mean(c, axis=-1, keepdims=True)
    cc = c - mu
    var = jnp.mean(cc * cc, axis=-1, keepdims=True)
    y = cc * lax.rsqrt(var + LN_EPS) * lg_ref[...] + lb_ref[...]
    y = y * jax.nn.sigmoid(y)
    x3 = x_ref[...] + _dot(y.astype(BF16), w_ref[...]) + b_ref[...]
    x3_ref[...] = x3
    _route(_rms(x3, g_ref[...]), wr_ref, br_ref, xs_out_ref, rinfo_ref, gtab_ref)


def _conf_tail(x2, u, dw_w, dw_b, c_s, ln_g, ln_b, w, b, g, wr, br, mp, seq):
    m, d = x2.shape
    tm = ROW_TILE
    npt = mp // tm
    r_in, r_out = _router_specs(tm, d)
    vec = pl.BlockSpec((1, d), lambda i: (0, 0))
    ptile = lambda i: jnp.minimum(i, npt - 1)
    return pl.pallas_call(
        functools.partial(_conf_tail_kernel, npt, seq // tm),
        grid=(m // tm,),
        in_specs=[
            pl.BlockSpec((tm, d), lambda i: (i, 0)),
            pl.BlockSpec((tm, d), lambda i: (ptile(i), 0)),
            pl.BlockSpec((CONV_HALO, d), lambda i: (jnp.maximum(ptile(i) * (tm // CONV_HALO) - 1, 0), 0)),
            pl.BlockSpec((C_KERNEL, d), lambda i: (0, 0)),
            vec,
            pl.BlockSpec((tm, d), lambda i: (jnp.maximum(i - npt, 0), 0)),
            vec, vec,
            pl.BlockSpec((d, d), lambda i: (0, 0)),
            vec,
        ] + r_in,
        out_specs=[pl.BlockSpec((tm, d), lambda i: (i, 0))] + r_out,
        out_shape=[jax.ShapeDtypeStruct((m, d), F32)] + _router_shapes(m, d),
        scratch_shapes=[pltpu.VMEM((SUBLANES, tm + CONV_HALO, d), F32)],
        compiler_params=_params("arbitrary"),
        name="conf_tail_router",
    )(x2, u, u, dw_w, dw_b, c_s, ln_g, ln_b, w, b, g, wr, br)


def _router_weights(w_rg, b_rg, w_re, b_re):
    d = w_rg.shape[0]
    gpad = ROUTER_EXPERT_ROW0 - N_EXPERT_GROUPS
    epad = ROUTER_ROWS - ROUTER_EXPERT_ROW0 - N_EXPERTS
    wt = jnp.concatenate([w_rg.T, jnp.zeros((gpad, d), F32), w_re.T, jnp.zeros((epad, d), F32)], axis=0)
    w_hi = wt.astype(BF16)
    w_lo = (wt - w_hi.astype(F32)).astype(BF16)
    br = jnp.concatenate([b_rg, jnp.zeros((gpad,), F32), b_re, jnp.zeros((epad,), F32)])[:, None]
    return jnp.stack([w_hi, w_lo]), br


def _moe(xs, gtab, wg, wu, wd, layer):
    chunk_expert, src, dst, n_chunks = _plan_chunks(gtab)
    return _experts(xs, chunk_expert, src, dst, n_chunks, wg, wu, wd, layer)


def kernel(x_prompt, x_sample, state_shortconv, cache_kv_w128, cache_kv_w512, cache_kv_w2048, state_conformer,
           g_mix, g_ffn, g_final, w_in, conv_a_w, w_out, w_pw1, b_pw1, dw_w, dw_b, ln_g, ln_b, w_pw2, b_pw2,
           w_router_group, b_router_group, w_router_expert, b_router_expert, w_gate, w_up, w_down):
    batch, seq, d = x_prompt.shape
    n_dec, t_dec, _ = x_sample.shape
    mp, ms = batch * seq, n_dec * t_dec
    assert g_mix.shape[0] == 2 and mp % ROW_TILE == 0 and ms % ROW_TILE == 0 and seq % ROW_TILE == 0
    xp = x_prompt.reshape(mp, d)
    xs = x_sample.reshape(ms, d)
    slab = (HEADS, HEAD_DIM)

    aw, qw = 3 * GROUP_W, N_GROUPS * GROUP_W
    w0 = w_in[0]
    kcols = w0[:, aw + qw:aw + 2 * qw].reshape(d, N_GROUPS, GROUP_W)
    vcols = w0[:, aw + 2 * qw:].reshape(d, N_GROUPS, GROUP_W)
    w_perm = jnp.concatenate([w0[:, :aw + qw], jnp.stack([kcols, vcols], axis=2).reshape(d, 2 * qw)], axis=1)
    q, kv, gbu = _inproj(xp, xs, g_mix[0][None, :], w_perm.astype(BF16))

    yb_p = _attn_prompt(q, kv, batch, seq)

    hpc = LANES // HEAD_DIM
    q_s = jnp.transpose(q[:, mp:].reshape(N_GROUPS, HEADS // hpc, n_dec, t_dec, hpc, HEAD_DIM),
                        (2, 0, 1, 4, 3, 5)).reshape(n_dec, N_GROUPS, HEADS, t_dec, HEAD_DIM)
    kv_s = jnp.transpose(kv[:, mp:].reshape(N_GROUPS, 2, HEADS // hpc, n_dec, t_dec, hpc, HEAD_DIM),
                         (1, 3, 0, 2, 5, 4, 6)).reshape(2, n_dec, N_GROUPS, HEADS, t_dec, HEAD_DIM)
    gbu_s = gbu[mp:].reshape(n_dec, t_dec, 2 * GROUP_W)
    c0, c1, c2 = (jnp.transpose(c[0], (0, 2, 3, 4, 1)) for c in (cache_kv_w128, cache_kv_w512, cache_kv_w2048))
    yb_s, ya_s, s_sc = _mix_sample(q_s, kv_s[0], kv_s[1], gbu_s, state_shortconv[0], c0, c1, c2, conv_a_w[0])
    ymix_s = jnp.concatenate([ya_s.reshape(ms, GROUP_W),
                              jnp.transpose(yb_s, (0, 2, 1, 3)).reshape(ms, GROUP_W)], axis=-1)

    wr0, br0 = _router_weights(w_router_group[0], b_router_group[0], w_router_expert[0], b_router_expert[0])
    x1, xsort0, rinfo0, gtab0 = _outproj(xp, xs, gbu, conv_a_w[0], yb_p, ymix_s, w_out[0].astype(BF16),
                                         g_ffn[0][None, :], wr0, br0, seq)
    y2 = _moe(xsort0, gtab0, w_gate, w_up, w_down, 0)

    x2, u = _combine_pw1(x1, y2, rinfo0, g_mix[1][None, :], w_pw1[0].astype(BF16), b_pw1[0][None, :])
    c_s3, s_cf = _dwconv_sample(jnp.transpose(state_conformer[0], (1, 0, 2)),
                                jnp.transpose(u[mp:].reshape(n_dec, t_dec, d), (1, 0, 2)), dw_w[0], dw_b[0][None, :])
    c_s = jnp.transpose(c_s3, (1, 0, 2)).reshape(ms, d)
    wr1, br1 = _router_weights(w_router_group[1], b_router_group[1], w_router_expert[1], b_router_expert[1])
    x3, xsort1, rinfo1, gtab1 = _conf_tail(x2, u, dw_w[0], dw_b[0][None, :], c_s, ln_g[0][None, :], ln_b[0][None, :],
                                           w_pw2[0].astype(BF16), b_pw2[0][None, :], g_ffn[1][None, :], wr1, br1,
                                           mp, seq)
    y2b = _moe(xsort1, gtab1, w_gate, w_up, w_down, 1)
    y_p, y_s = _combine_final(x3, y2b, rinfo1, g_final[None, :], mp)

    cpg = 2 * GROUP_W // LANES

    def tail_rows(arr, n, axis):
        return jnp.stack([lax.slice_in_dim(arr, (b + 1) * seq - n, (b + 1) * seq, axis=axis) for b in range(batch)])

    p_kv = []
    for gi, (w, _) in enumerate(DIL_GROUPS):
        rows = tail_rows(kv[gi * cpg:(gi + 1) * cpg], min(w, seq), 1)
        p_kv.append(jnp.transpose(rows, (0, 2, 1, 3)).reshape(1, batch, min(w, seq), 2, *slab))
    s_kv = [jnp.transpose(kv_s[:, :, gi], (1, 3, 0, 2, 4))[None] for gi in range(N_GROUPS)]
    p_sc = tail_rows(gbu, A_KERNEL - 1, 0)[:, :, GROUP_W:][None]
    p_cf = tail_rows(u, C_KERNEL - 1, 0)[None]
    return (y_p.reshape(batch, seq, d), y_s.reshape(n_dec, t_dec, d), p_sc, p_kv[0], p_kv[1], p_kv[2], p_cf,
            s_sc[None], s_kv[0], s_kv[1], s_kv[2], jnp.transpose(s_cf, (1, 0, 2))[None])
```

```python
import functools

import jax
import jax.numpy as jnp
from jax import lax
from jax.experimental import pallas as pl
from jax.experimental.pallas import tpu as pltpu

F32 = jnp.float32
BF16 = jnp.bfloat16

DIL_GROUPS = ((128, 1), (512, 4), (2048, 16))
N_GROUPS = len(DIL_GROUPS)
HEADS = 4
HEAD_DIM = 64
GROUP_W = HEADS * HEAD_DIM
NK = DIL_GROUPS[0][0] // DIL_GROUPS[0][1]
A_KERNEL = 3
C_KERNEL = 31
N_EXPERT_GROUPS = 4
EXPERTS_PER_GROUP = 8
N_EXPERTS = N_EXPERT_GROUPS * EXPERTS_PER_GROUP
RMS_EPS = 1e-6
LN_EPS = 1e-5
NEG_INF = -1e30

SUBLANES = 8
LANES = 128
VMEM_LIMIT = 48 * 1024 * 1024
ATTN_VMEM_LIMIT = 60 * 1024 * 1024
ATTN_BLOCKS_PER_BODY = 4

ROW_TILE = 256
GROUP_ROWS = 16
CHUNK_GROUPS = 32
CONV_HALO = 32
SEQ_BLOCK = 8
ATTN_SEQ_BLOCK = 2


def _params(*sem, vmem_limit=VMEM_LIMIT):
    return pltpu.CompilerParams(dimension_semantics=sem, vmem_limit_bytes=vmem_limit)


def _rms(x, g):
    return x * lax.rsqrt(jnp.mean(x * x, axis=-1, keepdims=True) + RMS_EPS) * g


def _dot(a, b):
    return jnp.dot(a, b, preferred_element_type=F32)


def _rows_to_lanes(ref, n_rows, base=None):
    parts = []
    for s in range(SUBLANES):
        idx = pl.ds(s, n_rows, stride=SUBLANES)
        parts.append(ref[idx, :] if base is None else ref[base, idx, :])
    return jnp.concatenate(parts, axis=-1)


def _lanes_to_rows(ref, val, base=None):
    n_rows = val.shape[0]
    for s in range(SUBLANES):
        idx = pl.ds(s, n_rows, stride=SUBLANES)
        piece = val[:, s * LANES:(s + 1) * LANES]
        if base is None:
            ref[idx, :] = piece
        else:
            ref[base, idx, :] = piece


def _inproj_kernel(n_prompt_tiles, xp_ref, xs_ref, g_ref, w_ref, q_ref, kv_ref, gbu_ref):
    i = pl.program_id(0)
    x = jnp.where(i < n_prompt_tiles, xp_ref[...], xs_ref[...])
    h = _rms(x, g_ref[...]).astype(BF16)
    aw = 3 * GROUP_W
    a = _dot(h, w_ref[:, 0:aw])
    gbu_ref[:, 0:GROUP_W] = a[:, 0:GROUP_W]
    gbu_ref[:, GROUP_W:2 * GROUP_W] = a[:, GROUP_W:2 * GROUP_W] * a[:, 2 * GROUP_W:aw]
    qw = N_GROUPS * GROUP_W
    _store_chunks(q_ref, _dot(h, w_ref[:, aw:aw + qw]) * (HEAD_DIM ** -0.5))
    _store_chunks(kv_ref, _dot(h, w_ref[:, aw + qw:]))


def _store_chunks(ref, val):
    for c in range(val.shape[1] // LANES):
        ref[c] = val[:, c * LANES:(c + 1) * LANES]


def _load_chunks(ref, rows=None):
    n = ref.shape[0]
    return jnp.concatenate([ref[c] if rows is None else ref[c, rows, :] for c in range(n)], axis=-1)


def _inproj(xp, xs, g, w):
    mp, d = xp.shape
    ms = xs.shape[0]
    tm = ROW_TILE
    npt, nst = mp // tm, ms // tm
    m = mp + ms
    ncols = w.shape[1]
    qw = N_GROUPS * GROUP_W
    return pl.pallas_call(
        functools.partial(_inproj_kernel, npt),
        grid=(npt + nst,),
        in_specs=[
            pl.BlockSpec((tm, d), lambda i: (jnp.minimum(i, npt - 1), 0)),
            pl.BlockSpec((tm, d), lambda i: (jnp.maximum(i - npt, 0), 0)),
            pl.BlockSpec((1, d), lambda i: (0, 0)),
            pl.BlockSpec((d, ncols), lambda i: (0, 0)),
        ],
        out_specs=[
            pl.BlockSpec((qw // LANES, tm, LANES), lambda i: (0, i, 0)),
            pl.BlockSpec((2 * qw // LANES, tm, LANES), lambda i: (0, i, 0)),
            pl.BlockSpec((tm, 2 * GROUP_W), lambda i: (i, 0)),
        ],
        out_shape=[
            jax.ShapeDtypeStruct((qw // LANES, m, LANES), F32),
            jax.ShapeDtypeStruct((2 * qw // LANES, m, LANES), F32),
            jax.ShapeDtypeStruct((m, 2 * GROUP_W), F32),
        ],
        compiler_params=_params("arbitrary"),
        name="inproj",
    )(xp, xs, g, w)


def _attn_prompt_kernel(q_ref, kv_ref, o_ref, m_scr, l_scr, acc_scr):
    g = pl.program_id(1)
    seq = q_ref.shape[1]

    @pl.when(g == 0)
    def _():
        m_scr[...] = jnp.full(m_scr.shape, NEG_INF, F32)
        l_scr[...] = jnp.zeros(l_scr.shape, F32)
        acc_scr[...] = jnp.zeros(acc_scr.shape, F32)

    qi = lax.broadcasted_iota(jnp.int32, (NK, 2 * NK), 0)
    kc = lax.broadcasted_iota(jnp.int32, (NK, 2 * NK), 1)
    band = (kc >= qi) & (kc <= qi + NK)
    lane = lax.broadcasted_iota(jnp.int32, (NK, LANES), 1)
    lane_kv = lax.broadcasted_iota(jnp.int32, (2 * NK, LANES), 1)
    ones_kv = jnp.ones((2 * NK, LANES), BF16)
    n_chunks = GROUP_W // LANES
    hpc = LANES // HEAD_DIM

    def block(blk, d, last):
        span = NK * d
        base = pl.multiple_of(blk * span, span)
        pbase = pl.multiple_of(jnp.maximum(blk - 1, 0) * span, span)
        mask = band & ((blk > 0) | (kc >= NK))
        for r in range(d):
            rows = pl.ds(base + r, NK, stride=d) if d > 1 else pl.ds(base, NK)
            prows = pl.ds(pbase + r, NK, stride=d) if d > 1 else pl.ds(pbase, NK)
            m_old = m_scr[rows, :]
            l_old = l_scr[rows, :]
            m_new, l_new = m_old, l_old
            for c in range(GROUP_W // LANES):
                qc = q_ref[c, rows, :].astype(BF16)
                kc_ = jnp.concatenate([kv_ref[c, prows, :], kv_ref[c, rows, :]], axis=0).astype(BF16)
                vc_ = jnp.concatenate([kv_ref[n_chunks + c, prows, :], kv_ref[n_chunks + c, rows, :]],
                                      axis=0).astype(BF16)
                acc = acc_scr[c, rows, :]
                scale, denom, contrib = None, None, None
                for hh in range(hpc):
                    h = c * hpc + hh
                    sel_q = (lane >= hh * HEAD_DIM) & (lane < (hh + 1) * HEAD_DIM)
                    sel_v = (lane_kv >= hh * HEAD_DIM) & (lane_kv < (hh + 1) * HEAD_DIM)
                    s = lax.dot_general(jnp.where(sel_q, qc, 0), kc_, (((1,), (1,)), ((), ())),
                                        preferred_element_type=F32)
                    s = jnp.where(mask, s, NEG_INF)
                    mo = m_old[:, h:h + 1]
                    mn = jnp.maximum(mo, jnp.max(s, axis=-1, keepdims=True))
                    p = jnp.exp(s - mn)
                    alpha = jnp.exp(mo - mn)
                    pv = _dot(p.astype(BF16), jnp.concatenate([jnp.where(sel_v, vc_, 0), ones_kv], axis=1))
                    ln = alpha * l_old[:, h:h + 1] + pv[:, LANES:]
                    contrib = pv[:, :LANES] if contrib is None else contrib + pv[:, :LANES]
                    scale = alpha if scale is None else jnp.where(sel_q, alpha, scale)
                    denom = ln if denom is None else jnp.where(sel_q, ln, denom)
                    m_new = jnp.where(lane == h, mn, m_new)
                    l_new = jnp.where(lane == h, ln, l_new)
                acc = scale * acc + contrib
                if last:
                    o_ref[c, rows, :] = acc / denom
                else:
                    acc_scr[c, rows, :] = acc
            if not last:
                m_scr[rows, :] = m_new
                l_scr[rows, :] = l_new

    for gi, (window, d) in enumerate(DIL_GROUPS):
        @pl.when(g == gi)
        def _(d=d, last=(gi == N_GROUPS - 1)):
            def body(blk, carry):
                block(blk, d, last)
                return carry
            n_blocks = seq // (NK * d)
            lax.fori_loop(0, n_blocks, body, 0, unroll=max(1, min(n_blocks, ATTN_BLOCKS_PER_BODY // d)))


def _attn_prompt(q, kv, batch, seq):
    assert seq % (NK * max(d for _, d in DIL_GROUPS)) == 0
    return pl.pallas_call(
        _attn_prompt_kernel,
        grid=(batch, N_GROUPS),
        in_specs=[
            pl.BlockSpec((GROUP_W // LANES, seq, LANES), lambda b, g: (g, b, 0)),
            pl.BlockSpec((2 * GROUP_W // LANES, seq, LANES), lambda b, g: (g, b, 0)),
        ],
        out_specs=pl.BlockSpec((GROUP_W // LANES, seq, LANES), lambda b, g: (0, b, 0)),
        out_shape=jax.ShapeDtypeStruct((GROUP_W // LANES, batch * seq, LANES), F32),
        scratch_shapes=[
            pltpu.VMEM((seq, LANES), F32),
            pltpu.VMEM((seq, LANES), F32),
            pltpu.VMEM((GROUP_W // LANES, seq, LANES), F32),
        ],
        compiler_params=_params("arbitrary", "arbitrary", vmem_limit=ATTN_VMEM_LIMIT),
        name="attn_prompt",
    )(q, kv)


def _mix_sample_kernel(q_ref, kn_ref, vn_ref, gbu_ref, st_ref, c0_ref, c1_ref, c2_ref, w_ref,
                       yb_ref, ya_ref, ns_ref):
    sb, t_len = q_ref.shape[0], q_ref.shape[3]
    w = w_ref[...]
    caches = (c0_ref, c1_ref, c2_ref)
    nt_dims = (((1,), (1,)), ((), ()))
    qn = lax.broadcasted_iota(jnp.int32, (t_len, t_len), 0)
    tn = lax.broadcasted_iota(jnp.int32, (t_len, t_len), 1)

    for s in range(sb):
        for h in range(HEADS):
            sc_parts, v_parts = [], []
            for gi, (window, d) in enumerate(DIL_GROUPS):
                c_ref = caches[gi]
                n_pos = c_ref.shape[-1]
                qg = q_ref[s, gi, h].astype(BF16)
                qc = lax.broadcasted_iota(jnp.int32, (t_len, n_pos), 0)
                pc = lax.broadcasted_iota(jnp.int32, (t_len, n_pos), 1)
                sc = _dot(qg, c_ref[s, 0, h].astype(BF16))
                sc_parts.append(jnp.where((((pc - qc) & (d - 1)) == 0) & (pc >= qc), sc, NEG_INF))
                v_parts.append((c_ref[s, 1, h].astype(BF16), True))
                sn = lax.dot_general(qg, kn_ref[s, gi, h].astype(BF16), nt_dims, preferred_element_type=F32)
                sc_parts.append(jnp.where((tn <= qn) & (((qn - tn) & (d - 1)) == 0), sn, NEG_INF))
                v_parts.append((vn_ref[s, gi, h].astype(BF16), False))
            mx = functools.reduce(jnp.maximum, [jnp.max(p, axis=-1, keepdims=True) for p in sc_parts])
            den = jnp.zeros((t_len, 1), F32)
            acc = jnp.zeros((t_len, HEAD_DIM), F32)
            for sc, (v, transposed) in zip(sc_parts, v_parts):
                p = jnp.exp(sc - mx)
                den = den + jnp.sum(p, axis=-1, keepdims=True)
                pb = p.astype(BF16)
                acc = acc + (lax.dot_general(pb, v, nt_dims, preferred_element_type=F32) if transposed
                             else _dot(pb, v))
            yb_ref[s, h] = acc / den
        gbu = gbu_ref[s]
        gb = gbu[:, 0:GROUP_W]
        u = gbu[:, GROUP_W:]
        st = st_ref[s]
        ext = [st[0:1], st[1:2]] + [u[t:t + 1] for t in range(t_len)]
        for t in range(t_len):
            conv = w[0:1] * ext[t] + w[1:2] * ext[t + 1] + w[2:3] * ext[t + 2]
            ya_ref[s, t:t + 1, :] = gb[t:t + 1] * conv
        for r in range(A_KERNEL - 1):
            ns_ref[s, r:r + 1, :] = ext[t_len + r]


def _mix_sample(q5, kn, vn, gbu3, st, c0, c1, c2, w):
    n, t_len = q5.shape[0], q5.shape[3]
    sb = ATTN_SEQ_BLOCK
    assert t_len >= A_KERNEL - 1 and n % sb == 0
    for c, (window, d) in zip((c0, c1, c2), DIL_GROUPS):
        assert c.shape[-1] == NK * d and d & (d - 1) == 0
    qspec = pl.BlockSpec((sb, N_GROUPS, HEADS, t_len, HEAD_DIM), lambda i: (i, 0, 0, 0, 0))
    cspec = lambda c: pl.BlockSpec((sb,) + c.shape[1:], lambda i: (i, 0, 0, 0, 0))
    return pl.pallas_call(
        _mix_sample_kernel,
        grid=(n // sb,),
        in_specs=[
            qspec, qspec, qspec,
            pl.BlockSpec((sb, t_len, 2 * GROUP_W), lambda i: (i, 0, 0)),
            pl.BlockSpec((sb, A_KERNEL - 1, GROUP_W), lambda i: (i, 0, 0)),
            cspec(c0), cspec(c1), cspec(c2),
            pl.BlockSpec((A_KERNEL, GROUP_W), lambda i: (0, 0)),
        ],
        out_specs=[
            pl.BlockSpec((sb, HEADS, t_len, HEAD_DIM), lambda i: (i, 0, 0, 0)),
            pl.BlockSpec((sb, t_len, GROUP_W), lambda i: (i, 0, 0)),
            pl.BlockSpec((sb, A_KERNEL - 1, GROUP_W), lambda i: (i, 0, 0)),
        ],
        out_shape=[
            jax.ShapeDtypeStruct((n, HEADS, t_len, HEAD_DIM), F32),
            jax.ShapeDtypeStruct((n, t_len, GROUP_W), F32),
            jax.ShapeDtypeStruct((n, A_KERNEL - 1, GROUP_W), F32),
        ],
        compiler_params=_params("arbitrary"),
        name="mix_sample",
    )(q5, kn, vn, gbu3, st, c0, c1, c2, w)


def _route(h2, wr_ref, br_ref, xs_ref, rinfo_ref, gtab_ref):
    tm = h2.shape[0]
    h_hi = h2.astype(BF16)
    h_lo = (h2 - h_hi.astype(F32)).astype(BF16)
    nt_dims = (((1,), (1,)), ((), ()))
    w_hi, w_lo = wr_ref[0], wr_ref[1]
    lt = (lax.dot_general(w_hi, h_hi, nt_dims, preferred_element_type=F32)
          + lax.dot_general(w_hi, h_lo, nt_dims, preferred_element_type=F32)
          + lax.dot_general(w_lo, h_hi, nt_dims, preferred_element_type=F32)) + br_ref[...]
    rg = lax.broadcasted_iota(jnp.int32, (SUBLANES, tm), 0)
    gl = jnp.where(rg < N_EXPERT_GROUPS, lt[0:SUBLANES], NEG_INF)
    ge = jnp.exp(gl - jnp.max(gl, axis=0, keepdims=True))
    gp = ge / jnp.sum(ge, axis=0, keepdims=True)
    g_val = jnp.max(gp, axis=0, keepdims=True)
    g_idx = jnp.min(jnp.where(gp == g_val, rg, SUBLANES), axis=0, keepdims=True)
    re = lax.broadcasted_iota(jnp.int32, (N_EXPERTS, tm), 0)
    sel = (re // EXPERTS_PER_GROUP) == g_idx
    el = jnp.where(sel, lt[ROUTER_EXPERT_ROW0:ROUTER_EXPERT_ROW0 + N_EXPERTS], NEG_INF)
    ee = jnp.exp(el - jnp.max(el, axis=0, keepdims=True))
    ep = jnp.where(sel, ee / jnp.sum(ee, axis=0, keepdims=True), -1.0)
    v1 = jnp.max(ep, axis=0, keepdims=True)
    i1 = jnp.min(jnp.where(ep == v1, re, N_EXPERTS), axis=0, keepdims=True)
    ep2 = jnp.where(re == i1, -1.0, ep)
    v2 = jnp.max(ep2, axis=0, keepdims=True)
    i2 = jnp.min(jnp.where(ep2 == v2, re, N_EXPERTS), axis=0, keepdims=True)
    scale = g_val / (v1 + v2)

    oh = [(re == i1).astype(F32), (re == i2).astype(F32)]
    cmat = (oh[0] + oh[1]).astype(BF16)
    earlier = (lax.broadcasted_iota(jnp.int32, (tm, tm), 0)
               < lax.broadcasted_iota(jnp.int32, (tm, tm), 1)).astype(BF16)
    before = _dot(cmat, earlier)
    cnt_row = lax.dot_general(jnp.ones((SUBLANES, tm), BF16), cmat, nt_dims, preferred_element_type=F32)
    grp_row = ((cnt_row.astype(jnp.int32) + (GROUP_ROWS - 1)) // GROUP_ROWS).astype(F32)
    lower = (lax.broadcasted_iota(jnp.int32, (N_EXPERTS, N_EXPERTS), 0)
             < lax.broadcasted_iota(jnp.int32, (N_EXPERTS, N_EXPERTS), 1)).astype(BF16)
    goff_row = _dot(grp_row.astype(BF16), lower)
    slot_rows = lax.broadcasted_iota(jnp.int32, (TILE_SLOTS, tm), 0)
    place = None
    pos = []
    for k in range(2):
        rank = jnp.sum(oh[k] * before, axis=0, keepdims=True)
        seg = _dot(goff_row.astype(BF16), oh[k].astype(BF16))[0:1]
        pos.append(seg * GROUP_ROWS + rank)
        hit = slot_rows == pos[k].astype(jnp.int32)
        place = hit if place is None else place | hit
    xs_ref[...] = _dot(jnp.where(place, 1.0, 0.0).astype(BF16), h_hi).astype(BF16)
    rl = lax.broadcasted_iota(jnp.int32, (LANES, tm), 0)
    info_t = jnp.where(rl == 0, v1 * scale, jnp.where(rl == 1, v2 * scale,
                       jnp.where(rl == 2, pos[0], jnp.where(rl == 3, pos[1], 0.0))))
    rinfo_ref[...] = info_t.T
    spread = (lax.broadcasted_iota(jnp.int32, (N_EXPERTS, LANES), 0)
              == lax.broadcasted_iota(jnp.int32, (N_EXPERTS, LANES), 1)).astype(BF16)
    gtab_ref[...] = _dot(grp_row.astype(BF16), spread).astype(jnp.int32)


ROUTER_EXPERT_ROW0 = SUBLANES
ROUTER_ROWS = 48
TILE_GROUPS = -(-(2 * ROW_TILE + (GROUP_ROWS - 1) * N_EXPERTS) // GROUP_ROWS)
TILE_SLOTS = TILE_GROUPS * GROUP_ROWS


def _router_specs(tm, d):
    ins = [pl.BlockSpec((1, d), lambda i: (0, 0)),
           pl.BlockSpec((2, ROUTER_ROWS, d), lambda i: (0, 0, 0)),
           pl.BlockSpec((ROUTER_ROWS, 1), lambda i: (0, 0))]
    outs = [pl.BlockSpec((TILE_SLOTS, d), lambda i: (i, 0)),
            pl.BlockSpec((tm, LANES), lambda i: (i, 0)),
            pl.BlockSpec((None, SUBLANES, LANES), lambda i: (i, 0, 0))]
    return ins, outs


def _router_shapes(m, d):
    nb = m // ROW_TILE
    return [jax.ShapeDtypeStruct((nb * TILE_SLOTS, d), BF16), jax.ShapeDtypeStruct((m, LANES), F32),
            jax.ShapeDtypeStruct((nb, SUBLANES, LANES), jnp.int32)]


def _outproj_kernel(n_prompt_tiles, tiles_per_seq, xp_ref, xs_ref, gbu_ref, halo_ref, cw_ref, yb_ref, ys_ref,
                    wo_ref, g_ref, wr_ref, br_ref, x1_ref, xs_out_ref, rinfo_ref, gtab_ref, conv_scr):
    i = pl.program_id(0)
    tm = xp_ref.shape[0]
    is_p = i < n_prompt_tiles
    x = jnp.where(is_p, xp_ref[...], xs_ref[...])
    u = gbu_ref[:, GROUP_W:]
    first = (i % tiles_per_seq) == 0
    conv_scr[0:SUBLANES, :] = jnp.where(first, 0.0, halo_ref[:, GROUP_W:])
    conv_scr[SUBLANES:, :] = u
    cw = cw_ref[...]
    ya = gbu_ref[:, 0:GROUP_W] * (cw[0:1, :] * conv_scr[pl.ds(SUBLANES - 2, tm), :]
                                  + cw[1:2, :] * conv_scr[pl.ds(SUBLANES - 1, tm), :] + cw[2:3, :] * u)
    ymix = jnp.where(is_p, jnp.concatenate([ya, _load_chunks(yb_ref)], axis=-1), ys_ref[...])
    x1 = x + _dot(ymix.astype(BF16), wo_ref[...])
    x1_ref[...] = x1
    _route(_rms(x1, g_ref[...]), wr_ref, br_ref, xs_out_ref, rinfo_ref, gtab_ref)


def _outproj(xp, xs, gbu, conv_w, yb_p, ymix_s, wo, g, wr, br, seq):
    mp, d = xp.shape
    ms = xs.shape[0]
    tm = ROW_TILE
    npt, nst = mp // tm, ms // tm
    m = mp + ms
    r_in, r_out = _router_specs(tm, d)
    pmap = lambda i: (jnp.minimum(i, npt - 1), 0)
    smap = lambda i: (jnp.maximum(i - npt, 0), 0)
    halo_map = lambda i: (jnp.maximum(jnp.minimum(i, npt - 1) * (tm // SUBLANES) - 1, 0), 0)
    return pl.pallas_call(
        functools.partial(_outproj_kernel, npt, seq // tm),
        grid=(npt + nst,),
        in_specs=[
            pl.BlockSpec((tm, d), pmap),
            pl.BlockSpec((tm, d), smap),
            pl.BlockSpec((tm, 2 * GROUP_W), pmap),
            pl.BlockSpec((SUBLANES, 2 * GROUP_W), halo_map),
            pl.BlockSpec((A_KERNEL, GROUP_W), lambda i: (0, 0)),
            pl.BlockSpec((GROUP_W // LANES, tm, LANES), lambda i: (0, jnp.minimum(i, npt - 1), 0)),
            pl.BlockSpec((tm, 2 * GROUP_W), smap),
            pl.BlockSpec((2 * GROUP_W, d), lambda i: (0, 0)),
        ] + r_in,
        out_specs=[pl.BlockSpec((tm, d), lambda i: (i, 0))] + r_out,
        out_shape=[jax.ShapeDtypeStruct((m, d), F32)] + _router_shapes(m, d),
        scratch_shapes=[pltpu.VMEM((tm + SUBLANES, GROUP_W), F32)],
        compiler_params=_params("arbitrary"),
        name="outproj_router",
    )(xp, xs, gbu, gbu, conv_w, yb_p, ymix_s, wo, g, wr, br)


def _experts_kernel(te_ref, src_ref, dst_ref, nt_ref, h_hbm, wg_ref, wu_ref, wd_ref, y_hbm, xbuf, ybuf, gsem, ssem,
                    *, trash_row):
    i = pl.program_id(0)
    nt = nt_ref[0]
    ng = CHUNK_GROUPS
    slot = i % 2
    other = 1 - slot

    def group_rows(r):
        return pl.ds(r if isinstance(r, int) else pl.multiple_of(r, GROUP_ROWS), GROUP_ROWS)

    def gather_copy(p, sl, j):
        return pltpu.make_async_copy(h_hbm.at[group_rows(src_ref[p]), :],
                                     xbuf.at[sl, group_rows(j * GROUP_ROWS), :], gsem.at[sl])

    def scatter_copy(p, sl, j):
        return pltpu.make_async_copy(ybuf.at[sl, group_rows(j * GROUP_ROWS), :],
                                     y_hbm.at[group_rows(dst_ref[p]), :], ssem.at[sl])

    def wait_gather(sl):
        pltpu.make_async_copy(h_hbm.at[pl.ds(0, ng * GROUP_ROWS), :], xbuf.at[sl], gsem.at[sl]).wait()

    def wait_scatter(sl):
        pltpu.make_async_copy(ybuf.at[sl], y_hbm.at[pl.ds(0, ng * GROUP_ROWS), :], ssem.at[sl]).wait()

    @pl.when(i == 0)
    def _():
        ybuf[...] = jnp.zeros(ybuf.shape, BF16)
        init = pltpu.make_async_copy(ybuf.at[0], y_hbm.at[pl.ds(trash_row, ng * GROUP_ROWS), :], ssem.at[0])
        init.start()
        init.wait()

        def tail_copy(b):
            first = pl.multiple_of(b * TILE_SLOTS + 2 * ROW_TILE, GROUP_ROWS)
            return pltpu.make_async_copy(ybuf.at[0, pl.ds(0, TILE_SLOTS - 2 * ROW_TILE), :],
                                         y_hbm.at[pl.ds(first, TILE_SLOTS - 2 * ROW_TILE), :], ssem.at[0])

        def start_body(b, c):
            tail_copy(b).start()
            return c

        def wait_body(b, c):
            tail_copy(b).wait()
            return c
        lax.fori_loop(0, trash_row // TILE_SLOTS, start_body, 0)
        lax.fori_loop(0, trash_row // TILE_SLOTS, wait_body, 0)
        for j in range(ng):
            gather_copy(j, 0, j).start()

    @pl.when(i < nt)
    def _():
        wait_gather(slot)

        @pl.when(i >= 1)
        def _():
            wait_scatter(slot)

        base = i * ng
        for j in range(ng):
            gather_copy(base + ng + j, other, j).start(priority=j % 2)
            scatter_copy(base + j, other, j).start(priority=(j + 1) % 2)
        x = xbuf[slot]
        gate = _dot(x, wg_ref[...].astype(BF16))
        up = _dot(x, wu_ref[...].astype(BF16))
        act = (gate * jax.nn.sigmoid(gate) * up).astype(BF16)
        ybuf[slot] = _dot(act, wd_ref[...].astype(BF16)).astype(BF16)

    @pl.when(i == nt)
    def _():
        wait_gather(slot)
        wait_scatter(slot)
        for j in range(ng):
            scatter_copy(i * ng + j, other, j).start()
        wait_scatter(other)


def _experts(xs, tile_expert, src, dst, n_tiles, wg, wu, wd, layer):
    ng = CHUNK_GROUPS
    nt_max = tile_expert.shape[0]
    d, f = wg.shape[2], wg.shape[3]
    n_rows = xs.shape[0] + ng * GROUP_ROWS
    wmap = lambda i, te_ref, src_ref, dst_ref, nt_ref: (layer, te_ref[i], 0, 0)
    grid_spec = pltpu.PrefetchScalarGridSpec(
        num_scalar_prefetch=4,
        grid=(nt_max,),
        in_specs=[
            pl.BlockSpec(memory_space=pl.ANY),
            pl.BlockSpec((None, None, d, f), wmap),
            pl.BlockSpec((None, None, d, f), wmap),
            pl.BlockSpec((None, None, f, d), wmap),
        ],
        out_specs=pl.BlockSpec(memory_space=pl.ANY),
        scratch_shapes=[
            pltpu.VMEM((2, ng * GROUP_ROWS, d), BF16),
            pltpu.VMEM((2, ng * GROUP_ROWS, d), BF16),
            pltpu.SemaphoreType.DMA((2,)),
            pltpu.SemaphoreType.DMA((2,)),
        ],
    )
    return pl.pallas_call(
        functools.partial(_experts_kernel, trash_row=xs.shape[0]),
        grid_spec=grid_spec,
        out_shape=jax.ShapeDtypeStruct((n_rows, d), BF16),
        compiler_params=_params("arbitrary"),
        name="experts",
    )(tile_expert, src, dst, n_tiles, xs, wg, wu, wd)


def _plan_chunks(gtab):
    ng = CHUNK_GROUPS
    grp = gtab[:, 0, :N_EXPERTS]
    nb = grp.shape[0]
    nt_max = -(-nb * TILE_GROUPS // ng) + N_EXPERTS + 1
    total = jnp.sum(grp, axis=0)
    chunks = (total + ng - 1) // ng
    chunk_end = jnp.cumsum(chunks)
    chunk_start = chunk_end - chunks
    n_chunks = chunk_end[-1]
    seg_off = jnp.cumsum(grp, axis=1) - grp
    seg_end = jnp.cumsum(grp, axis=0)
    seg_start = seg_end - grp
    cidx = jnp.arange(nt_max, dtype=jnp.int32)
    chunk_expert = jnp.sum((cidx[:, None] >= chunk_end[None, :]).astype(jnp.int32), axis=1)
    last_expert = jnp.max(jnp.where(total > 0, jnp.arange(N_EXPERTS), 0))
    chunk_expert = jnp.where(cidx < n_chunks, jnp.minimum(chunk_expert, N_EXPERTS - 1), last_expert).astype(jnp.int32)
    is_e = chunk_expert[:, None] == jnp.arange(N_EXPERTS, dtype=jnp.int32)[None, :]
    pick = lambda per_expert: jnp.sum(jnp.where(is_e, per_expert[None, :], 0), axis=1)
    pick_rows = lambda table: jnp.sum(jnp.where(is_e[:, :, None], table.T[None, :, :], 0), axis=1)
    within = (cidx - pick(chunk_start))[:, None] * ng + jnp.arange(ng, dtype=jnp.int32)[None, :]
    valid = (within < pick(total)[:, None]) & (cidx < n_chunks)[:, None]
    ends = pick_rows(seg_end)
    base = pick_rows(jnp.arange(nb, dtype=jnp.int32)[:, None] * TILE_GROUPS + seg_off - seg_start)
    step = jnp.concatenate([base[:, :1], base[:, 1:] - base[:, :-1]], axis=1)
    passed = jnp.concatenate([jnp.ones((nt_max, ng, 1), bool), within[:, :, None] >= ends[:, None, :-1]], axis=2)
    group = within + jnp.sum(jnp.where(passed, step[:, None, :], 0), axis=2)
    pad_dst = (nb * TILE_GROUPS + jnp.arange(ng, dtype=jnp.int32)) * GROUP_ROWS
    src = jnp.where(valid, group * GROUP_ROWS, 0).astype(jnp.int32).reshape(nt_max * ng)
    dst = jnp.where(valid, group * GROUP_ROWS, pad_dst[None, :]).astype(jnp.int32).reshape(nt_max * ng)
    dst = jnp.concatenate([pad_dst, dst])
    return chunk_expert, src, dst, n_chunks.astype(jnp.int32).reshape(1)


def _combine(x_ref, y_ref, rinfo_ref):
    tm = x_ref.shape[0]
    n_slots = y_ref.shape[0]
    r = rinfo_ref[...]
    col = lax.broadcasted_iota(jnp.int32, (tm, n_slots), 1)
    pick = (jnp.where(col == r[:, 2:3].astype(jnp.int32), r[:, 0:1], 0.0)
            + jnp.where(col == r[:, 3:4].astype(jnp.int32), r[:, 1:2], 0.0))
    return x_ref[...] + _dot(pick.astype(BF16), y_ref[...])


def _combine_specs(tm, d):
    return [
        pl.BlockSpec((tm, d), lambda i: (i, 0)),
        pl.BlockSpec((TILE_SLOTS, d), lambda i: (i, 0)),
        pl.BlockSpec((tm, LANES), lambda i: (i, 0)),
    ]


def _combine_pw1_kernel(x_ref, y_ref, rinfo_ref, g_ref, w_ref, b_ref, x2_ref, u_ref):
    x2 = _combine(x_ref, y_ref, rinfo_ref)
    x2_ref[...] = x2
    d = x2.shape[1]
    z = _dot(_rms(x2, g_ref[...]).astype(BF16), w_ref[...]) + b_ref[...]
    u_ref[...] = z[:, :d] * jax.nn.sigmoid(z[:, d:])


def _combine_pw1(x1, y2, rinfo, g, w, b):
    m, d = x1.shape
    tm = ROW_TILE
    return pl.pallas_call(
        _combine_pw1_kernel,
        grid=(m // tm,),
        in_specs=_combine_specs(tm, d) + [
            pl.BlockSpec((1, d), lambda i: (0, 0)),
            pl.BlockSpec((d, 2 * d), lambda i: (0, 0)),
            pl.BlockSpec((1, 2 * d), lambda i: (0, 0)),
        ],
        out_specs=[pl.BlockSpec((tm, d), lambda i: (i, 0)), pl.BlockSpec((tm, d), lambda i: (i, 0))],
        out_shape=[jax.ShapeDtypeStruct((m, d), F32), jax.ShapeDtypeStruct((m, d), F32)],
        compiler_params=_params("arbitrary"),
        name="combine_pw1",
    )(x1, y2, rinfo, g, w, b)


def _combine_final_kernel(n_prompt_tiles, x_ref, y_ref, rinfo_ref, g_ref, yp_ref, ys_ref):
    i = pl.program_id(0)
    y = _rms(_combine(x_ref, y_ref, rinfo_ref), g_ref[...])

    @pl.when(i < n_prompt_tiles)
    def _():
        yp_ref[...] = y

    @pl.when(i >= n_prompt_tiles)
    def _():
        ys_ref[...] = y


def _combine_final(x3, y2, rinfo, g, mp):
    m, d = x3.shape
    tm = ROW_TILE
    npt = mp // tm
    return pl.pallas_call(
        functools.partial(_combine_final_kernel, npt),
        grid=(m // tm,),
        in_specs=_combine_specs(tm, d) + [pl.BlockSpec((1, d), lambda i: (0, 0))],
        out_specs=[pl.BlockSpec((tm, d), lambda i: (jnp.minimum(i, npt - 1), 0)),
                   pl.BlockSpec((tm, d), lambda i: (jnp.maximum(i - npt, 0), 0))],
        out_shape=[jax.ShapeDtypeStruct((mp, d), F32), jax.ShapeDtypeStruct((m - mp, d), F32)],
        compiler_params=_params("arbitrary"),
        name="combine_final",
    )(x3, y2, rinfo, g)


def _dwconv_prompt_tile(first, u_ref, halo_ref, w_ref, b_ref, scr):
    tm = u_ref.shape[0]
    scr[0, 0:CONV_HALO, :] = jnp.where(first, 0.0, halo_ref[...])
    scr[0, CONV_HALO:, :] = u_ref[...]
    n_shift = tm + CONV_HALO - SUBLANES
    for s in range(1, SUBLANES):
        scr[s, 0:n_shift, :] = scr[0, pl.ds(s, n_shift), :]
    acc = jnp.broadcast_to(b_ref[...], u_ref.shape)
    for k in range(C_KERNEL):
        off = CONV_HALO - (C_KERNEL - 1) + k
        acc = acc + w_ref[k:k + 1, :] * scr[off % SUBLANES, pl.ds(off - off % SUBLANES, tm), :]
    return acc


def _dwconv_sample_kernel(st_ref, u_ref, w_ref, b_ref, c_ref, ns_ref):
    n_state, t_len = st_ref.shape[0], u_ref.shape[0]
    rows = [st_ref[r] for r in range(n_state)] + [u_ref[t] for t in range(t_len)]
    for t in range(t_len):
        acc = jnp.broadcast_to(b_ref[...], rows[0].shape)
        for k in range(C_KERNEL):
            acc = acc + w_ref[k:k + 1, :] * rows[t + k]
        c_ref[t] = acc
    for r in range(n_state):
        ns_ref[r] = rows[r + t_len]


def _dwconv_sample(st, u3, w, b):
    n_state, n, d = st.shape
    t_len = u3.shape[0]
    sb = SEQ_BLOCK
    assert n_state == C_KERNEL - 1 and n % sb == 0
    return pl.pallas_call(
        _dwconv_sample_kernel,
        grid=(n // sb,),
        in_specs=[
            pl.BlockSpec((n_state, sb, d), lambda i: (0, i, 0)),
            pl.BlockSpec((t_len, sb, d), lambda i: (0, i, 0)),
            pl.BlockSpec((C_KERNEL, d), lambda i: (0, 0)),
            pl.BlockSpec((1, d), lambda i: (0, 0)),
        ],
        out_specs=[
            pl.BlockSpec((t_len, sb, d), lambda i: (0, i, 0)),
            pl.BlockSpec((n_state, sb, d), lambda i: (0, i, 0)),
        ],
        out_shape=[jax.ShapeDtypeStruct((t_len, n, d), F32), jax.ShapeDtypeStruct((n_state, n, d), F32)],
        compiler_params=_params("arbitrary"),
        name="dwconv_sample",
    )(st, u3, w, b)


def _conf_tail_kernel(n_prompt_tiles, tiles_per_seq, x_ref, u_ref, halo_ref, dww_ref, dwb_ref, cs_ref,
                      lg_ref, lb_ref, w_ref, b_ref, g_ref, wr_ref, br_ref,
                      x3_ref, xs_out_ref, rinfo_ref, gtab_ref, conv_scr):
    i = pl.program_id(0)
    c_p = _dwconv_prompt_tile((i % tiles_per_seq) == 0, u_ref, halo_ref, dww_ref, dwb_ref, conv_scr)
    c = jnp.where(i < n_prompt_tiles, c_p, cs_ref[...])
    mu = jnp.mean(c, axis=-1, keepdims=True)
    cc = c - mu
    var = jnp.mean(cc * cc, axis=-1, keepdims=True)
    y = cc * lax.rsqrt(var + LN_EPS) * lg_ref[...] + lb_ref[...]
    y = y * jax.nn.sigmoid(y)
    x3 = x_ref[...] + _dot(y.astype(BF16), w_ref[...]) + b_ref[...]
    x3_ref[...] = x3
    _route(_rms(x3, g_ref[...]), wr_ref, br_ref, xs_out_ref, rinfo_ref, gtab_ref)


def _conf_tail(x2, u, dw_w, dw_b, c_s, ln_g, ln_b, w, b, g, wr, br, mp, seq):
    m, d = x2.shape
    tm = ROW_TILE
    npt = mp // tm
    r_in, r_out = _router_specs(tm, d)
    vec = pl.BlockSpec((1, d), lambda i: (0, 0))
    ptile = lambda i: jnp.minimum(i, npt - 1)
    return pl.pallas_call(
        functools.partial(_conf_tail_kernel, npt, seq // tm),
        grid=(m // tm,),
        in_specs=[
            pl.BlockSpec((tm, d), lambda i: (i, 0)),
            pl.BlockSpec((tm, d), lambda i: (ptile(i), 0)),
            pl.BlockSpec((CONV_HALO, d), lambda i: (jnp.maximum(ptile(i) * (tm // CONV_HALO) - 1, 0), 0)),
            pl.BlockSpec((C_KERNEL, d), lambda i: (0, 0)),
            vec,
            pl.BlockSpec((tm, d), lambda i: (jnp.maximum(i - npt, 0), 0)),
            vec, vec,
            pl.BlockSpec((d, d), lambda i: (0, 0)),
            vec,
        ] + r_in,
        out_specs=[pl.BlockSpec((tm, d), lambda i: (i, 0))] + r_out,
        out_shape=[jax.ShapeDtypeStruct((m, d), F32)] + _router_shapes(m, d),
        scratch_shapes=[pltpu.VMEM((SUBLANES, tm + CONV_HALO, d), F32)],
        compiler_params=_params("arbitrary"),
        name="conf_tail_router",
    )(x2, u, u, dw_w, dw_b, c_s, ln_g, ln_b, w, b, g, wr, br)


def _router_weights(w_rg, b_rg, w_re, b_re):
    d = w_rg.shape[0]
    gpad = ROUTER_EXPERT_ROW0 - N_EXPERT_GROUPS
    epad = ROUTER_ROWS - ROUTER_EXPERT_ROW0 - N_EXPERTS
    wt = jnp.concatenate([w_rg.T, jnp.zeros((gpad, d), F32), w_re.T, jnp.zeros((epad, d), F32)], axis=0)
    w_hi = wt.astype(BF16)
    w_lo = (wt - w_hi.astype(F32)).astype(BF16)
    br = jnp.concatenate([b_rg, jnp.zeros((gpad,), F32), b_re, jnp.zeros((epad,), F32)])[:, None]
    return jnp.stack([w_hi, w_lo]), br


def _moe(xs, gtab, wg, wu, wd, layer):
    chunk_expert, src, dst, n_chunks = _plan_chunks(gtab)
    return _experts(xs, chunk_expert, src, dst, n_chunks, wg, wu, wd, layer)


def kernel(x_prompt, x_sample, state_shortconv, cache_kv_w128, cache_kv_w512, cache_kv_w2048, state_conformer,
           g_mix, g_ffn, g_final, w_in, conv_a_w, w_out, w_pw1, b_pw1, dw_w, dw_b, ln_g, ln_b, w_pw2, b_pw2,
           w_router_group, b_router_group, w_router_expert, b_router_expert, w_gate, w_up, w_down):
    batch, seq, d = x_prompt.shape
    n_dec, t_dec, _ = x_sample.shape
    mp, ms = batch * seq, n_dec * t_dec
    assert g_mix.shape[0] == 2 and mp % ROW_TILE == 0 and ms % ROW_TILE == 0 and seq % ROW_TILE == 0
    xp = x_prompt.reshape(mp, d)
    xs = x_sample.reshape(ms, d)
    slab = (HEADS, HEAD_DIM)

    aw, qw = 3 * GROUP_W, N_GROUPS * GROUP_W
    w0 = w_in[0]
    kcols = w0[:, aw + qw:aw + 2 * qw].reshape(d, N_GROUPS, GROUP_W)
    vcols = w0[:, aw + 2 * qw:].reshape(d, N_GROUPS, GROUP_W)
    w_perm = jnp.concatenate([w0[:, :aw + qw], jnp.stack([kcols, vcols], axis=2).reshape(d, 2 * qw)], axis=1)
    q, kv, gbu = _inproj(xp, xs, g_mix[0][None, :], w_perm.astype(BF16))

    yb_p = _attn_prompt(q, kv, batch, seq)

    hpc = LANES // HEAD_DIM
    q_s = jnp.transpose(q[:, mp:].reshape(N_GROUPS, HEADS // hpc, n_dec, t_dec, hpc, HEAD_DIM),
                        (2, 0, 1, 4, 3, 5)).reshape(n_dec, N_GROUPS, HEADS, t_dec, HEAD_DIM)
    kv_s = jnp.transpose(kv[:, mp:].reshape(N_GROUPS, 2, HEADS // hpc, n_dec, t_dec, hpc, HEAD_DIM),
                         (1, 3, 0, 2, 5, 4, 6)).reshape(2, n_dec, N_GROUPS, HEADS, t_dec, HEAD_DIM)
    gbu_s = gbu[mp:].reshape(n_dec, t_dec, 2 * GROUP_W)
    c0, c1, c2 = (jnp.transpose(c[0], (0, 2, 3, 4, 1)) for c in (cache_kv_w128, cache_kv_w512, cache_kv_w2048))
    yb_s, ya_s, s_sc = _mix_sample(q_s, kv_s[0], kv_s[1], gbu_s, state_shortconv[0], c0, c1, c2, conv_a_w[0])
    ymix_s = jnp.concatenate([ya_s.reshape(ms, GROUP_W),
                              jnp.transpose(yb_s, (0, 2, 1, 3)).reshape(ms, GROUP_W)], axis=-1)

    wr0, br0 = _router_weights(w_router_group[0], b_router_group[0], w_router_expert[0], b_router_expert[0])
    x1, xsort0, rinfo0, gtab0 = _outproj(xp, xs, gbu, conv_a_w[0], yb_p, ymix_s, w_out[0].astype(BF16),
                                         g_ffn[0][None, :], wr0, br0, seq)
    y2 = _moe(xsort0, gtab0, w_gate, w_up, w_down, 0)

    x2, u = _combine_pw1(x1, y2, rinfo0, g_mix[1][None, :], w_pw1[0].astype(BF16), b_pw1[0][None, :])
    c_s3, s_cf = _dwconv_sample(jnp.transpose(state_conformer[0], (1, 0, 2)),
                                jnp.transpose(u[mp:].reshape(n_dec, t_dec, d), (1, 0, 2)), dw_w[0], dw_b[0][None, :])
    c_s = jnp.transpose(c_s3, (1, 0, 2)).reshape(ms, d)
    wr1, br1 = _router_weights(w_router_group[1], b_router_group[1], w_router_expert[1], b_router_expert[1])
    x3, xsort1, rinfo1, gtab1 = _conf_tail(x2, u, dw_w[0], dw_b[0][None, :], c_s, ln_g[0][None, :], ln_b[0][None, :],
                                           w_pw2[0].astype(BF16), b_pw2[0][None, :], g_ffn[1][None, :], wr1, br1,
                                           mp, seq)
    y2b = _moe(xsort1, gtab1, w_gate, w_up, w_down, 1)
    y_p, y_s = _combine_final(x3, y2b, rinfo1, g_final[None, :], mp)

    cpg = 2 * GROUP_W // LANES

    def tail_rows(arr, n, axis):
        return jnp.stack([lax.slice_in_dim(arr, (b + 1) * seq - n, (b + 1) * seq, axis=axis) for b in range(batch)])

    p_kv = []
    for gi, (w, _) in enumerate(DIL_GROUPS):
        rows = tail_rows(kv[gi * cpg:(gi + 1) * cpg], min(w, seq), 1)
        p_kv.append(jnp.transpose(rows, (0, 2, 1, 3)).reshape(1, batch, min(w, seq), 2, *slab))
    s_kv = [jnp.transpose(kv_s[:, :, gi], (1, 3, 0, 2, 4))[None] for gi in range(N_GROUPS)]
    p_sc = tail_rows(gbu, A_KERNEL - 1, 0)[:, :, GROUP_W:][None]
    p_cf = tail_rows(u, C_KERNEL - 1, 0)[None]
    return (y_p.reshape(batch, seq, d), y_s.reshape(n_dec, t_dec, d), p_sc, p_kv[0], p_kv[1], p_kv[2], p_cf,
            s_sc[None], s_kv[0], s_kv[1], s_kv[2], jnp.transpose(s_cf, (1, 0, 2))[None])
```

```python
import functools

import jax
import jax.numpy as jnp
from jax import lax
from jax.experimental import pallas as pl
from jax.experimental.pallas import tpu as pltpu

F32 = jnp.float32
BF16 = jnp.bfloat16

DIL_GROUPS = ((128, 1), (512, 4), (2048, 16))
N_GROUPS = len(DIL_GROUPS)
HEADS = 4
HEAD_DIM = 64
GROUP_W = HEADS * HEAD_DIM
NK = DIL_GROUPS[0][0] // DIL_GROUPS[0][1]
A_KERNEL = 3
C_KERNEL = 31
N_EXPERT_GROUPS = 4
EXPERTS_PER_GROUP = 8
N_EXPERTS = N_EXPERT_GROUPS * EXPERTS_PER_GROUP
RMS_EPS = 1e-6
LN_EPS = 1e-5
NEG_INF = -1e30

SUBLANES = 8
LANES = 128
VMEM_LIMIT = 48 * 1024 * 1024
ATTN_VMEM_LIMIT = 60 * 1024 * 1024
ATTN_BLOCKS_PER_BODY = 8

ROW_TILE = 256
GROUP_ROWS = 16
CHUNK_GROUPS = 32
CONV_HALO = 32
SEQ_BLOCK = 8
ATTN_SEQ_BLOCK = 2


def _params(*sem, vmem_limit=VMEM_LIMIT):
    return pltpu.CompilerParams(dimension_semantics=sem, vmem_limit_bytes=vmem_limit)


def _rms(x, g):
    return x * lax.rsqrt(jnp.mean(x * x, axis=-1, keepdims=True) + RMS_EPS) * g


def _dot(a, b):
    return jnp.dot(a, b, preferred_element_type=F32)


def _rows_to_lanes(ref, n_rows, base=None):
    parts = []
    for s in range(SUBLANES):
        idx = pl.ds(s, n_rows, stride=SUBLANES)
        parts.append(ref[idx, :] if base is None else ref[base, idx, :])
    return jnp.concatenate(parts, axis=-1)


def _lanes_to_rows(ref, val, base=None):
    n_rows = val.shape[0]
    for s in range(SUBLANES):
        idx = pl.ds(s, n_rows, stride=SUBLANES)
        piece = val[:, s * LANES:(s + 1) * LANES]
        if base is None:
            ref[idx, :] = piece
        else:
            ref[base, idx, :] = piece


def _inproj_kernel(n_prompt_tiles, xp_ref, xs_ref, g_ref, w_ref, q_ref, kv_ref, gbu_ref):
    i = pl.program_id(0)
    x = jnp.where(i < n_prompt_tiles, xp_ref[...], xs_ref[...])
    h = _rms(x, g_ref[...]).astype(BF16)
    aw = 3 * GROUP_W
    a = _dot(h, w_ref[:, 0:aw])
    gbu_ref[:, 0:GROUP_W] = a[:, 0:GROUP_W]
    gbu_ref[:, GROUP_W:2 * GROUP_W] = a[:, GROUP_W:2 * GROUP_W] * a[:, 2 * GROUP_W:aw]
    qw = N_GROUPS * GROUP_W
    _store_chunks(q_ref, _dot(h, w_ref[:, aw:aw + qw]) * (HEAD_DIM ** -0.5))
    _store_chunks(kv_ref, _dot(h, w_ref[:, aw + qw:]))


def _store_chunks(ref, val):
    for c in range(val.shape[1] // LANES):
        ref[c] = val[:, c * LANES:(c + 1) * LANES]


def _load_chunks(ref, rows=None):
    n = ref.shape[0]
    return jnp.concatenate([ref[c] if rows is None else ref[c, rows, :] for c in range(n)], axis=-1)


def _inproj(xp, xs, g, w):
    mp, d = xp.shape
    ms = xs.shape[0]
    tm = ROW_TILE
    npt, nst = mp // tm, ms // tm
    m = mp + ms
    ncols = w.shape[1]
    qw = N_GROUPS * GROUP_W
    return pl.pallas_call(
        functools.partial(_inproj_kernel, npt),
        grid=(npt + nst,),
        in_specs=[
            pl.BlockSpec((tm, d), lambda i: (jnp.minimum(i, npt - 1), 0)),
            pl.BlockSpec((tm, d), lambda i: (jnp.maximum(i - npt, 0), 0)),
            pl.BlockSpec((1, d), lambda i: (0, 0)),
            pl.BlockSpec((d, ncols), lambda i: (0, 0)),
        ],
        out_specs=[
            pl.BlockSpec((qw // LANES, tm, LANES), lambda i: (0, i, 0)),
            pl.BlockSpec((2 * qw // LANES, tm, LANES), lambda i: (0, i, 0)),
            pl.BlockSpec((tm, 2 * GROUP_W), lambda i: (i, 0)),
        ],
        out_shape=[
            jax.ShapeDtypeStruct((qw // LANES, m, LANES), F32),
            jax.ShapeDtypeStruct((2 * qw // LANES, m, LANES), F32),
            jax.ShapeDtypeStruct((m, 2 * GROUP_W), F32),
        ],
        compiler_params=_params("arbitrary"),
        name="inproj",
    )(xp, xs, g, w)


def _attn_prompt_kernel(q_ref, kv_ref, o_ref, m_scr, l_scr, acc_scr):
    g = pl.program_id(1)
    seq = q_ref.shape[1]

    @pl.when(g == 0)
    def _():
        m_scr[...] = jnp.full(m_scr.shape, NEG_INF, F32)
        l_scr[...] = jnp.zeros(l_scr.shape, F32)
        acc_scr[...] = jnp.zeros(acc_scr.shape, F32)

    qi = lax.broadcasted_iota(jnp.int32, (NK, 2 * NK), 0)
    kc = lax.broadcasted_iota(jnp.int32, (NK, 2 * NK), 1)
    band = (kc >= qi) & (kc <= qi + NK)
    lane = lax.broadcasted_iota(jnp.int32, (NK, LANES), 1)
    lane_kv = lax.broadcasted_iota(jnp.int32, (2 * NK, LANES), 1)
    ones_kv = jnp.ones((2 * NK, LANES), BF16)
    n_chunks = GROUP_W // LANES
    hpc = LANES // HEAD_DIM

    def block(blk, d, last):
        span = NK * d
        base = pl.multiple_of(blk * span, span)
        pbase = pl.multiple_of(jnp.maximum(blk - 1, 0) * span, span)
        mask = band & ((blk > 0) | (kc >= NK))
        for r in range(d):
            rows = pl.ds(base + r, NK, stride=d) if d > 1 else pl.ds(base, NK)
            prows = pl.ds(pbase + r, NK, stride=d) if d > 1 else pl.ds(pbase, NK)
            m_old = m_scr[rows, :]
            l_old = l_scr[rows, :]
            m_new, l_new = m_old, l_old
            for c in range(GROUP_W // LANES):
                qc = q_ref[c, rows, :].astype(BF16)
                kc_ = jnp.concatenate([kv_ref[c, prows, :], kv_ref[c, rows, :]], axis=0).astype(BF16)
                vc_ = jnp.concatenate([kv_ref[n_chunks + c, prows, :], kv_ref[n_chunks + c, rows, :]],
                                      axis=0).astype(BF16)
                acc = acc_scr[c, rows, :]
                scale, denom, contrib = None, None, None
                for hh in range(hpc):
                    h = c * hpc + hh
                    sel_q = (lane >= hh * HEAD_DIM) & (lane < (hh + 1) * HEAD_DIM)
                    sel_v = (lane_kv >= hh * HEAD_DIM) & (lane_kv < (hh + 1) * HEAD_DIM)
                    s = lax.dot_general(jnp.where(sel_q, qc, 0), kc_, (((1,), (1,)), ((), ())),
                                        preferred_element_type=F32)
                    s = jnp.where(mask, s, NEG_INF)
                    mo = m_old[:, h:h + 1]
                    mn = jnp.maximum(mo, jnp.max(s, axis=-1, keepdims=True))
                    p = jnp.exp(s - mn)
                    alpha = jnp.exp(mo - mn)
                    pv = _dot(p.astype(BF16), jnp.concatenate([jnp.where(sel_v, vc_, 0), ones_kv], axis=1))
                    ln = alpha * l_old[:, h:h + 1] + pv[:, LANES:]
                    contrib = pv[:, :LANES] if contrib is None else contrib + pv[:, :LANES]
                    scale = alpha if scale is None else jnp.where(sel_q, alpha, scale)
                    denom = ln if denom is None else jnp.where(sel_q, ln, denom)
                    m_new = jnp.where(lane == h, mn, m_new)
                    l_new = jnp.where(lane == h, ln, l_new)
                acc = scale * acc + contrib
                if last:
                    o_ref[c, rows, :] = acc / denom
                else:
                    acc_scr[c, rows, :] = acc
            if not last:
                m_scr[rows, :] = m_new
                l_scr[rows, :] = l_new

    for gi, (window, d) in enumerate(DIL_GROUPS):
        @pl.when(g == gi)
        def _(d=d, last=(gi == N_GROUPS - 1)):
            def body(blk, carry):
                block(blk, d, last)
                return carry
            n_blocks = seq // (NK * d)
            lax.fori_loop(0, n_blocks, body, 0, unroll=max(1, min(n_blocks, ATTN_BLOCKS_PER_BODY // d)))


def _attn_prompt(q, kv, batch, seq):
    assert seq % (NK * max(d for _, d in DIL_GROUPS)) == 0
    return pl.pallas_call(
        _attn_prompt_kernel,
        grid=(batch, N_GROUPS),
        in_specs=[
            pl.BlockSpec((GROUP_W // LANES, seq, LANES), lambda b, g: (g, b, 0)),
            pl.BlockSpec((2 * GROUP_W // LANES, seq, LANES), lambda b, g: (g, b, 0)),
        ],
        out_specs=pl.BlockSpec((GROUP_W // LANES, seq, LANES), lambda b, g: (0, b, 0)),
        out_shape=jax.ShapeDtypeStruct((GROUP_W // LANES, batch * seq, LANES), F32),
        scratch_shapes=[
            pltpu.VMEM((seq, LANES), F32),
            pltpu.VMEM((seq, LANES), F32),
            pltpu.VMEM((GROUP_W // LANES, seq, LANES), F32),
        ],
        compiler_params=_params("arbitrary", "arbitrary", vmem_limit=ATTN_VMEM_LIMIT),
        name="attn_prompt",
    )(q, kv)


def _mix_sample_kernel(q_ref, kn_ref, vn_ref, gbu_ref, st_ref, c0_ref, c1_ref, c2_ref, w_ref,
                       yb_ref, ya_ref, ns_ref):
    sb, t_len = q_ref.shape[0], q_ref.shape[3]
    w = w_ref[...]
    caches = (c0_ref, c1_ref, c2_ref)
    nt_dims = (((1,), (1,)), ((), ()))
    qn = lax.broadcasted_iota(jnp.int32, (t_len, t_len), 0)
    tn = lax.broadcasted_iota(jnp.int32, (t_len, t_len), 1)

    for s in range(sb):
        for h in range(HEADS):
            sc_parts, v_parts = [], []
            for gi, (window, d) in enumerate(DIL_GROUPS):
                c_ref = caches[gi]
                n_pos = c_ref.shape[-1]
                qg = q_ref[s, gi, h].astype(BF16)
                qc = lax.broadcasted_iota(jnp.int32, (t_len, n_pos), 0)
                pc = lax.broadcasted_iota(jnp.int32, (t_len, n_pos), 1)
                sc = _dot(qg, c_ref[s, 0, h].astype(BF16))
                sc_parts.append(jnp.where((((pc - qc) & (d - 1)) == 0) & (pc >= qc), sc, NEG_INF))
                v_parts.append((c_ref[s, 1, h].astype(BF16), True))
                sn = lax.dot_general(qg, kn_ref[s, gi, h].astype(BF16), nt_dims, preferred_element_type=F32)
                sc_parts.append(jnp.where((tn <= qn) & (((qn - tn) & (d - 1)) == 0), sn, NEG_INF))
                v_parts.append((vn_ref[s, gi, h].astype(BF16), False))
            mx = functools.reduce(jnp.maximum, [jnp.max(p, axis=-1, keepdims=True) for p in sc_parts])
            den = jnp.zeros((t_len, 1), F32)
            acc = jnp.zeros((t_len, HEAD_DIM), F32)
            for sc, (v, transposed) in zip(sc_parts, v_parts):
                p = jnp.exp(sc - mx)
                den = den + jnp.sum(p, axis=-1, keepdims=True)
                pb = p.astype(BF16)
                acc = acc + (lax.dot_general(pb, v, nt_dims, preferred_element_type=F32) if transposed
                             else _dot(pb, v))
            yb_ref[s, h] = acc / den
        gbu = gbu_ref[s]
        gb = gbu[:, 0:GROUP_W]
        u = gbu[:, GROUP_W:]
        st = st_ref[s]
        ext = [st[0:1], st[1:2]] + [u[t:t + 1] for t in range(t_len)]
        for t in range(t_len):
            conv = w[0:1] * ext[t] + w[1:2] * ext[t + 1] + w[2:3] * ext[t + 2]
            ya_ref[s, t:t + 1, :] = gb[t:t + 1] * conv
        for r in range(A_KERNEL - 1):
            ns_ref[s, r:r + 1, :] = ext[t_len + r]


def _mix_sample(q5, kn, vn, gbu3, st, c0, c1, c2, w):
    n, t_len = q5.shape[0], q5.shape[3]
    sb = ATTN_SEQ_BLOCK
    assert t_len >= A_KERNEL - 1 and n % sb == 0
    for c, (window, d) in zip((c0, c1, c2), DIL_GROUPS):
        assert c.shape[-1] == NK * d and d & (d - 1) == 0
    qspec = pl.BlockSpec((sb, N_GROUPS, HEADS, t_len, HEAD_DIM), lambda i: (i, 0, 0, 0, 0))
    cspec = lambda c: pl.BlockSpec((sb,) + c.shape[1:], lambda i: (i, 0, 0, 0, 0))
    return pl.pallas_call(
        _mix_sample_kernel,
        grid=(n // sb,),
        in_specs=[
            qspec, qspec, qspec,
            pl.BlockSpec((sb, t_len, 2 * GROUP_W), lambda i: (i, 0, 0)),
            pl.BlockSpec((sb, A_KERNEL - 1, GROUP_W), lambda i: (i, 0, 0)),
            cspec(c0), cspec(c1), cspec(c2),
            pl.BlockSpec((A_KERNEL, GROUP_W), lambda i: (0, 0)),
        ],
        out_specs=[
            pl.BlockSpec((sb, HEADS, t_len, HEAD_DIM), lambda i: (i, 0, 0, 0)),
            pl.BlockSpec((sb, t_len, GROUP_W), lambda i: (i, 0, 0)),
            pl.BlockSpec((sb, A_KERNEL - 1, GROUP_W), lambda i: (i, 0, 0)),
        ],
        out_shape=[
            jax.ShapeDtypeStruct((n, HEADS, t_len, HEAD_DIM), F32),
            jax.ShapeDtypeStruct((n, t_len, GROUP_W), F32),
            jax.ShapeDtypeStruct((n, A_KERNEL - 1, GROUP_W), F32),
        ],
        compiler_params=_params("arbitrary"),
        name="mix_sample",
    )(q5, kn, vn, gbu3, st, c0, c1, c2, w)


def _route(h2, wr_ref, br_ref, xs_ref, rinfo_ref, gtab_ref):
    tm = h2.shape[0]
    h_hi = h2.astype(BF16)
    h_lo = (h2 - h_hi.astype(F32)).astype(BF16)
    nt_dims = (((1,), (1,)), ((), ()))
    w_hi, w_lo = wr_ref[0], wr_ref[1]
    lt = (lax.dot_general(w_hi, h_hi, nt_dims, preferred_element_type=F32)
          + lax.dot_general(w_hi, h_lo, nt_dims, preferred_element_type=F32)
          + lax.dot_general(w_lo, h_hi, nt_dims, preferred_element_type=F32)) + br_ref[...]
    rg = lax.broadcasted_iota(jnp.int32, (SUBLANES, tm), 0)
    gl = jnp.where(rg < N_EXPERT_GROUPS, lt[0:SUBLANES], NEG_INF)
    ge = jnp.exp(gl - jnp.max(gl, axis=0, keepdims=True))
    gp = ge / jnp.sum(ge, axis=0, keepdims=True)
    g_val = jnp.max(gp, axis=0, keepdims=True)
    g_idx = jnp.min(jnp.where(gp == g_val, rg, SUBLANES), axis=0, keepdims=True)
    re = lax.broadcasted_iota(jnp.int32, (N_EXPERTS, tm), 0)
    sel = (re // EXPERTS_PER_GROUP) == g_idx
    el = jnp.where(sel, lt[ROUTER_EXPERT_ROW0:ROUTER_EXPERT_ROW0 + N_EXPERTS], NEG_INF)
    ee = jnp.exp(el - jnp.max(el, axis=0, keepdims=True))
    ep = jnp.where(sel, ee / jnp.sum(ee, axis=0, keepdims=True), -1.0)
    v1 = jnp.max(ep, axis=0, keepdims=True)
    i1 = jnp.min(jnp.where(ep == v1, re, N_EXPERTS), axis=0, keepdims=True)
    ep2 = jnp.where(re == i1, -1.0, ep)
    v2 = jnp.max(ep2, axis=0, keepdims=True)
    i2 = jnp.min(jnp.where(ep2 == v2, re, N_EXPERTS), axis=0, keepdims=True)
    scale = g_val / (v1 + v2)

    oh = [(re == i1).astype(F32), (re == i2).astype(F32)]
    cmat = (oh[0] + oh[1]).astype(BF16)
    earlier = (lax.broadcasted_iota(jnp.int32, (tm, tm), 0)
               < lax.broadcasted_iota(jnp.int32, (tm, tm), 1)).astype(BF16)
    before = _dot(cmat, earlier)
    cnt_row = lax.dot_general(jnp.ones((SUBLANES, tm), BF16), cmat, nt_dims, preferred_element_type=F32)
    grp_row = ((cnt_row.astype(jnp.int32) + (GROUP_ROWS - 1)) // GROUP_ROWS).astype(F32)
    lower = (lax.broadcasted_iota(jnp.int32, (N_EXPERTS, N_EXPERTS), 0)
             < lax.broadcasted_iota(jnp.int32, (N_EXPERTS, N_EXPERTS), 1)).astype(BF16)
    goff_row = _dot(grp_row.astype(BF16), lower)
    slot_rows = lax.broadcasted_iota(jnp.int32, (TILE_SLOTS, tm), 0)
    place = None
    pos = []
    for k in range(2):
        rank = jnp.sum(oh[k] * before, axis=0, keepdims=True)
        seg = _dot(goff_row.astype(BF16), oh[k].astype(BF16))[0:1]
        pos.append(seg * GROUP_ROWS + rank)
        hit = slot_rows == pos[k].astype(jnp.int32)
        place = hit if place is None else place | hit
    xs_ref[...] = _dot(jnp.where(place, 1.0, 0.0).astype(BF16), h_hi).astype(BF16)
    rl = lax.broadcasted_iota(jnp.int32, (LANES, tm), 0)
    info_t = jnp.where(rl == 0, v1 * scale, jnp.where(rl == 1, v2 * scale,
                       jnp.where(rl == 2, pos[0], jnp.where(rl == 3, pos[1], 0.0))))
    rinfo_ref[...] = info_t.T
    spread = (lax.broadcasted_iota(jnp.int32, (N_EXPERTS, LANES), 0)
              == lax.broadcasted_iota(jnp.int32, (N_EXPERTS, LANES), 1)).astype(BF16)
    gtab_ref[...] = _dot(grp_row.astype(BF16), spread).astype(jnp.int32)


ROUTER_EXPERT_ROW0 = SUBLANES
ROUTER_ROWS = 48
TILE_GROUPS = -(-(2 * ROW_TILE + (GROUP_ROWS - 1) * N_EXPERTS) // GROUP_ROWS)
TILE_SLOTS = TILE_GROUPS * GROUP_ROWS


def _router_specs(tm, d):
    ins = [pl.BlockSpec((1, d), lambda i: (0, 0)),
           pl.BlockSpec((2, ROUTER_ROWS, d), lambda i: (0, 0, 0)),
           pl.BlockSpec((ROUTER_ROWS, 1), lambda i: (0, 0))]
    outs = [pl.BlockSpec((TILE_SLOTS, d), lambda i: (i, 0)),
            pl.BlockSpec((tm, LANES), lambda i: (i, 0)),
            pl.BlockSpec((None, SUBLANES, LANES), lambda i: (i, 0, 0))]
    return ins, outs


def _router_shapes(m, d):
    nb = m // ROW_TILE
    return [jax.ShapeDtypeStruct((nb * TILE_SLOTS, d), BF16), jax.ShapeDtypeStruct((m, LANES), F32),
            jax.ShapeDtypeStruct((nb, SUBLANES, LANES), jnp.int32)]


def _outproj_kernel(n_prompt_tiles, tiles_per_seq, xp_ref, xs_ref, gbu_ref, halo_ref, cw_ref, yb_ref, ys_ref,
                    wo_ref, g_ref, wr_ref, br_ref, x1_ref, xs_out_ref, rinfo_ref, gtab_ref, conv_scr):
    i = pl.program_id(0)
    tm = xp_ref.shape[0]
    is_p = i < n_prompt_tiles
    x = jnp.where(is_p, xp_ref[...], xs_ref[...])
    u = gbu_ref[:, GROUP_W:]
    first = (i % tiles_per_seq) == 0
    conv_scr[0:SUBLANES, :] = jnp.where(first, 0.0, halo_ref[:, GROUP_W:])
    conv_scr[SUBLANES:, :] = u
    cw = cw_ref[...]
    ya = gbu_ref[:, 0:GROUP_W] * (cw[0:1, :] * conv_scr[pl.ds(SUBLANES - 2, tm), :]
                                  + cw[1:2, :] * conv_scr[pl.ds(SUBLANES - 1, tm), :] + cw[2:3, :] * u)
    ymix = jnp.where(is_p, jnp.concatenate([ya, _load_chunks(yb_ref)], axis=-1), ys_ref[...])
    x1 = x + _dot(ymix.astype(BF16), wo_ref[...])
    x1_ref[...] = x1
    _route(_rms(x1, g_ref[...]), wr_ref, br_ref, xs_out_ref, rinfo_ref, gtab_ref)


def _outproj(xp, xs, gbu, conv_w, yb_p, ymix_s, wo, g, wr, br, seq):
    mp, d = xp.shape
    ms = xs.shape[0]
    tm = ROW_TILE
    npt, nst = mp // tm, ms // tm
    m = mp + ms
    r_in, r_out = _router_specs(tm, d)
    pmap = lambda i: (jnp.minimum(i, npt - 1), 0)
    smap = lambda i: (jnp.maximum(i - npt, 0), 0)
    halo_map = lambda i: (jnp.maximum(jnp.minimum(i, npt - 1) * (tm // SUBLANES) - 1, 0), 0)
    return pl.pallas_call(
        functools.partial(_outproj_kernel, npt, seq // tm),
        grid=(npt + nst,),
        in_specs=[
            pl.BlockSpec((tm, d), pmap),
            pl.BlockSpec((tm, d), smap),
            pl.BlockSpec((tm, 2 * GROUP_W), pmap),
            pl.BlockSpec((SUBLANES, 2 * GROUP_W), halo_map),
            pl.BlockSpec((A_KERNEL, GROUP_W), lambda i: (0, 0)),
            pl.BlockSpec((GROUP_W // LANES, tm, LANES), lambda i: (0, jnp.minimum(i, npt - 1), 0)),
            pl.BlockSpec((tm, 2 * GROUP_W), smap),
            pl.BlockSpec((2 * GROUP_W, d), lambda i: (0, 0)),
        ] + r_in,
        out_specs=[pl.BlockSpec((tm, d), lambda i: (i, 0))] + r_out,
        out_shape=[jax.ShapeDtypeStruct((m, d), F32)] + _router_shapes(m, d),
        scratch_shapes=[pltpu.VMEM((tm + SUBLANES, GROUP_W), F32)],
        compiler_params=_params("arbitrary"),
        name="outproj_router",
    )(xp, xs, gbu, gbu, conv_w, yb_p, ymix_s, wo, g, wr, br)


def _experts_kernel(te_ref, src_ref, dst_ref, nt_ref, h_hbm, wg_ref, wu_ref, wd_ref, y_hbm, xbuf, ybuf, gsem, ssem,
                    *, trash_row):
    i = pl.program_id(0)
    nt = nt_ref[0]
    ng = CHUNK_GROUPS
    slot = i % 2
    other = 1 - slot

    def group_rows(r):
        return pl.ds(r if isinstance(r, int) else pl.multiple_of(r, GROUP_ROWS), GROUP_ROWS)

    def gather_copy(p, sl, j):
        return pltpu.make_async_copy(h_hbm.at[group_rows(src_ref[p]), :],
                                     xbuf.at[sl, group_rows(j * GROUP_ROWS), :], gsem.at[sl])

    def scatter_copy(p, sl, j):
        return pltpu.make_async_copy(ybuf.at[sl, group_rows(j * GROUP_ROWS), :],
                                     y_hbm.at[group_rows(dst_ref[p]), :], ssem.at[sl])

    def wait_gather(sl):
        pltpu.make_async_copy(h_hbm.at[pl.ds(0, ng * GROUP_ROWS), :], xbuf.at[sl], gsem.at[sl]).wait()

    def wait_scatter(sl):
        pltpu.make_async_copy(ybuf.at[sl], y_hbm.at[pl.ds(0, ng * GROUP_ROWS), :], ssem.at[sl]).wait()

    @pl.when(i == 0)
    def _():
        ybuf[...] = jnp.zeros(ybuf.shape, BF16)
        init = pltpu.make_async_copy(ybuf.at[0], y_hbm.at[pl.ds(trash_row, ng * GROUP_ROWS), :], ssem.at[0])
        init.start()
        init.wait()

        def tail_copy(b):
            first = pl.multiple_of(b * TILE_SLOTS + 2 * ROW_TILE, GROUP_ROWS)
            return pltpu.make_async_copy(ybuf.at[0, pl.ds(0, TILE_SLOTS - 2 * ROW_TILE), :],
                                         y_hbm.at[pl.ds(first, TILE_SLOTS - 2 * ROW_TILE), :], ssem.at[0])

        def start_body(b, c):
            tail_copy(b).start()
            return c

        def wait_body(b, c):
            tail_copy(b).wait()
            return c
        lax.fori_loop(0, trash_row // TILE_SLOTS, start_body, 0)
        lax.fori_loop(0, trash_row // TILE_SLOTS, wait_body, 0)
        for j in range(ng):
            gather_copy(j, 0, j).start()

    @pl.when(i < nt)
    def _():
        wait_gather(slot)

        @pl.when(i >= 1)
        def _():
            wait_scatter(slot)

        base = i * ng
        for j in range(ng):
            gather_copy(base + ng + j, other, j).start(priority=j % 2)
            scatter_copy(base + j, other, j).start(priority=(j + 1) % 2)
        x = xbuf[slot]
        gate = _dot(x, wg_ref[...].astype(BF16))
        up = _dot(x, wu_ref[...].astype(BF16))
        act = (gate * jax.nn.sigmoid(gate) * up).astype(BF16)
        ybuf[slot] = _dot(act, wd_ref[...].astype(BF16)).astype(BF16)

    @pl.when(i == nt)
    def _():
        wait_gather(slot)
        wait_scatter(slot)
        for j in range(ng):
            scatter_copy(i * ng + j, other, j).start()
        wait_scatter(other)


def _experts(xs, tile_expert, src, dst, n_tiles, wg, wu, wd, layer):
    ng = CHUNK_GROUPS
    nt_max = tile_expert.shape[0]
    d, f = wg.shape[2], wg.shape[3]
    n_rows = xs.shape[0] + ng * GROUP_ROWS
    wmap = lambda i, te_ref, src_ref, dst_ref, nt_ref: (layer, te_ref[i], 0, 0)
    grid_spec = pltpu.PrefetchScalarGridSpec(
        num_scalar_prefetch=4,
        grid=(nt_max,),
        in_specs=[
            pl.BlockSpec(memory_space=pl.ANY),
            pl.BlockSpec((None, None, d, f), wmap),
            pl.BlockSpec((None, None, d, f), wmap),
            pl.BlockSpec((None, None, f, d), wmap),
        ],
        out_specs=pl.BlockSpec(memory_space=pl.ANY),
        scratch_shapes=[
            pltpu.VMEM((2, ng * GROUP_ROWS, d), BF16),
            pltpu.VMEM((2, ng * GROUP_ROWS, d), BF16),
            pltpu.SemaphoreType.DMA((2,)),
            pltpu.SemaphoreType.DMA((2,)),
        ],
    )
    return pl.pallas_call(
        functools.partial(_experts_kernel, trash_row=xs.shape[0]),
        grid_spec=grid_spec,
        out_shape=jax.ShapeDtypeStruct((n_rows, d), BF16),
        compiler_params=_params("arbitrary"),
        name="experts",
    )(tile_expert, src, dst, n_tiles, xs, wg, wu, wd)


def _plan_chunks(gtab):
    ng = CHUNK_GROUPS
    grp = gtab[:, 0, :N_EXPERTS]
    nb = grp.shape[0]
    nt_max = -(-nb * TILE_GROUPS // ng) + N_EXPERTS + 1
    total = jnp.sum(grp, axis=0)
    chunks = (total + ng - 1) // ng
    chunk_end = jnp.cumsum(chunks)
    chunk_start = chunk_end - chunks
    n_chunks = chunk_end[-1]
    seg_off = jnp.cumsum(grp, axis=1) - grp
    seg_end = jnp.cumsum(grp, axis=0)
    seg_start = seg_end - grp
    cidx = jnp.arange(nt_max, dtype=jnp.int32)
    chunk_expert = jnp.sum((cidx[:, None] >= chunk_end[None, :]).astype(jnp.int32), axis=1)
    last_expert = jnp.max(jnp.where(total > 0, jnp.arange(N_EXPERTS), 0))
    chunk_expert = jnp.where(cidx < n_chunks, jnp.minimum(chunk_expert, N_EXPERTS - 1), last_expert).astype(jnp.int32)
    is_e = chunk_expert[:, None] == jnp.arange(N_EXPERTS, dtype=jnp.int32)[None, :]
    pick = lambda per_expert: jnp.sum(jnp.where(is_e, per_expert[None, :], 0), axis=1)
    pick_rows = lambda table: jnp.sum(jnp.where(is_e[:, :, None], table.T[None, :, :], 0), axis=1)
    within = (cidx - pick(chunk_start))[:, None] * ng + jnp.arange(ng, dtype=jnp.int32)[None, :]
    valid = (within < pick(total)[:, None]) & (cidx < n_chunks)[:, None]
    ends = pick_rows(seg_end)
    base = pick_rows(jnp.arange(nb, dtype=jnp.int32)[:, None] * TILE_GROUPS + seg_off - seg_start)
    step = jnp.concatenate([base[:, :1], base[:, 1:] - base[:, :-1]], axis=1)
    passed = jnp.concatenate([jnp.ones((nt_max, ng, 1), bool), within[:, :, None] >= ends[:, None, :-1]], axis=2)
    group = within + jnp.sum(jnp.where(passed, step[:, None, :], 0), axis=2)
    pad_dst = (nb * TILE_GROUPS + jnp.arange(ng, dtype=jnp.int32)) * GROUP_ROWS
    src = jnp.where(valid, group * GROUP_ROWS, 0).astype(jnp.int32).reshape(nt_max * ng)
    dst = jnp.where(valid, group * GROUP_ROWS, pad_dst[None, :]).astype(jnp.int32).reshape(nt_max * ng)
    dst = jnp.concatenate([pad_dst, dst])
    return chunk_expert, src, dst, n_chunks.astype(jnp.int32).reshape(1)


def _combine(x_ref, y_ref, rinfo_ref):
    tm = x_ref.shape[0]
    n_slots = y_ref.shape[0]
    r = rinfo_ref[...]
    col = lax.broadcasted_iota(jnp.int32, (tm, n_slots), 1)
    pick = (jnp.where(col == r[:, 2:3].astype(jnp.int32), r[:, 0:1], 0.0)
            + jnp.where(col == r[:, 3:4].astype(jnp.int32), r[:, 1:2], 0.0))
    return x_ref[...] + _dot(pick.astype(BF16), y_ref[...])


def _combine_specs(tm, d):
    return [
        pl.BlockSpec((tm, d), lambda i: (i, 0)),
        pl.BlockSpec((TILE_SLOTS, d), lambda i: (i, 0)),
        pl.BlockSpec((tm, LANES), lambda i: (i, 0)),
    ]


def _combine_pw1_kernel(x_ref, y_ref, rinfo_ref, g_ref, w_ref, b_ref, x2_ref, u_ref):
    x2 = _combine(x_ref, y_ref, rinfo_ref)
    x2_ref[...] = x2
    d = x2.shape[1]
    z = _dot(_rms(x2, g_ref[...]).astype(BF16), w_ref[...]) + b_ref[...]
    u_ref[...] = z[:, :d] * jax.nn.sigmoid(z[:, d:])


def _combine_pw1(x1, y2, rinfo, g, w, b):
    m, d = x1.shape
    tm = ROW_TILE
    return pl.pallas_call(
        _combine_pw1_kernel,
        grid=(m // tm,),
        in_specs=_combine_specs(tm, d) + [
            pl.BlockSpec((1, d), lambda i: (0, 0)),
            pl.BlockSpec((d, 2 * d), lambda i: (0, 0)),
            pl.BlockSpec((1, 2 * d), lambda i: (0, 0)),
        ],
        out_specs=[pl.BlockSpec((tm, d), lambda i: (i, 0)), pl.BlockSpec((tm, d), lambda i: (i, 0))],
        out_shape=[jax.ShapeDtypeStruct((m, d), F32), jax.ShapeDtypeStruct((m, d), F32)],
        compiler_params=_params("arbitrary"),
        name="combine_pw1",
    )(x1, y2, rinfo, g, w, b)


def _combine_final_kernel(n_prompt_tiles, x_ref, y_ref, rinfo_ref, g_ref, yp_ref, ys_ref):
    i = pl.program_id(0)
    y = _rms(_combine(x_ref, y_ref, rinfo_ref), g_ref[...])

    @pl.when(i < n_prompt_tiles)
    def _():
        yp_ref[...] = y

    @pl.when(i >= n_prompt_tiles)
    def _():
        ys_ref[...] = y


def _combine_final(x3, y2, rinfo, g, mp):
    m, d = x3.shape
    tm = ROW_TILE
    npt = mp // tm
    return pl.pallas_call(
        functools.partial(_combine_final_kernel, npt),
        grid=(m // tm,),
        in_specs=_combine_specs(tm, d) + [pl.BlockSpec((1, d), lambda i: (0, 0))],
        out_specs=[pl.BlockSpec((tm, d), lambda i: (jnp.minimum(i, npt - 1), 0)),
                   pl.BlockSpec((tm, d), lambda i: (jnp.maximum(i - npt, 0), 0))],
        out_shape=[jax.ShapeDtypeStruct((mp, d), F32), jax.ShapeDtypeStruct((m - mp, d), F32)],
        compiler_params=_params("arbitrary"),
        name="combine_final",
    )(x3, y2, rinfo, g)


def _dwconv_prompt_tile(first, u_ref, halo_ref, w_ref, b_ref, scr):
    tm = u_ref.shape[0]
    scr[0, 0:CONV_HALO, :] = jnp.where(first, 0.0, halo_ref[...])
    scr[0, CONV_HALO:, :] = u_ref[...]
    n_shift = tm + CONV_HALO - SUBLANES
    for s in range(1, SUBLANES):
        scr[s, 0:n_shift, :] = scr[0, pl.ds(s, n_shift), :]
    acc = jnp.broadcast_to(b_ref[...], u_ref.shape)
    for k in range(C_KERNEL):
        off = CONV_HALO - (C_KERNEL - 1) + k
        acc = acc + w_ref[k:k + 1, :] * scr[off % SUBLANES, pl.ds(off - off % SUBLANES, tm), :]
    return acc


def _dwconv_sample_kernel(st_ref, u_ref, w_ref, b_ref, c_ref, ns_ref):
    n_state, t_len = st_ref.shape[0], u_ref.shape[0]
    rows = [st_ref[r] for r in range(n_state)] + [u_ref[t] for t in range(t_len)]
    for t in range(t_len):
        acc = jnp.broadcast_to(b_ref[...], rows[0].shape)
        for k in range(C_KERNEL):
            acc = acc + w_ref[k:k + 1, :] * rows[t + k]
        c_ref[t] = acc
    for r in range(n_state):
        ns_ref[r] = rows[r + t_len]


def _dwconv_sample(st, u3, w, b):
    n_state, n, d = st.shape
    t_len = u3.shape[0]
    sb = SEQ_BLOCK
    assert n_state == C_KERNEL - 1 and n % sb == 0
    return pl.pallas_call(
        _dwconv_sample_kernel,
        grid=(n // sb,),
        in_specs=[
            pl.BlockSpec((n_state, sb, d), lambda i: (0, i, 0)),
            pl.BlockSpec((t_len, sb, d), lambda i: (0, i, 0)),
            pl.BlockSpec((C_KERNEL, d), lambda i: (0, 0)),
            pl.BlockSpec((1, d), lambda i: (0, 0)),
        ],
        out_specs=[
            pl.BlockSpec((t_len, sb, d), lambda i: (0, i, 0)),
            pl.BlockSpec((n_state, sb, d), lambda i: (0, i, 0)),
        ],
        out_shape=[jax.ShapeDtypeStruct((t_len, n, d), F32), jax.ShapeDtypeStruct((n_state, n, d), F32)],
        compiler_params=_params("arbitrary"),
        name="dwconv_sample",
    )(st, u3, w, b)


def _conf_tail_kernel(n_prompt_tiles, tiles_per_seq, x_ref, u_ref, halo_ref, dww_ref, dwb_ref, cs_ref,
                      lg_ref, lb_ref, w_ref, b_ref, g_ref, wr_ref, br_ref,
                      x3_ref, xs_out_ref, rinfo_ref, gtab_ref, conv_scr):
    i = pl.program_id(0)
    c_p = _dwconv_prompt_tile((i % tiles_per_seq) == 0, u_ref, halo_ref, dww_ref, dwb_ref, conv_scr)
    c = jnp.where(i < n_prompt_tiles, c_p, cs_ref[...])
    mu = jnp.mean(c, axis=-1, keepdims=True)
    cc = c - mu
    var = jnp.mean(cc * cc, axis=-1, keepdims=True)
    y = cc * lax.rsqrt(var + LN_EPS) * lg_ref[...] + lb_ref[...]
    y = y * jax.nn.sigmoid(y)
    x3 = x_ref[...] + _dot(y.astype(BF16), w_ref[...]) + b_ref[...]
    x3_ref[...] = x3
    _route(_rms(x3, g_ref[...]), wr_ref, br_ref, xs_out_ref, rinfo_ref, gtab_ref)


def _conf_tail(x2, u, dw_w, dw_b, c_s, ln_g, ln_b, w, b, g, wr, br, mp, seq):
    m, d = x2.shape
    tm = ROW_TILE
    npt = mp // tm
    r_in, r_out = _router_specs(tm, d)
    vec = pl.BlockSpec((1, d), lambda i: (0, 0))
    ptile = lambda i: jnp.minimum(i, npt - 1)
    return pl.pallas_call(
        functools.partial(_conf_tail_kernel, npt, seq // tm),
        grid=(m // tm,),
        in_specs=[
            pl.BlockSpec((tm, d), lambda i: (i, 0)),
            pl.BlockSpec((tm, d), lambda i: (ptile(i), 0)),
            pl.BlockSpec((CONV_HALO, d), lambda i: (jnp.maximum(ptile(i) * (tm // CONV_HALO) - 1, 0), 0)),
            pl.BlockSpec((C_KERNEL, d), lambda i: (0, 0)),
            vec,
            pl.BlockSpec((tm, d), lambda i: (jnp.maximum(i - npt, 0), 0)),
            vec, vec,
            pl.BlockSpec((d, d), lambda i: (0, 0)),
            vec,
        ] + r_in,
        out_specs=[pl.BlockSpec((tm, d), lambda i: (i, 0))] + r_out,
        out_shape=[jax.ShapeDtypeStruct((m, d), F32)] + _router_shapes(m, d),
        scratch_shapes=[pltpu.VMEM((SUBLANES, tm + CONV_HALO, d), F32)],
        compiler_params=_params("arbitrary"),
        name="conf_tail_router",
    )(x2, u, u, dw_w, dw_b, c_s, ln_g, ln_b, w, b, g, wr, br)


def _router_weights(w_rg, b_rg, w_re, b_re):
    d = w_rg.shape[0]
    gpad = ROUTER_EXPERT_ROW0 - N_EXPERT_GROUPS
    epad = ROUTER_ROWS - ROUTER_EXPERT_ROW0 - N_EXPERTS
    wt = jnp.concatenate([w_rg.T, jnp.zeros((gpad, d), F32), w_re.T, jnp.zeros((epad, d), F32)], axis=0)
    w_hi = wt.astype(BF16)
    w_lo = (wt - w_hi.astype(F32)).astype(BF16)
    br = jnp.concatenate([b_rg, jnp.zeros((gpad,), F32), b_re, jnp.zeros((epad,), F32)])[:, None]
    return jnp.stack([w_hi, w_lo]), br


def _moe(xs, gtab, wg, wu, wd, layer):
    chunk_expert, src, dst, n_chunks = _plan_chunks(gtab)
    return _experts(xs, chunk_expert, src, dst, n_chunks, wg, wu, wd, layer)


def kernel(x_prompt, x_sample, state_shortconv, cache_kv_w128, cache_kv_w512, cache_kv_w2048, state_conformer,
           g_mix, g_ffn, g_final, w_in, conv_a_w, w_out, w_pw1, b_pw1, dw_w, dw_b, ln_g, ln_b, w_pw2, b_pw2,
           w_router_group, b_router_group, w_router_expert, b_router_expert, w_gate, w_up, w_down):
    batch, seq, d = x_prompt.shape
    n_dec, t_dec, _ = x_sample.shape
    mp, ms = batch * seq, n_dec * t_dec
    assert g_mix.shape[0] == 2 and mp % ROW_TILE == 0 and ms % ROW_TILE == 0 and seq % ROW_TILE == 0
    xp = x_prompt.reshape(mp, d)
    xs = x_sample.reshape(ms, d)
    slab = (HEADS, HEAD_DIM)

    aw, qw = 3 * GROUP_W, N_GROUPS * GROUP_W
    w0 = w_in[0]
    kcols = w0[:, aw + qw:aw + 2 * qw].reshape(d, N_GROUPS, GROUP_W)
    vcols = w0[:, aw + 2 * qw:].reshape(d, N_GROUPS, GROUP_W)
    w_perm = jnp.concatenate([w0[:, :aw + qw], jnp.stack([kcols, vcols], axis=2).reshape(d, 2 * qw)], axis=1)
    q, kv, gbu = _inproj(xp, xs, g_mix[0][None, :], w_perm.astype(BF16))

    yb_p = _attn_prompt(q, kv, batch, seq)

    hpc = LANES // HEAD_DIM
    q_s = jnp.transpose(q[:, mp:].reshape(N_GROUPS, HEADS // hpc, n_dec, t_dec, hpc, HEAD_DIM),
                        (2, 0, 1, 4, 3, 5)).reshape(n_dec, N_GROUPS, HEADS, t_dec, HEAD_DIM)
    kv_s = jnp.transpose(kv[:, mp:].reshape(N_GROUPS, 2, HEADS // hpc, n_dec, t_dec, hpc, HEAD_DIM),
                         (1, 3, 0, 2, 5, 4, 6)).reshape(2, n_dec, N_GROUPS, HEADS, t_dec, HEAD_DIM)
    gbu_s = gbu[mp:].reshape(n_dec, t_dec, 2 * GROUP_W)
    c0, c1, c2 = (jnp.transpose(c[0], (0, 2, 3, 4, 1)) for c in (cache_kv_w128, cache_kv_w512, cache_kv_w2048))
    yb_s, ya_s, s_sc = _mix_sample(q_s, kv_s[0], kv_s[1], gbu_s, state_shortconv[0], c0, c1, c2, conv_a_w[0])
    ymix_s = jnp.concatenate([ya_s.reshape(ms, GROUP_W),
                              jnp.transpose(yb_s, (0, 2, 1, 3)).reshape(ms, GROUP_W)], axis=-1)

    wr0, br0 = _router_weights(w_router_group[0], b_router_group[0], w_router_expert[0], b_router_expert[0])
    x1, xsort0, rinfo0, gtab0 = _outproj(xp, xs, gbu, conv_a_w[0], yb_p, ymix_s, w_out[0].astype(BF16),
                                         g_ffn[0][None, :], wr0, br0, seq)
    y2 = _moe(xsort0, gtab0, w_gate, w_up, w_down, 0)

    x2, u = _combine_pw1(x1, y2, rinfo0, g_mix[1][None, :], w_pw1[0].astype(BF16), b_pw1[0][None, :])
    c_s3, s_cf = _dwconv_sample(jnp.transpose(state_conformer[0], (1, 0, 2)),
                                jnp.transpose(u[mp:].reshape(n_dec, t_dec, d), (1, 0, 2)), dw_w[0], dw_b[0][None, :])
    c_s = jnp.transpose(c_s3, (1, 0, 2)).reshape(ms, d)
    wr1, br1 = _router_weights(w_router_group[1], b_router_group[1], w_router_expert[1], b_router_expert[1])
    x3, xsort1, rinfo1, gtab1 = _conf_tail(x2, u, dw_w[0], dw_b[0][None, :], c_s, ln_g[0][None, :], ln_b[0][None, :],
                                           w_pw2[0].astype(BF16), b_pw2[0][None, :], g_ffn[1][None, :], wr1, br1,
                                           mp, seq)
    y2b = _moe(xsort1, gtab1, w_gate, w_up, w_down, 1)
    y_p, y_s = _combine_final(x3, y2b, rinfo1, g_final[None, :], mp)

    cpg = 2 * GROUP_W // LANES

    def tail_rows(arr, n, axis):
        return jnp.stack([lax.slice_in_dim(arr, (b + 1) * seq - n, (b + 1) * seq, axis=axis) for b in range(batch)])

    p_kv = []
    for gi, (w, _) in enumerate(DIL_GROUPS):
        rows = tail_rows(kv[gi * cpg:(gi + 1) * cpg], min(w, seq), 1)
        p_kv.append(jnp.transpose(rows, (0, 2, 1, 3)).reshape(1, batch, min(w, seq), 2, *slab))
    s_kv = [jnp.transpose(kv_s[:, :, gi], (1, 3, 0, 2, 4))[None] for gi in range(N_GROUPS)]
    p_sc = tail_rows(gbu, A_KERNEL - 1, 0)[:, :, GROUP_W:][None]
    p_cf = tail_rows(u, C_KERNEL - 1, 0)[None]
    return (y_p.reshape(batch, seq, d), y_s.reshape(n_dec, t_dec, d), p_sc, p_kv[0], p_kv[1], p_kv[2], p_cf,
            s_sc[None], s_kv[0], s_kv[1], s_kv[2], jnp.transpose(s_cf, (1, 0, 2))[None])
```

```python
import functools

import jax
import jax.numpy as jnp
from jax import lax
from jax.experimental import pallas as pl
from jax.experimental.pallas import tpu as pltpu

F32 = jnp.float32
BF16 = jnp.bfloat16

DIL_GROUPS = ((128, 1), (512, 4), (2048, 16))
N_GROUPS = len(DIL_GROUPS)
HEADS = 4
HEAD_DIM = 64
GROUP_W = HEADS * HEAD_DIM
NK = DIL_GROUPS[0][0] // DIL_GROUPS[0][1]
A_KERNEL = 3
C_KERNEL = 31
N_EXPERT_GROUPS = 4
EXPERTS_PER_GROUP = 8
N_EXPERTS = N_EXPERT_GROUPS * EXPERTS_PER_GROUP
RMS_EPS = 1e-6
LN_EPS = 1e-5
NEG_INF = -1e30

SUBLANES = 8
LANES = 128
VMEM_LIMIT = 48 * 1024 * 1024
ATTN_VMEM_LIMIT = 60 * 1024 * 1024
ATTN_BLOCKS_PER_BODY = 8

ROW_TILE = 256
GROUP_ROWS = 16
CHUNK_GROUPS = 32
CONV_HALO = 32
SEQ_BLOCK = 8
ATTN_SEQ_BLOCK = 2


def _params(*sem, vmem_limit=VMEM_LIMIT):
    return pltpu.CompilerParams(dimension_semantics=sem, vmem_limit_bytes=vmem_limit)


def _rms(x, g):
    return x * lax.rsqrt(jnp.mean(x * x, axis=-1, keepdims=True) + RMS_EPS) * g


def _dot(a, b):
    return jnp.dot(a, b, preferred_element_type=F32)


def _rows_to_lanes(ref, n_rows, base=None):
    parts = []
    for s in range(SUBLANES):
        idx = pl.ds(s, n_rows, stride=SUBLANES)
        parts.append(ref[idx, :] if base is None else ref[base, idx, :])
    return jnp.concatenate(parts, axis=-1)


def _lanes_to_rows(ref, val, base=None):
    n_rows = val.shape[0]
    for s in range(SUBLANES):
        idx = pl.ds(s, n_rows, stride=SUBLANES)
        piece = val[:, s * LANES:(s + 1) * LANES]
        if base is None:
            ref[idx, :] = piece
        else:
            ref[base, idx, :] = piece


def _inproj_kernel(n_prompt_tiles, xp_ref, xs_ref, g_ref, w_ref, q_ref, kv_ref, gbu_ref):
    i = pl.program_id(0)
    x = jnp.where(i < n_prompt_tiles, xp_ref[...], xs_ref[...])
    h = _rms(x, g_ref[...]).astype(BF16)
    aw = 3 * GROUP_W
    a = _dot(h, w_ref[:, 0:aw])
    gbu_ref[:, 0:GROUP_W] = a[:, 0:GROUP_W]
    gbu_ref[:, GROUP_W:2 * GROUP_W] = a[:, GROUP_W:2 * GROUP_W] * a[:, 2 * GROUP_W:aw]
    qw = N_GROUPS * GROUP_W
    _store_chunks(q_ref, _dot(h, w_ref[:, aw:aw + qw]) * (HEAD_DIM ** -0.5))
    _store_chunks(kv_ref, _dot(h, w_ref[:, aw + qw:]))


def _store_chunks(ref, val):
    for c in range(val.shape[1] // LANES):
        ref[c] = val[:, c * LANES:(c + 1) * LANES]


def _load_chunks(ref, rows=None):
    n = ref.shape[0]
    return jnp.concatenate([ref[c] if rows is None else ref[c, rows, :] for c in range(n)], axis=-1)


def _inproj(xp, xs, g, w):
    mp, d = xp.shape
    ms = xs.shape[0]
    tm = ROW_TILE
    npt, nst = mp // tm, ms // tm
    m = mp + ms
    ncols = w.shape[1]
    qw = N_GROUPS * GROUP_W
    return pl.pallas_call(
        functools.partial(_inproj_kernel, npt),
        grid=(npt + nst,),
        in_specs=[
            pl.BlockSpec((tm, d), lambda i: (jnp.minimum(i, npt - 1), 0)),
            pl.BlockSpec((tm, d), lambda i: (jnp.maximum(i - npt, 0), 0)),
            pl.BlockSpec((1, d), lambda i: (0, 0)),
            pl.BlockSpec((d, ncols), lambda i: (0, 0)),
        ],
        out_specs=[
            pl.BlockSpec((qw // LANES, tm, LANES), lambda i: (0, i, 0)),
            pl.BlockSpec((2 * qw // LANES, tm, LANES), lambda i: (0, i, 0)),
            pl.BlockSpec((tm, 2 * GROUP_W), lambda i: (i, 0)),
        ],
        out_shape=[
            jax.ShapeDtypeStruct((qw // LANES, m, LANES), F32),
            jax.ShapeDtypeStruct((2 * qw // LANES, m, LANES), F32),
            jax.ShapeDtypeStruct((m, 2 * GROUP_W), F32),
        ],
        compiler_params=_params("arbitrary"),
        name="inproj",
    )(xp, xs, g, w)


def _attn_prompt_kernel(q_ref, kv_ref, o_ref, m_scr, l_scr, acc_scr):
    g = pl.program_id(1)
    seq = q_ref.shape[1]

    @pl.when(g == 0)
    def _():
        m_scr[...] = jnp.full(m_scr.shape, NEG_INF, F32)
        l_scr[...] = jnp.zeros(l_scr.shape, F32)
        acc_scr[...] = jnp.zeros(acc_scr.shape, F32)

    qi = lax.broadcasted_iota(jnp.int32, (NK, 2 * NK), 0)
    kc = lax.broadcasted_iota(jnp.int32, (NK, 2 * NK), 1)
    band = (kc >= qi) & (kc <= qi + NK)
    lane = lax.broadcasted_iota(jnp.int32, (NK, LANES), 1)
    lane_kv = lax.broadcasted_iota(jnp.int32, (2 * NK, LANES), 1)
    ones_kv = jnp.ones((2 * NK, LANES), BF16)
    n_chunks = GROUP_W // LANES
    hpc = LANES // HEAD_DIM

    def block(blk, d, last):
        span = NK * d
        base = pl.multiple_of(blk * span, span)
        pbase = pl.multiple_of(jnp.maximum(blk - 1, 0) * span, span)
        mask = band & ((blk > 0) | (kc >= NK))
        for r in range(d):
            rows = pl.ds(base + r, NK, stride=d) if d > 1 else pl.ds(base, NK)
            prows = pl.ds(pbase + r, NK, stride=d) if d > 1 else pl.ds(pbase, NK)
            m_old = m_scr[rows, :]
            l_old = l_scr[rows, :]
            m_new, l_new = m_old, l_old
            for c in range(GROUP_W // LANES):
                qc = q_ref[c, rows, :].astype(BF16)
                kc_ = jnp.concatenate([kv_ref[c, prows, :], kv_ref[c, rows, :]], axis=0).astype(BF16)
                vc_ = jnp.concatenate([kv_ref[n_chunks + c, prows, :], kv_ref[n_chunks + c, rows, :]],
                                      axis=0).astype(BF16)
                acc = acc_scr[c, rows, :]
                scale, denom, contrib = None, None, None
                for hh in range(hpc):
                    h = c * hpc + hh
                    sel_q = (lane >= hh * HEAD_DIM) & (lane < (hh + 1) * HEAD_DIM)
                    sel_v = (lane_kv >= hh * HEAD_DIM) & (lane_kv < (hh + 1) * HEAD_DIM)
                    s = lax.dot_general(jnp.where(sel_q, qc, 0), kc_, (((1,), (1,)), ((), ())),
                                        preferred_element_type=F32)
                    s = jnp.where(mask, s, NEG_INF)
                    mo = m_old[:, h:h + 1]
                    mn = jnp.maximum(mo, jnp.max(s, axis=-1, keepdims=True))
                    p = jnp.exp(s - mn)
                    alpha = jnp.exp(mo - mn)
                    pv = _dot(p.astype(BF16), jnp.concatenate([jnp.where(sel_v, vc_, 0), ones_kv], axis=1))
                    ln = alpha * l_old[:, h:h + 1] + pv[:, LANES:]
                    contrib = pv[:, :LANES] if contrib is None else contrib + pv[:, :LANES]
                    scale = alpha if scale is None else jnp.where(sel_q, alpha, scale)
                    denom = ln if denom is None else jnp.where(sel_q, ln, denom)
                    m_new = jnp.where(lane == h, mn, m_new)
                    l_new = jnp.where(lane == h, ln, l_new)
                acc = scale * acc + contrib
                if last:
                    o_ref[c, rows, :] = acc / denom
                else:
                    acc_scr[c, rows, :] = acc
            if not last:
                m_scr[rows, :] = m_new
                l_scr[rows, :] = l_new

    for gi, (window, d) in enumerate(DIL_GROUPS):
        @pl.when(g == gi)
        def _(d=d, last=(gi == N_GROUPS - 1)):
            def body(blk, carry):
                block(blk, d, last)
                return carry
            n_blocks = seq // (NK * d)
            lax.fori_loop(0, n_blocks, body, 0, unroll=max(1, min(n_blocks, ATTN_BLOCKS_PER_BODY // d)))


def _attn_prompt(q, kv, batch, seq):
    assert seq % (NK * max(d for _, d in DIL_GROUPS)) == 0
    return pl.pallas_call(
        _attn_prompt_kernel,
        grid=(batch, N_GROUPS),
        in_specs=[
            pl.BlockSpec((GROUP_W // LANES, seq, LANES), lambda b, g: (g, b, 0)),
            pl.BlockSpec((2 * GROUP_W // LANES, seq, LANES), lambda b, g: (g, b, 0)),
        ],
        out_specs=pl.BlockSpec((GROUP_W // LANES, seq, LANES), lambda b, g: (0, b, 0)),
        out_shape=jax.ShapeDtypeStruct((GROUP_W // LANES, batch * seq, LANES), F32),
        scratch_shapes=[
            pltpu.VMEM((seq, LANES), F32),
            pltpu.VMEM((seq, LANES), F32),
            pltpu.VMEM((GROUP_W // LANES, seq, LANES), F32),
        ],
        compiler_params=_params("arbitrary", "arbitrary", vmem_limit=ATTN_VMEM_LIMIT),
        name="attn_prompt",
    )(q, kv)


def _mix_sample_kernel(q_ref, kn_ref, vn_ref, gbu_ref, st_ref, c0_ref, c1_ref, c2_ref, w_ref,
                       yb_ref, ya_ref, ns_ref):
    sb, t_len = q_ref.shape[0], q_ref.shape[3]
    w = w_ref[...]
    caches = (c0_ref, c1_ref, c2_ref)
    nt_dims = (((1,), (1,)), ((), ()))
    qn = lax.broadcasted_iota(jnp.int32, (t_len, t_len), 0)
    tn = lax.broadcasted_iota(jnp.int32, (t_len, t_len), 1)

    for s in range(sb):
        for h in range(HEADS):
            sc_parts, v_parts = [], []
            for gi, (window, d) in enumerate(DIL_GROUPS):
                c_ref = caches[gi]
                n_pos = c_ref.shape[-1]
                qg = q_ref[s, gi, h].astype(BF16)
                qc = lax.broadcasted_iota(jnp.int32, (t_len, n_pos), 0)
                pc = lax.broadcasted_iota(jnp.int32, (t_len, n_pos), 1)
                sc = _dot(qg, c_ref[s, 0, h].astype(BF16))
                sc_parts.append(jnp.where((((pc - qc) & (d - 1)) == 0) & (pc >= qc), sc, NEG_INF))
                v_parts.append((c_ref[s, 1, h].astype(BF16), True))
                sn = lax.dot_general(qg, kn_ref[s, gi, h].astype(BF16), nt_dims, preferred_element_type=F32)
                sc_parts.append(jnp.where((tn <= qn) & (((qn - tn) & (d - 1)) == 0), sn, NEG_INF))
                v_parts.append((vn_ref[s, gi, h].astype(BF16), False))
            mx = functools.reduce(jnp.maximum, [jnp.max(p, axis=-1, keepdims=True) for p in sc_parts])
            den = jnp.zeros((t_len, 1), F32)
            acc = jnp.zeros((t_len, HEAD_DIM), F32)
            for sc, (v, transposed) in zip(sc_parts, v_parts):
                p = jnp.exp(sc - mx)
                den = den + jnp.sum(p, axis=-1, keepdims=True)
                pb = p.astype(BF16)
                acc = acc + (lax.dot_general(pb, v, nt_dims, preferred_element_type=F32) if transposed
                             else _dot(pb, v))
            yb_ref[s, h] = acc / den
        gbu = gbu_ref[s]
        gb = gbu[:, 0:GROUP_W]
        u = gbu[:, GROUP_W:]
        st = st_ref[s]
        ext = [st[0:1], st[1:2]] + [u[t:t + 1] for t in range(t_len)]
        for t in range(t_len):
            conv = w[0:1] * ext[t] + w[1:2] * ext[t + 1] + w[2:3] * ext[t + 2]
            ya_ref[s, t:t + 1, :] = gb[t:t + 1] * conv
        for r in range(A_KERNEL - 1):
            ns_ref[s, r:r + 1, :] = ext[t_len + r]


def _mix_sample(q5, kn, vn, gbu3, st, c0, c1, c2, w):
    n, t_len = q5.shape[0], q5.shape[3]
    sb = ATTN_SEQ_BLOCK
    assert t_len >= A_KERNEL - 1 and n % sb == 0
    for c, (window, d) in zip((c0, c1, c2), DIL_GROUPS):
        assert c.shape[-1] == NK * d and d & (d - 1) == 0
    qspec = pl.BlockSpec((sb, N_GROUPS, HEADS, t_len, HEAD_DIM), lambda i: (i, 0, 0, 0, 0))
    cspec = lambda c: pl.BlockSpec((sb,) + c.shape[1:], lambda i: (i, 0, 0, 0, 0))
    return pl.pallas_call(
        _mix_sample_kernel,
        grid=(n // sb,),
        in_specs=[
            qspec, qspec, qspec,
            pl.BlockSpec((sb, t_len, 2 * GROUP_W), lambda i: (i, 0, 0)),
            pl.BlockSpec((sb, A_KERNEL - 1, GROUP_W), lambda i: (i, 0, 0)),
            cspec(c0), cspec(c1), cspec(c2),
            pl.BlockSpec((A_KERNEL, GROUP_W), lambda i: (0, 0)),
        ],
        out_specs=[
            pl.BlockSpec((sb, HEADS, t_len, HEAD_DIM), lambda i: (i, 0, 0, 0)),
            pl.BlockSpec((sb, t_len, GROUP_W), lambda i: (i, 0, 0)),
            pl.BlockSpec((sb, A_KERNEL - 1, GROUP_W), lambda i: (i, 0, 0)),
        ],
        out_shape=[
            jax.ShapeDtypeStruct((n, HEADS, t_len, HEAD_DIM), F32),
            jax.ShapeDtypeStruct((n, t_len, GROUP_W), F32),
            jax.ShapeDtypeStruct((n, A_KERNEL - 1, GROUP_W), F32),
        ],
        compiler_params=_params("arbitrary"),
        name="mix_sample",
    )(q5, kn, vn, gbu3, st, c0, c1, c2, w)


def _route(h2, wr_ref, br_ref, xs_ref, rinfo_ref, gtab_ref):
    tm = h2.shape[0]
    h_hi = h2.astype(BF16)
    h_lo = (h2 - h_hi.astype(F32)).astype(BF16)
    nt_dims = (((1,), (1,)), ((), ()))
    w_hi, w_lo = wr_ref[0], wr_ref[1]
    lt = (lax.dot_general(w_hi, h_hi, nt_dims, preferred_element_type=F32)
          + lax.dot_general(w_hi, h_lo, nt_dims, preferred_element_type=F32)
          + lax.dot_general(w_lo, h_hi, nt_dims, preferred_element_type=F32)) + br_ref[...]
    rg = lax.broadcasted_iota(jnp.int32, (SUBLANES, tm), 0)
    gl = jnp.where(rg < N_EXPERT_GROUPS, lt[0:SUBLANES], NEG_INF)
    ge = jnp.exp(gl - jnp.max(gl, axis=0, keepdims=True))
    gp = ge / jnp.sum(ge, axis=0, keepdims=True)
    g_val = jnp.max(gp, axis=0, keepdims=True)
    g_idx = jnp.min(jnp.where(gp == g_val, rg, SUBLANES), axis=0, keepdims=True)
    re = lax.broadcasted_iota(jnp.int32, (N_EXPERTS, tm), 0)
    sel = (re // EXPERTS_PER_GROUP) == g_idx
    el = jnp.where(sel, lt[ROUTER_EXPERT_ROW0:ROUTER_EXPERT_ROW0 + N_EXPERTS], NEG_INF)
    ee = jnp.exp(el - jnp.max(el, axis=0, keepdims=True))
    ep = jnp.where(sel, ee / jnp.sum(ee, axis=0, keepdims=True), -1.0)
    v1 = jnp.max(ep, axis=0, keepdims=True)
    i1 = jnp.min(jnp.where(ep == v1, re, N_EXPERTS), axis=0, keepdims=True)
    ep2 = jnp.where(re == i1, -1.0, ep)
    v2 = jnp.max(ep2, axis=0, keepdims=True)
    i2 = jnp.min(jnp.where(ep2 == v2, re, N_EXPERTS), axis=0, keepdims=True)
    scale = g_val / (v1 + v2)

    oh = [(re == i1).astype(F32), (re == i2).astype(F32)]
    cmat = (oh[0] + oh[1]).astype(BF16)
    earlier = (lax.broadcasted_iota(jnp.int32, (tm, tm), 0)
               < lax.broadcasted_iota(jnp.int32, (tm, tm), 1)).astype(BF16)
    before = _dot(cmat, earlier)
    cnt_row = lax.dot_general(jnp.ones((SUBLANES, tm), BF16), cmat, nt_dims, preferred_element_type=F32)
    grp_row = ((cnt_row.astype(jnp.int32) + (GROUP_ROWS - 1)) // GROUP_ROWS).astype(F32)
    lower = (lax.broadcasted_iota(jnp.int32, (N_EXPERTS, N_EXPERTS), 0)
             < lax.broadcasted_iota(jnp.int32, (N_EXPERTS, N_EXPERTS), 1)).astype(BF16)
    goff_row = _dot(grp_row.astype(BF16), lower)
    slot_rows = lax.broadcasted_iota(jnp.int32, (TILE_SLOTS, tm), 0)
    place = None
    pos = []
    for k in range(2):
        rank = jnp.sum(oh[k] * before, axis=0, keepdims=True)
        seg = _dot(goff_row.astype(BF16), oh[k].astype(BF16))[0:1]
        pos.append(seg * GROUP_ROWS + rank)
        hit = slot_rows == pos[k].astype(jnp.int32)
        place = hit if place is None else place | hit
    xs_ref[...] = _dot(jnp.where(place, 1.0, 0.0).astype(BF16), h_hi).astype(BF16)
    rl = lax.broadcasted_iota(jnp.int32, (LANES, tm), 0)
    info_t = jnp.where(rl == 0, v1 * scale, jnp.where(rl == 1, v2 * scale,
                       jnp.where(rl == 2, pos[0], jnp.where(rl == 3, pos[1], 0.0))))
    rinfo_ref[...] = info_t.T
    spread = (lax.broadcasted_iota(jnp.int32, (N_EXPERTS, LANES), 0)
              == lax.broadcasted_iota(jnp.int32, (N_EXPERTS, LANES), 1)).astype(BF16)
    gtab_ref[...] = _dot(grp_row.astype(BF16), spread).astype(jnp.int32)


ROUTER_EXPERT_ROW0 = SUBLANES
ROUTER_ROWS = 48
TILE_GROUPS = -(-(2 * ROW_TILE + (GROUP_ROWS - 1) * N_EXPERTS) // GROUP_ROWS)
TILE_SLOTS = TILE_GROUPS * GROUP_ROWS


def _router_specs(tm, d):
    ins = [pl.BlockSpec((1, d), lambda i: (0, 0)),
           pl.BlockSpec((2, ROUTER_ROWS, d), lambda i: (0, 0, 0)),
           pl.BlockSpec((ROUTER_ROWS, 1), lambda i: (0, 0))]
    outs = [pl.BlockSpec((TILE_SLOTS, d), lambda i: (i, 0)),
            pl.BlockSpec((tm, LANES), lambda i: (i, 0)),
            pl.BlockSpec((None, SUBLANES, LANES), lambda i: (i, 0, 0))]
    return ins, outs


def _router_shapes(m, d):
    nb = m // ROW_TILE
    return [jax.ShapeDtypeStruct((nb * TILE_SLOTS, d), BF16), jax.ShapeDtypeStruct((m, LANES), F32),
            jax.ShapeDtypeStruct((nb, SUBLANES, LANES), jnp.int32)]


def _outproj_kernel(n_prompt_tiles, tiles_per_seq, xp_ref, xs_ref, gbu_ref, halo_ref, cw_ref, yb_ref, ys_ref,
                    wo_ref, g_ref, wr_ref, br_ref, x1_ref, xs_out_ref, rinfo_ref, gtab_ref, conv_scr):
    i = pl.program_id(0)
    tm = xp_ref.shape[0]
    is_p = i < n_prompt_tiles
    x = jnp.where(is_p, xp_ref[...], xs_ref[...])
    u = gbu_ref[:, GROUP_W:]
    first = (i % tiles_per_seq) == 0
    conv_scr[0:SUBLANES, :] = jnp.where(first, 0.0, halo_ref[:, GROUP_W:])
    conv_scr[SUBLANES:, :] = u
    cw = cw_ref[...]
    ya = gbu_ref[:, 0:GROUP_W] * (cw[0:1, :] * conv_scr[pl.ds(SUBLANES - 2, tm), :]
                                  + cw[1:2, :] * conv_scr[pl.ds(SUBLANES - 1, tm), :] + cw[2:3, :] * u)
    ymix = jnp.where(is_p, jnp.concatenate([ya, _load_chunks(yb_ref)], axis=-1), ys_ref[...])
    x1 = x + _dot(ymix.astype(BF16), wo_ref[...])
    x1_ref[...] = x1
    _route(_rms(x1, g_ref[...]), wr_ref, br_ref, xs_out_ref, rinfo_ref, gtab_ref)


def _outproj(xp, xs, gbu, conv_w, yb_p, ymix_s, wo, g, wr, br, seq):
    mp, d = xp.shape
    ms = xs.shape[0]
    tm = ROW_TILE
    npt, nst = mp // tm, ms // tm
    m = mp + ms
    r_in, r_out = _router_specs(tm, d)
    pmap = lambda i: (jnp.minimum(i, npt - 1), 0)
    smap = lambda i: (jnp.maximum(i - npt, 0), 0)
    halo_map = lambda i: (jnp.maximum(jnp.minimum(i, npt - 1) * (tm // SUBLANES) - 1, 0), 0)
    return pl.pallas_call(
        functools.partial(_outproj_kernel, npt, seq // tm),
        grid=(npt + nst,),
        in_specs=[
            pl.BlockSpec((tm, d), pmap),
            pl.BlockSpec((tm, d), smap),
            pl.BlockSpec((tm, 2 * GROUP_W), pmap),
            pl.BlockSpec((SUBLANES, 2 * GROUP_W), halo_map),
            pl.BlockSpec((A_KERNEL, GROUP_W), lambda i: (0, 0)),
            pl.BlockSpec((GROUP_W // LANES, tm, LANES), lambda i: (0, jnp.minimum(i, npt - 1), 0)),
            pl.BlockSpec((tm, 2 * GROUP_W), smap),
            pl.BlockSpec((2 * GROUP_W, d), lambda i: (0, 0)),
        ] + r_in,
        out_specs=[pl.BlockSpec((tm, d), lambda i: (i, 0))] + r_out,
        out_shape=[jax.ShapeDtypeStruct((m, d), F32)] + _router_shapes(m, d),
        scratch_shapes=[pltpu.VMEM((tm + SUBLANES, GROUP_W), F32)],
        compiler_params=_params("arbitrary"),
        name="outproj_router",
    )(xp, xs, gbu, gbu, conv_w, yb_p, ymix_s, wo, g, wr, br)


def _experts_kernel(te_ref, src_ref, dst_ref, nt_ref, h_hbm, wg_ref, wu_ref, wd_ref, y_hbm, xbuf, ybuf, gsem, ssem,
                    *, trash_row):
    i = pl.program_id(0)
    nt = nt_ref[0]
    ng = CHUNK_GROUPS
    slot = i % 2
    other = 1 - slot

    def group_rows(r):
        return pl.ds(r if isinstance(r, int) else pl.multiple_of(r, GROUP_ROWS), GROUP_ROWS)

    def gather_copy(p, sl, j):
        return pltpu.make_async_copy(h_hbm.at[group_rows(src_ref[p]), :],
                                     xbuf.at[sl, group_rows(j * GROUP_ROWS), :], gsem.at[sl])

    def scatter_copy(p, sl, j):
        return pltpu.make_async_copy(ybuf.at[sl, group_rows(j * GROUP_ROWS), :],
                                     y_hbm.at[group_rows(dst_ref[p]), :], ssem.at[sl])

    def wait_gather(sl):
        pltpu.make_async_copy(h_hbm.at[pl.ds(0, ng * GROUP_ROWS), :], xbuf.at[sl], gsem.at[sl]).wait()

    def wait_scatter(sl):
        pltpu.make_async_copy(ybuf.at[sl], y_hbm.at[pl.ds(0, ng * GROUP_ROWS), :], ssem.at[sl]).wait()

    @pl.when(i == 0)
    def _():
        ybuf[...] = jnp.zeros(ybuf.shape, BF16)
        init = pltpu.make_async_copy(ybuf.at[0], y_hbm.at[pl.ds(trash_row, ng * GROUP_ROWS), :], ssem.at[0])
        init.start()
        init.wait()

        def tail_copy(b):
            first = pl.multiple_of(b * TILE_SLOTS + 2 * ROW_TILE, GROUP_ROWS)
            return pltpu.make_async_copy(ybuf.at[0, pl.ds(0, TILE_SLOTS - 2 * ROW_TILE), :],
                                         y_hbm.at[pl.ds(first, TILE_SLOTS - 2 * ROW_TILE), :], ssem.at[0])

        def start_body(b, c):
            tail_copy(b).start()
            return c

        def wait_body(b, c):
            tail_copy(b).wait()
            return c
        lax.fori_loop(0, trash_row // TILE_SLOTS, start_body, 0)
        lax.fori_loop(0, trash_row // TILE_SLOTS, wait_body, 0)
        for j in range(ng):
            gather_copy(j, 0, j).start()

    @pl.when(i < nt)
    def _():
        wait_gather(slot)

        @pl.when(i >= 1)
        def _():
            wait_scatter(slot)

        base = i * ng
        for j in range(ng):
            gather_copy(base + ng + j, other, j).start(priority=j % 2)
            scatter_copy(base + j, other, j).start(priority=(j + 1) % 2)
        x = xbuf[slot]
        gate = _dot(x, wg_ref[...].astype(BF16))
        up = _dot(x, wu_ref[...].astype(BF16))
        act = (gate * jax.nn.sigmoid(gate) * up).astype(BF16)
        ybuf[slot] = _dot(act, wd_ref[...].astype(BF16)).astype(BF16)

    @pl.when(i == nt)
    def _():
        wait_gather(slot)
        wait_scatter(slot)
        for j in range(ng):
            scatter_copy(i * ng + j, other, j).start()
        wait_scatter(other)


def _experts(xs, tile_expert, src, dst, n_tiles, wg, wu, wd, layer):
    ng = CHUNK_GROUPS
    nt_max = tile_expert.shape[0]
    d, f = wg.shape[2], wg.shape[3]
    n_rows = xs.shape[0] + ng * GROUP_ROWS
    wmap = lambda i, te_ref, src_ref, dst_ref, nt_ref: (layer, te_ref[i], 0, 0)
    grid_spec = pltpu.PrefetchScalarGridSpec(
        num_scalar_prefetch=4,
        grid=(nt_max,),
        in_specs=[
            pl.BlockSpec(memory_space=pl.ANY),
            pl.BlockSpec((None, None, d, f), wmap),
            pl.BlockSpec((None, None, d, f), wmap),
            pl.BlockSpec((None, None, f, d), wmap),
        ],
        out_specs=pl.BlockSpec(memory_space=pl.ANY),
        scratch_shapes=[
            pltpu.VMEM((2, ng * GROUP_ROWS, d), BF16),
            pltpu.VMEM((2, ng * GROUP_ROWS, d), BF16),
            pltpu.SemaphoreType.DMA((2,)),
            pltpu.SemaphoreType.DMA((2,)),
        ],
    )
    return pl.pallas_call(
        functools.partial(_experts_kernel, trash_row=xs.shape[0]),
        grid_spec=grid_spec,
        out_shape=jax.ShapeDtypeStruct((n_rows, d), BF16),
        compiler_params=_params("arbitrary"),
        name="experts",
    )(tile_expert, src, dst, n_tiles, xs, wg, wu, wd)


def _plan_chunks(gtab):
    ng = CHUNK_GROUPS
    grp = gtab[:, 0, :N_EXPERTS]
    nb = grp.shape[0]
    nt_max = -(-nb * TILE_GROUPS // ng) + N_EXPERTS + 1
    total = jnp.sum(grp, axis=0)
    chunks = (total + ng - 1) // ng
    chunk_end = jnp.cumsum(chunks)
    chunk_start = chunk_end - chunks
    n_chunks = chunk_end[-1]
    seg_off = jnp.cumsum(grp, axis=1) - grp
    seg_end = jnp.cumsum(grp, axis=0)
    seg_start = seg_end - grp
    cidx = jnp.arange(nt_max, dtype=jnp.int32)
    chunk_expert = jnp.sum((cidx[:, None] >= chunk_end[None, :]).astype(jnp.int32), axis=1)
    last_expert = jnp.max(jnp.where(total > 0, jnp.arange(N_EXPERTS), 0))
    chunk_expert = jnp.where(cidx < n_chunks, jnp.minimum(chunk_expert, N_EXPERTS - 1), last_expert).astype(jnp.int32)
    is_e = chunk_expert[:, None] == jnp.arange(N_EXPERTS, dtype=jnp.int32)[None, :]
    pick = lambda per_expert: jnp.sum(jnp.where(is_e, per_expert[None, :], 0), axis=1)
    pick_rows = lambda table: jnp.sum(jnp.where(is_e[:, :, None], table.T[None, :, :], 0), axis=1)
    within = (cidx - pick(chunk_start))[:, None] * ng + jnp.arange(ng, dtype=jnp.int32)[None, :]
    valid = (within < pick(total)[:, None]) & (cidx < n_chunks)[:, None]
    ends = pick_rows(seg_end)
    base = pick_rows(jnp.arange(nb, dtype=jnp.int32)[:, None] * TILE_GROUPS + seg_off - seg_start)
    step = jnp.concatenate([base[:, :1], base[:, 1:] - base[:, :-1]], axis=1)
    passed = jnp.concatenate([jnp.ones((nt_max, ng, 1), bool), within[:, :, None] >= ends[:, None, :-1]], axis=2)
    group = within + jnp.sum(jnp.where(passed, step[:, None, :], 0), axis=2)
    pad_dst = (nb * TILE_GROUPS + jnp.arange(ng, dtype=jnp.int32)) * GROUP_ROWS
    src = jnp.where(valid, group * GROUP_ROWS, 0).astype(jnp.int32).reshape(nt_max * ng)
    dst = jnp.where(valid, group * GROUP_ROWS, pad_dst[None, :]).astype(jnp.int32).reshape(nt_max * ng)
    dst = jnp.concatenate([pad_dst, dst])
    return chunk_expert, src, dst, n_chunks.astype(jnp.int32).reshape(1)


def _combine(x_ref, y_ref, rinfo_ref):
    tm = x_ref.shape[0]
    n_slots = y_ref.shape[0]
    r = rinfo_ref[...]
    col = lax.broadcasted_iota(jnp.int32, (tm, n_slots), 1)
    pick = (jnp.where(col == r[:, 2:3].astype(jnp.int32), r[:, 0:1], 0.0)
            + jnp.where(col == r[:, 3:4].astype(jnp.int32), r[:, 1:2], 0.0))
    return x_ref[...] + _dot(pick.astype(BF16), y_ref[...])


def _combine_specs(tm, d):
    return [
        pl.BlockSpec((tm, d), lambda i: (i, 0)),
        pl.BlockSpec((TILE_SLOTS, d), lambda i: (i, 0)),
        pl.BlockSpec((tm, LANES), lambda i: (i, 0)),
    ]


def _combine_pw1_kernel(x_ref, y_ref, rinfo_ref, g_ref, w_ref, b_ref, x2_ref, u_ref):
    x2 = _combine(x_ref, y_ref, rinfo_ref)
    x2_ref[...] = x2
    d = x2.shape[1]
    z = _dot(_rms(x2, g_ref[...]).astype(BF16), w_ref[...]) + b_ref[...]
    u_ref[...] = z[:, :d] * jax.nn.sigmoid(z[:, d:])


def _combine_pw1(x1, y2, rinfo, g, w, b):
    m, d = x1.shape
    tm = ROW_TILE
    return pl.pallas_call(
        _combine_pw1_kernel,
        grid=(m // tm,),
        in_specs=_combine_specs(tm, d) + [
            pl.BlockSpec((1, d), lambda i: (0, 0)),
            pl.BlockSpec((d, 2 * d), lambda i: (0, 0)),
            pl.BlockSpec((1, 2 * d), lambda i: (0, 0)),
        ],
        out_specs=[pl.BlockSpec((tm, d), lambda i: (i, 0)), pl.BlockSpec((tm, d), lambda i: (i, 0))],
        out_shape=[jax.ShapeDtypeStruct((m, d), F32), jax.ShapeDtypeStruct((m, d), F32)],
        compiler_params=_params("arbitrary"),
        name="combine_pw1",
    )(x1, y2, rinfo, g, w, b)


def _combine_final_kernel(n_prompt_tiles, x_ref, y_ref, rinfo_ref, g_ref, yp_ref, ys_ref):
    i = pl.program_id(0)
    y = _rms(_combine(x_ref, y_ref, rinfo_ref), g_ref[...])

    @pl.when(i < n_prompt_tiles)
    def _():
        yp_ref[...] = y

    @pl.when(i >= n_prompt_tiles)
    def _():
        ys_ref[...] = y


def _combine_final(x3, y2, rinfo, g, mp):
    m, d = x3.shape
    tm = ROW_TILE
    npt = mp // tm
    return pl.pallas_call(
        functools.partial(_combine_final_kernel, npt),
        grid=(m // tm,),
        in_specs=_combine_specs(tm, d) + [pl.BlockSpec((1, d), lambda i: (0, 0))],
        out_specs=[pl.BlockSpec((tm, d), lambda i: (jnp.minimum(i, npt - 1), 0)),
                   pl.BlockSpec((tm, d), lambda i: (jnp.maximum(i - npt, 0), 0))],
        out_shape=[jax.ShapeDtypeStruct((mp, d), F32), jax.ShapeDtypeStruct((m - mp, d), F32)],
        compiler_params=_params("arbitrary"),
        name="combine_final",
    )(x3, y2, rinfo, g)


def _dwconv_prompt_tile(first, u_ref, halo_ref, w_ref, b_ref, scr):
    tm = u_ref.shape[0]
    scr[0, 0:CONV_HALO, :] = jnp.where(first, 0.0, halo_ref[...])
    scr[0, CONV_HALO:, :] = u_ref[...]
    n_shift = tm + CONV_HALO - SUBLANES
    for s in range(1, SUBLANES):
        scr[s, 0:n_shift, :] = scr[0, pl.ds(s, n_shift), :]
    acc = jnp.broadcast_to(b_ref[...], u_ref.shape)
    for k in range(C_KERNEL):
        off = CONV_HALO - (C_KERNEL - 1) + k
        acc = acc + w_ref[k:k + 1, :] * scr[off % SUBLANES, pl.ds(off - off % SUBLANES, tm), :]
    return acc


def _dwconv_prompt_kernel(tiles_per_seq, u_ref, halo_ref, w_ref, b_ref, c_ref, scr):
    first = (pl.program_id(0) % tiles_per_seq) == 0
    c_ref[...] = _dwconv_prompt_tile(first, u_ref, halo_ref, w_ref, b_ref, scr)


def _dwconv_prompt(u, w, b, mp, seq):
    tm = ROW_TILE
    d = u.shape[1]
    return pl.pallas_call(
        functools.partial(_dwconv_prompt_kernel, seq // tm),
        grid=(mp // tm,),
        in_specs=[
            pl.BlockSpec((tm, d), lambda i: (i, 0)),
            pl.BlockSpec((CONV_HALO, d), lambda i: (jnp.maximum(i * (tm // CONV_HALO) - 1, 0), 0)),
            pl.BlockSpec((C_KERNEL, d), lambda i: (0, 0)),
            pl.BlockSpec((1, d), lambda i: (0, 0)),
        ],
        out_specs=pl.BlockSpec((tm, d), lambda i: (i, 0)),
        out_shape=jax.ShapeDtypeStruct((mp, d), F32),
        scratch_shapes=[pltpu.VMEM((SUBLANES, tm + CONV_HALO, d), F32)],
        compiler_params=_params("arbitrary"),
        name="dwconv_prompt",
    )(u, u, w, b)


def _dwconv_sample_kernel(st_ref, u_ref, w_ref, b_ref, c_ref, ns_ref):
    n_state, t_len = st_ref.shape[0], u_ref.shape[0]
    rows = [st_ref[r] for r in range(n_state)] + [u_ref[t] for t in range(t_len)]
    for t in range(t_len):
        acc = jnp.broadcast_to(b_ref[...], rows[0].shape)
        for k in range(C_KERNEL):
            acc = acc + w_ref[k:k + 1, :] * rows[t + k]
        c_ref[t] = acc
    for r in range(n_state):
        ns_ref[r] = rows[r + t_len]


def _dwconv_sample(st, u3, w, b):
    n_state, n, d = st.shape
    t_len = u3.shape[0]
    sb = SEQ_BLOCK
    assert n_state == C_KERNEL - 1 and n % sb == 0
    return pl.pallas_call(
        _dwconv_sample_kernel,
        grid=(n // sb,),
        in_specs=[
            pl.BlockSpec((n_state, sb, d), lambda i: (0, i, 0)),
            pl.BlockSpec((t_len, sb, d), lambda i: (0, i, 0)),
            pl.BlockSpec((C_KERNEL, d), lambda i: (0, 0)),
            pl.BlockSpec((1, d), lambda i: (0, 0)),
        ],
        out_specs=[
            pl.BlockSpec((t_len, sb, d), lambda i: (0, i, 0)),
            pl.BlockSpec((n_state, sb, d), lambda i: (0, i, 0)),
        ],
        out_shape=[jax.ShapeDtypeStruct((t_len, n, d), F32), jax.ShapeDtypeStruct((n_state, n, d), F32)],
        compiler_params=_params("arbitrary"),
        name="dwconv_sample",
    )(st, u3, w, b)


def _conf_tail_kernel(n_prompt_tiles, x_ref, cp_ref, cs_ref, lg_ref, lb_ref, w_ref, b_ref, g_ref, wr_ref, br_ref,
                      x3_ref, xs_out_ref, rinfo_ref, gtab_ref):
    i = pl.program_id(0)
    c = jnp.where(i < n_prompt_tiles, cp_ref[...], cs_ref[...])
    mu = jnp.mean(c, axis=-1, keepdims=True)
    cc = c - mu
    var = jnp.mean(cc * cc, axis=-1, keepdims=True)
    y = cc * lax.rsqrt(var + LN_EPS) * lg_ref[...] + lb_ref[...]
    y = y * jax.nn.sigmoid(y)
    x3 = x_ref[...] + _dot(y.astype(BF16), w_ref[...]) + b_ref[...]
    x3_ref[...] = x3
    _route(_rms(x3, g_ref[...]), wr_ref, br_ref, xs_out_ref, rinfo_ref, gtab_ref)


def _conf_tail(x2, c_p, c_s, ln_g, ln_b, w, b, g, wr, br):
    m, d = x2.shape
    tm = ROW_TILE
    npt = c_p.shape[0] // tm
    r_in, r_out = _router_specs(tm, d)
    vec = pl.BlockSpec((1, d), lambda i: (0, 0))
    return pl.pallas_call(
        functools.partial(_conf_tail_kernel, npt),
        grid=(m // tm,),
        in_specs=[
            pl.BlockSpec((tm, d), lambda i: (i, 0)),
            pl.BlockSpec((tm, d), lambda i: (jnp.minimum(i, npt - 1), 0)),
            pl.BlockSpec((tm, d), lambda i: (jnp.maximum(i - npt, 0), 0)),
            vec, vec,
            pl.BlockSpec((d, d), lambda i: (0, 0)),
            vec,
        ] + r_in,
        out_specs=[pl.BlockSpec((tm, d), lambda i: (i, 0))] + r_out,
        out_shape=[jax.ShapeDtypeStruct((m, d), F32)] + _router_shapes(m, d),
        compiler_params=_params("arbitrary"),
        name="conf_tail_router",
    )(x2, c_p, c_s, ln_g, ln_b, w, b, g, wr, br)


def _router_weights(w_rg, b_rg, w_re, b_re):
    d = w_rg.shape[0]
    gpad = ROUTER_EXPERT_ROW0 - N_EXPERT_GROUPS
    epad = ROUTER_ROWS - ROUTER_EXPERT_ROW0 - N_EXPERTS
    wt = jnp.concatenate([w_rg.T, jnp.zeros((gpad, d), F32), w_re.T, jnp.zeros((epad, d), F32)], axis=0)
    w_hi = wt.astype(BF16)
    w_lo = (wt - w_hi.astype(F32)).astype(BF16)
    br = jnp.concatenate([b_rg, jnp.zeros((gpad,), F32), b_re, jnp.zeros((epad,), F32)])[:, None]
    return jnp.stack([w_hi, w_lo]), br


def _moe(xs, gtab, wg, wu, wd, layer):
    chunk_expert, src, dst, n_chunks = _plan_chunks(gtab)
    return _experts(xs, chunk_expert, src, dst, n_chunks, wg, wu, wd, layer)


def kernel(x_prompt, x_sample, state_shortconv, cache_kv_w128, cache_kv_w512, cache_kv_w2048, state_conformer,
           g_mix, g_ffn, g_final, w_in, conv_a_w, w_out, w_pw1, b_pw1, dw_w, dw_b, ln_g, ln_b, w_pw2, b_pw2,
           w_router_group, b_router_group, w_router_expert, b_router_expert, w_gate, w_up, w_down):
    batch, seq, d = x_prompt.shape
    n_dec, t_dec, _ = x_sample.shape
    mp, ms = batch * seq, n_dec * t_dec
    assert g_mix.shape[0] == 2 and mp % ROW_TILE == 0 and ms % ROW_TILE == 0 and seq % ROW_TILE == 0
    xp = x_prompt.reshape(mp, d)
    xs = x_sample.reshape(ms, d)
    slab = (HEADS, HEAD_DIM)

    aw, qw = 3 * GROUP_W, N_GROUPS * GROUP_W
    w0 = w_in[0]
    kcols = w0[:, aw + qw:aw + 2 * qw].reshape(d, N_GROUPS, GROUP_W)
    vcols = w0[:, aw + 2 * qw:].reshape(d, N_GROUPS, GROUP_W)
    w_perm = jnp.concatenate([w0[:, :aw + qw], jnp.stack([kcols, vcols], axis=2).reshape(d, 2 * qw)], axis=1)
    q, kv, gbu = _inproj(xp, xs, g_mix[0][None, :], w_perm.astype(BF16))

    yb_p = _attn_prompt(q, kv, batch, seq)

    hpc = LANES // HEAD_DIM
    q_s = jnp.transpose(q[:, mp:].reshape(N_GROUPS, HEADS // hpc, n_dec, t_dec, hpc, HEAD_DIM),
                        (2, 0, 1, 4, 3, 5)).reshape(n_dec, N_GROUPS, HEADS, t_dec, HEAD_DIM)
    kv_s = jnp.transpose(kv[:, mp:].reshape(N_GROUPS, 2, HEADS // hpc, n_dec, t_dec, hpc, HEAD_DIM),
                         (1, 3, 0, 2, 5, 4, 6)).reshape(2, n_dec, N_GROUPS, HEADS, t_dec, HEAD_DIM)
    gbu_s = gbu[mp:].reshape(n_dec, t_dec, 2 * GROUP_W)
    c0, c1, c2 = (jnp.transpose(c[0], (0, 2, 3, 4, 1)) for c in (cache_kv_w128, cache_kv_w512, cache_kv_w2048))
    yb_s, ya_s, s_sc = _mix_sample(q_s, kv_s[0], kv_s[1], gbu_s, state_shortconv[0], c0, c1, c2, conv_a_w[0])
    ymix_s = jnp.concatenate([ya_s.reshape(ms, GROUP_W),
                              jnp.transpose(yb_s, (0, 2, 1, 3)).reshape(ms, GROUP_W)], axis=-1)

    wr0, br0 = _router_weights(w_router_group[0], b_router_group[0], w_router_expert[0], b_router_expert[0])
    x1, xsort0, rinfo0, gtab0 = _outproj(xp, xs, gbu, conv_a_w[0], yb_p, ymix_s, w_out[0].astype(BF16),
                                         g_ffn[0][None, :], wr0, br0, seq)
    y2 = _moe(xsort0, gtab0, w_gate, w_up, w_down, 0)

    x2, u = _combine_pw1(x1, y2, rinfo0, g_mix[1][None, :], w_pw1[0].astype(BF16), b_pw1[0][None, :])
    c_s3, s_cf = _dwconv_sample(jnp.transpose(state_conformer[0], (1, 0, 2)),
                                jnp.transpose(u[mp:].reshape(n_dec, t_dec, d), (1, 0, 2)), dw_w[0], dw_b[0][None, :])
    c_s = jnp.transpose(c_s3, (1, 0, 2)).reshape(ms, d)
    wr1, br1 = _router_weights(w_router_group[1], b_router_group[1], w_router_expert[1], b_router_expert[1])
    c_p = _dwconv_prompt(u, dw_w[0], dw_b[0][None, :], mp, seq)
    x3, xsort1, rinfo1, gtab1 = _conf_tail(x2, c_p, c_s, ln_g[0][None, :], ln_b[0][None, :],
                                           w_pw2[0].astype(BF16), b_pw2[0][None, :], g_ffn[1][None, :], wr1, br1)
    y2b = _moe(xsort1, gtab1, w_gate, w_up, w_down, 1)
    y_p, y_s = _combine_final(x3, y2b, rinfo1, g_final[None, :], mp)

    cpg = 2 * GROUP_W // LANES

    def tail_rows(arr, n, axis):
        return jnp.stack([lax.slice_in_dim(arr, (b + 1) * seq - n, (b + 1) * seq, axis=axis) for b in range(batch)])

    p_kv = []
    for gi, (w, _) in enumerate(DIL_GROUPS):
        rows = tail_rows(kv[gi * cpg:(gi + 1) * cpg], min(w, seq), 1)
        p_kv.append(jnp.transpose(rows, (0, 2, 1, 3)).reshape(1, batch, min(w, seq), 2, *slab))
    s_kv = [jnp.transpose(kv_s[:, :, gi], (1, 3, 0, 2, 4))[None] for gi in range(N_GROUPS)]
    p_sc = tail_rows(gbu, A_KERNEL - 1, 0)[:, :, GROUP_W:][None]
    p_cf = tail_rows(u, C_KERNEL - 1, 0)[None]
    return (y_p.reshape(batch, seq, d), y_s.reshape(n_dec, t_dec, d), p_sc, p_kv[0], p_kv[1], p_kv[2], p_cf,
            s_sc[None], s_kv[0], s_kv[1], s_kv[2], jnp.transpose(s_cf, (1, 0, 2))[None])
```
